```python
import math
import jax, jax.numpy as jnp
from jax import lax
import numpy as np

D_MODEL = 1024
BATCH = 16
SEQ = 4096
DEPTH = 1

HEAD_DIM = 64
N_HEADS_A = 8
N_HEADS_B = 8
N_HEADS = N_HEADS_A + N_HEADS_B
D_A = N_HEADS_A * HEAD_DIM
D_B = N_HEADS_B * HEAD_DIM
D_MIX = D_A + D_B
N_IDX_HEADS = 4
IDX_DIM = 64
TOPK_MAX = 256
DIL_PATTERNS = ((128, 1), (512, 4), (2048, 16))
BLOCK = 128
N_BUCKETS = 32
MAX_DISTANCE = 128
EPS = 1e-6
SPLITS = (D_A, D_A, D_A, D_A, N_IDX_HEADS * IDX_DIM, IDX_DIM, N_IDX_HEADS, D_B, D_B, D_B, D_B)
D_IN = sum(SPLITS)

kernel_name = "hybrid_dsa_dilated_parallel_heads"


def rms_norm(x, g):
    xf = x.astype(jnp.float32)
    y = xf * lax.rsqrt(jnp.mean(xf * xf, axis=-1, keepdims=True) + EPS) * g.astype(jnp.float32)
    return y.astype(x.dtype)


def rel_bucket(dist):
    max_exact = N_BUCKETS // 2
    d = jnp.maximum(dist, 0)
    df = jnp.maximum(d, 1).astype(jnp.float32)
    large = max_exact + (jnp.log(df / max_exact) / math.log(MAX_DISTANCE / max_exact)
                         * (N_BUCKETS - max_exact)).astype(jnp.int32)
    large = jnp.minimum(large, N_BUCKETS - 1)
    return jnp.where(d < max_exact, d, large)


def dsa_attention(q, k, v, q_idx, k_idx, w_idx, bias_table):
    b, s, h, dh = q.shape
    topk = min(TOPK_MAX, s // 4)
    nblk = s // BLOCK
    scale = HEAD_DIM ** -0.5
    idx_scale = (N_IDX_HEADS * IDX_DIM) ** -0.5
    key_pos = jnp.arange(s)

    def to_blocks(a):
        return jnp.moveaxis(a.reshape(b, nblk, BLOCK, *a.shape[2:]), 1, 0)

    def block_fn(args):
        qb, qib, wb, start = args
        q_pos = start + jnp.arange(BLOCK)
        rel = jnp.einsum('bqhd,bsd->bqhs', qib, k_idx)
        score = jnp.einsum('bqhs,bqh->bqs', jax.nn.relu(rel).astype(jnp.float32),
                           wb.astype(jnp.float32)) * idx_scale
        causal = key_pos[None, :] <= q_pos[:, None]
        score = jnp.where(causal[None], score, -jnp.inf)
        top_val, top_idx = lax.top_k(score, topk)
        valid = jnp.isfinite(top_val)
        k_sel = jax.vmap(lambda kk, ii: kk[ii])(k, top_idx)
        v_sel = jax.vmap(lambda vv, ii: vv[ii])(v, top_idx)
        logits = jnp.einsum('bqhd,bqkhd->bqhk', qb, k_sel).astype(jnp.float32) * scale
        bucket = rel_bucket(q_pos[None, :, None] - top_idx)
        bias = jnp.moveaxis(bias_table[bucket], -1, -2).astype(jnp.float32)
        logits = jnp.where(valid[:, :, None, :], logits + bias, -jnp.inf)
        p = jax.nn.softmax(logits, axis=-1)
        return jnp.einsum('bqhk,bqkhd->bqhd', p.astype(v.dtype), v_sel)

    starts = jnp.arange(nblk) * BLOCK
    out = lax.map(block_fn, (to_blocks(q), to_blocks(q_idx), to_blocks(w_idx), starts))
    return jnp.moveaxis(out, 0, 1).reshape(b, s, h, dh)


def dilated_pattern(q, k, v, window, dilation, bias_table):
    b, s, h, dh = q.shape
    seg = dilation * BLOCK
    p_len = -(-s // seg) * seg
    pad = p_len - s
    n_sub = p_len // dilation
    nb = n_sub // BLOCK
    steps = window // dilation
    scale = HEAD_DIM ** -0.5

    def to_segments(a):
        a = jnp.pad(a, ((0, 0), (0, pad), (0, 0), (0, 0)))
        a = a.reshape(b, n_sub, dilation, h, dh).transpose(2, 0, 1, 3, 4)
        return a.reshape(dilation, b, nb, BLOCK, h, dh)

    def band(a):
        prev = jnp.pad(a, ((0, 0), (0, 0), (1, 0), (0, 0), (0, 0), (0, 0)))[:, :, :-1]
        return jnp.concatenate([prev, a], axis=3)

    def seg_major(a):
        return jnp.moveaxis(a, 2, 1).reshape(dilation * nb, b, *a.shape[3:])

    qs, ks, vs = to_segments(q), to_segments(k), to_segments(v)
    qs, kb, vb = seg_major(qs), seg_major(band(ks)), seg_major(band(vs))
    blk_ids = jnp.arange(dilation * nb) % nb

    a_idx = jnp.arange(BLOCK)[:, None]
    c_idx = jnp.arange(2 * BLOCK)[None, :]
    diff = a_idx - c_idx + BLOCK
    band_mask = (diff >= 0) & (diff <= steps)
    bias = jnp.transpose(bias_table[rel_bucket(diff * dilation)], (2, 0, 1)).astype(jnp.float32)

    def seg_fn(args):
        qq, kk, vv, blk = args
        key_ok = (blk * BLOCK - BLOCK + jnp.arange(2 * BLOCK)) >= 0
        mask = band_mask & key_ok[None, :]
        logits = jnp.einsum('bqhd,bkhd->bhqk', qq, kk).astype(jnp.float32) * scale + bias
        logits = jnp.where(mask, logits, -jnp.inf)
        m = jnp.max(logits, axis=-1)
        pexp = jnp.exp(logits - m[..., None])
        den = jnp.sum(pexp, axis=-1)
        num = jnp.einsum('bhqk,bkhd->bqhd', pexp, vv.astype(jnp.float32))
        return jnp.swapaxes(m, 1, 2), jnp.swapaxes(den, 1, 2), num

    m, den, num = lax.map(seg_fn, (qs, kb, vb, blk_ids))

    def unseg(a):
        rest = a.shape[3:]
        a = a.reshape(dilation, nb, b, BLOCK, *rest)
        a = jnp.moveaxis(jnp.moveaxis(a, 0, 3), 0, 1)
        return a.reshape(b, p_len, *rest)[:, :s]

    return unseg(m), unseg(den), unseg(num)


def dilated_attention(q, k, v, bias_table):
    stats = [dilated_pattern(q, k, v, w, d, bias_table) for (w, d) in DIL_PATTERNS]
    m_all = jnp.stack([st[0] for st in stats])
    wts = jnp.exp(m_all - jnp.max(m_all, axis=0))
    den = sum(wts[i] * stats[i][1] for i in range(len(stats)))
    num = sum(wts[i][..., None] * stats[i][2] for i in range(len(stats)))
    return (num / den[..., None]).astype(q.dtype)


def setup_inputs(seed: int = 0) -> dict:
    key = jax.random.key(seed)
    ks = jax.random.split(key, 9)
    x = jax.random.normal(ks[0], (BATCH, SEQ, D_MODEL), jnp.float32)
    norm_gain = 1.0 + 0.02 * jax.random.normal(ks[1], (DEPTH, D_MODEL), jnp.float32)
    w_in = jax.random.normal(ks[2], (DEPTH, D_MODEL, D_IN), jnp.float32) * D_MODEL ** -0.5
    w_out = jax.random.normal(ks[3], (DEPTH, D_MIX, D_MODEL), jnp.float32) * D_MIX ** -0.5
    rel_bias = 0.1 * jax.random.normal(ks[4], (N_BUCKETS, N_HEADS), jnp.float32)
    q_norm_a = 1.0 + 0.02 * jax.random.normal(ks[5], (DEPTH, HEAD_DIM), jnp.float32)
    k_norm_a = 1.0 + 0.02 * jax.random.normal(ks[6], (DEPTH, HEAD_DIM), jnp.float32)
    q_norm_b = 1.0 + 0.02 * jax.random.normal(ks[7], (DEPTH, HEAD_DIM), jnp.float32)
    k_norm_b = 1.0 + 0.02 * jax.random.normal(ks[8], (DEPTH, HEAD_DIM), jnp.float32)
    return {"x": x, "norm_gain": norm_gain, "w_in": w_in, "w_out": w_out, "rel_bias": rel_bias,
            "q_norm_a": q_norm_a, "k_norm_a": k_norm_a, "q_norm_b": q_norm_b, "k_norm_b": k_norm_b}


def reference(x, norm_gain, w_in, w_out, rel_bias, q_norm_a, k_norm_a, q_norm_b, k_norm_b):
    b, s, _ = x.shape
    split_points = np.cumsum(SPLITS)[:-1].tolist()
    bias_a = rel_bias[:, :N_HEADS_A]
    bias_b = rel_bias[:, N_HEADS_A:]
    for layer in range(DEPTH):
        xn = rms_norm(x, norm_gain[layer])
        proj = jnp.einsum('bsd,de->bse', xn, w_in[layer])
        qa, ka, va, za, qi, ki, wi, qb, kb, vb, zb = jnp.split(proj, split_points, axis=-1)
        qa = rms_norm(qa.reshape(b, s, N_HEADS_A, HEAD_DIM), q_norm_a[layer])
        ka = rms_norm(ka.reshape(b, s, N_HEADS_A, HEAD_DIM), k_norm_a[layer])
        va = va.reshape(b, s, N_HEADS_A, HEAD_DIM)
        qi = qi.reshape(b, s, N_IDX_HEADS, IDX_DIM)
        qb = rms_norm(qb.reshape(b, s, N_HEADS_B, HEAD_DIM), q_norm_b[layer])
        kb = rms_norm(kb.reshape(b, s, N_HEADS_B, HEAD_DIM), k_norm_b[layer])
        vb = vb.reshape(b, s, N_HEADS_B, HEAD_DIM)
        a_out = dsa_attention(qa, ka, va, qi, ki, wi, bias_a)
        b_out = dilated_attention(qb, kb, vb, bias_b)
        mixed = jnp.concatenate([a_out.reshape(b, s, D_A) * jax.nn.silu(za),
                                 b_out.reshape(b, s, D_B) * jax.nn.silu(zb)], axis=-1)
        x = x + jnp.einsum('bse,ed->bsd', mixed, w_out[layer])
    return x
```

```python
import functools
import math

import jax
import jax.numpy as jnp
from jax import lax
from jax.experimental import pallas as pl
from jax.experimental.pallas import tpu as pltpu

HEAD_DIM = 64
N_HEADS_A = 8
N_HEADS_B = 8
D_A = N_HEADS_A * HEAD_DIM
D_B = N_HEADS_B * HEAD_DIM
N_IDX_HEADS = 4
IDX_DIM = 64
TOPK_MAX = 256
DIL_PATTERNS = ((128, 1), (512, 4), (2048, 16))
BLOCK = 128
N_BUCKETS = 32
MAX_DISTANCE = 128
EPS = 1e-6

LANES = 128
NEG_BIG = -1e30
VMEM_LIMIT = 56 * 1024 * 1024

F32 = jnp.float32
BF16 = jnp.bfloat16
NT_DIMS = (((1,), (1,)), ((), ()))


def _dot(a, b):
    return jnp.dot(a, b, preferred_element_type=F32)


def _dot_nt(a, b):
    return lax.dot_general(a, b, NT_DIMS, preferred_element_type=F32)


def _rel_bucket(dist):
    max_exact = N_BUCKETS // 2
    d = jnp.maximum(dist, 0)
    df = jnp.maximum(d, 1).astype(F32)
    large = max_exact + (jnp.log(df / max_exact) / math.log(MAX_DISTANCE / max_exact)
                         * (N_BUCKETS - max_exact)).astype(jnp.int32)
    large = jnp.minimum(large, N_BUCKETS - 1)
    return jnp.where(d < max_exact, d, large)


def _inproj_kernel(x_ref, g_ref, w_ref, bd_ref, gn_ref,
                   qa_ref, ka_ref, va_ref, ga_ref, qi_ref, ki_ref, wi_ref,
                   qb_ref, kb_ref, vb_ref, gb_ref):
    x = x_ref[...]
    r = lax.rsqrt(jnp.mean(x * x, axis=-1, keepdims=True) + EPS)
    xn = (x * r * g_ref[...]).astype(BF16)
    bd = bd_ref[...]

    def head_norm(q, gain, scale):
        ss = _dot((q * q).astype(BF16), bd)
        return q * lax.rsqrt(ss * (1.0 / HEAD_DIM) + EPS) * (gain * scale)

    def branch(col0, gq, gk, q_ref, k_ref, v_ref, gate_ref):
        p = _dot(xn, w_ref[:, col0:col0 + 4 * D_A])
        q = head_norm(p[:, 0:D_A], gq, HEAD_DIM ** -0.5)
        k = head_norm(p[:, D_A:2 * D_A], gk, 1.0)
        z = p[:, 3 * D_A:4 * D_A]
        q_ref[...] = q.astype(BF16)
        k_ref[...] = k.astype(BF16)
        v_ref[...] = p[:, 2 * D_A:3 * D_A].astype(BF16)
        gate_ref[...] = (z * jax.nn.sigmoid(z)).astype(BF16)

    branch(0, gn_ref[0:1, :], gn_ref[1:2, :], qa_ref, ka_ref, va_ref, ga_ref)
    pi = _dot(xn, w_ref[:, 4 * D_A:4 * D_A + 4 * LANES])
    qi_ref[...] = pi[:, 0:2 * LANES].astype(BF16)
    ki_ref[...] = pi[:, 2 * LANES:3 * LANES].astype(BF16)
    wi_ref[...] = pi[:, 3 * LANES:4 * LANES]
    branch(4 * D_A + 4 * LANES, gn_ref[2:3, :], gn_ref[3:4, :], qb_ref, kb_ref, vb_ref, gb_ref)


def _inproj(x2, g, wp, bd, gn, tm):
    n, d = x2.shape
    ncol = wp.shape[1]
    const = lambda i: (0, 0)
    row = lambda i: (i, 0)
    one_buf = dict(pipeline_mode=pl.Buffered(1))
    out_bf = lambda c: jax.ShapeDtypeStruct((n, c), BF16)
    out_shapes = (out_bf(D_A), out_bf(D_A), out_bf(D_A), out_bf(D_A),
                  out_bf(2 * LANES), out_bf(LANES), jax.ShapeDtypeStruct((n, LANES), F32),
                  out_bf(D_B), out_bf(D_B), out_bf(D_B), out_bf(D_B))
    spec = lambda c: pl.BlockSpec((tm, c), row)
    out_specs = (spec(D_A), spec(D_A), spec(D_A), spec(D_A), spec(2 * LANES), spec(LANES), spec(LANES),
                 spec(D_B), spec(D_B), spec(D_B), spec(D_B))
    return pl.pallas_call(
        _inproj_kernel,
        grid=(n // tm,),
        in_specs=[pl.BlockSpec((tm, d), row),
                  pl.BlockSpec((1, d), const, **one_buf),
                  pl.BlockSpec((d, ncol), const, **one_buf),
                  pl.BlockSpec((D_A, D_A), const, **one_buf),
                  pl.BlockSpec((4, D_A), const, **one_buf)],
        out_specs=out_specs,
        out_shape=out_shapes,
        compiler_params=pltpu.CompilerParams(dimension_semantics=("parallel",),
                                             vmem_limit_bytes=VMEM_LIMIT),
        name="inproj",
    )(x2, g, wp, bd, gn)


def _ordered_bits_to_float(u):
    key = u ^ jnp.int32(-2 ** 31)
    bits = key ^ ((key >> 31) & jnp.int32(0x7FFFFFFF))
    return lax.bitcast_convert_type(bits, F32)


def _dsa_kernel(topk, tq, nq, qa_ref, ka_ref, va_ref, qi_ref, ki_ref, wi_ref, ga_ref, bias_ref,
                out_ref, sc_ref, m_ref, l_ref, acc_ref, jt_ref):
    tk = tq
    i = pl.program_id(1)
    nchunks = i + 1
    idx_scale = (N_IDX_HEADS * IDX_DIM) ** -0.5
    lane = lax.broadcasted_iota(jnp.int32, (tq, LANES), 1)
    low_half = lane < HEAD_DIM

    def split_pair(ref, npairs):
        out = []
        for p in range(npairs):
            blk = ref[0, :, p * LANES:(p + 1) * LANES]
            zero = jnp.zeros_like(blk)
            out.append(jnp.where(low_half, blk, zero))
            out.append(jnp.where(low_half, zero, blk))
        return out

    qiz = split_pair(qi_ref, N_IDX_HEADS // 2)
    w = wi_ref[0]
    row_g = i * tq + lax.broadcasted_iota(jnp.int32, (tq, tk), 0)
    col_l = lax.broadcasted_iota(jnp.int32, (tq, tk), 1)

    def score_chunk(c, carry):
        kc = ki_ref[0, pl.ds(pl.multiple_of(c * tk, tk), tk), :]
        acc = jnp.zeros((tq, tk), F32)
        for h in range(N_IDX_HEADS):
            rel = _dot_nt(qiz[h], kc)
            acc = acc + jnp.maximum(rel, 0.0) * w[:, h:h + 1]
        s = acc * idx_scale + 0.0
        s = jnp.where(col_l + c * tk <= row_g, s, -jnp.inf)
        sc_ref[c] = s
        return carry

    lax.fori_loop(0, nchunks, score_chunk, 0)

    ngrp = tk // LANES

    def count_pass(pred):
        def body(c, acc):
            for g in range(ngrp):
                s = sc_ref[c, :, g * LANES:(g + 1) * LANES]
                acc = acc + jnp.where(pred(s, c, g), 1.0, 0.0)
            return acc
        acc = lax.fori_loop(0, nchunks, body, jnp.zeros((tq, LANES), F32))
        return jnp.sum(acc, axis=1, keepdims=True)

    def bit_step(it, cur):
        cand = cur | lax.shift_left(jnp.int32(1), 31 - it)
        candf = _ordered_bits_to_float(cand)
        cnt = count_pass(lambda s, c, g: s >= candf)
        return jnp.where(cnt >= float(topk), cand, cur)

    cur = lax.fori_loop(0, 32, bit_step, jnp.zeros((tq, LANES), jnp.int32))
    short = (cur & jnp.int32(-2 ** 23)) == 0
    thr = jnp.where(short, -jnp.inf, _ordered_bits_to_float(cur))
    cnt_gt = count_pass(lambda s, c, g: s > thr)
    cnt_ge = count_pass(lambda s, c, g: s >= thr)
    need_m1 = float(topk) - cnt_gt - 1.0
    short1 = short[:, 0:1]
    tie_excess = jnp.where(short1, 0.0, cnt_ge - float(topk))

    jt_ref[...] = jnp.where(short, -1, jnp.int32(2 ** 30))

    @pl.when(jnp.max(tie_excess) > 0.0)
    def _():
        nbits = max(1, (nq * tk - 1).bit_length())

        def tie_step(it, curj):
            cand = curj | lax.shift_left(jnp.int32(1), nbits - 1 - it)

            def pred(s, c, g):
                return (s == thr) & (lane < cand - (c * tk + g * LANES))
            cnt = count_pass(pred)
            return jnp.where(cnt <= need_m1, cand, curj)

        curj = lax.fori_loop(0, nbits, tie_step, jnp.zeros((tq, LANES), jnp.int32))
        jt_ref[...] = jnp.where(short, -1, curj)

    jt = jt_ref[...]

    qz = split_pair(qa_ref, N_HEADS_A // 2)
    m_ref[...] = jnp.full(m_ref.shape, NEG_BIG, F32)
    l_ref[...] = jnp.zeros(l_ref.shape, F32)
    acc_ref[...] = jnp.zeros(acc_ref.shape, F32)

    def attn_chunk(c, carry):
        k0 = pl.multiple_of(c * tk, tk)
        variant = jnp.clip(c - i + 2, 0, 2)
        madd = []
        for g in range(ngrp):
            s = sc_ref[c, :, g * LANES:(g + 1) * LANES]
            sel = (s > thr) | ((s == thr) & (lane <= jt - (c * tk + g * LANES)))
            madd.append(jnp.where(sel, 0.0, NEG_BIG))
        madd = jnp.concatenate(madd, axis=1)
        for p in range(N_HEADS_A // 2):
            kp = ka_ref[0, pl.ds(k0, tk), p * LANES:(p + 1) * LANES]
            vp = va_ref[0, pl.ds(k0, tk), p * LANES:(p + 1) * LANES]
            upd = []
            for e in range(2):
                h = 2 * p + e
                logits = _dot_nt(qz[h], kp) + bias_ref[variant, h] + madd
                m_old = m_ref[h]
                m_new = jnp.maximum(m_old, jnp.max(logits, axis=1, keepdims=True))
                alpha = jnp.exp(m_old - m_new)
                pr = jnp.exp(logits - m_new[:, 0:1])
                l_ref[h] = alpha * l_ref[h] + jnp.sum(pr, axis=1, keepdims=True)
                m_ref[h] = m_new
                upd.append((alpha, _dot(pr.astype(BF16), vp)))
            alpha2 = jnp.where(low_half, upd[0][0], upd[1][0])
            pv2 = jnp.where(low_half, upd[0][1], upd[1][1])
            acc_ref[p] = alpha2 * acc_ref[p] + pv2
        return carry

    lax.fori_loop(0, nchunks, attn_chunk, 0)

    for p in range(N_HEADS_A // 2):
        l2 = jnp.where(low_half, l_ref[2 * p], l_ref[2 * p + 1])
        o = acc_ref[p] / l2
        gate = ga_ref[0, :, p * LANES:(p + 1) * LANES].astype(F32)
        out_ref[0, :, p * LANES:(p + 1) * LANES] = (o * gate).astype(BF16)


def _dsa(qa, ka, va, qi, ki, wi, ga, bias_a, tq):
    b, s, _ = qa.shape
    topk = min(TOPK_MAX, s // 4)
    nq = s // tq
    qblk = lambda c: pl.BlockSpec((1, tq, c), lambda bi, i: (bi, i, 0))
    full = lambda c: pl.BlockSpec((1, s, c), lambda bi, i: (bi, 0, 0))
    grid_spec = pltpu.PrefetchScalarGridSpec(
        num_scalar_prefetch=0,
        grid=(b, nq),
        in_specs=[qblk(D_A), full(D_A), full(D_A), qblk(2 * LANES), full(LANES), qblk(LANES), qblk(D_A),
                  pl.BlockSpec((3, N_HEADS_A, tq, tq), lambda bi, i: (0, 0, 0, 0),
                               pipeline_mode=pl.Buffered(1))],
        out_specs=qblk(D_A),
        scratch_shapes=[pltpu.VMEM((nq, tq, tq), F32),
                        pltpu.VMEM((N_HEADS_A, tq, LANES), F32),
                        pltpu.VMEM((N_HEADS_A, tq, LANES), F32),
                        pltpu.VMEM((N_HEADS_A // 2, tq, LANES), F32),
                        pltpu.VMEM((tq, LANES), jnp.int32)],
    )
    return pl.pallas_call(
        functools.partial(_dsa_kernel, topk, tq, nq),
        grid_spec=grid_spec,
        out_shape=jax.ShapeDtypeStruct((b, s, D_A), BF16),
        compiler_params=pltpu.CompilerParams(dimension_semantics=("parallel", "arbitrary"),
                                             vmem_limit_bytes=VMEM_LIMIT),
        name="dsa",
    )(qa, ka, va, qi, ki, wi, ga, bias_a)


def _dil_kernel(qb_blocks, q_ref, kc_ref, kp_ref, vc_ref, vp_ref, bias_ref, o_ref, lse_ref):
    j = pl.program_id(2)
    lane = lax.broadcasted_iota(jnp.int32, (BLOCK, LANES), 1)
    low_half = lane < HEAD_DIM
    for blk in range(qb_blocks):
        rows = slice(blk * BLOCK, (blk + 1) * BLOCK)
        if blk == 0:
            variant = jnp.where(j == 0, 1, 0)
        else:
            variant = 0
        lse_tile = jnp.zeros((BLOCK, LANES), F32)
        for p in range(N_HEADS_B // 2):
            lanes = slice(p * LANES, (p + 1) * LANES)
            if blk == 0:
                kband = jnp.concatenate([kp_ref[0, :, lanes], kc_ref[0, 0:BLOCK, lanes]], axis=0)
                vband = jnp.concatenate([vp_ref[0, :, lanes], vc_ref[0, 0:BLOCK, lanes]], axis=0)
            else:
                band = slice((blk - 1) * BLOCK, (blk + 1) * BLOCK)
                kband = kc_ref[0, band, lanes]
                vband = vc_ref[0, band, lanes]
            qp = q_ref[0, rows, lanes]
            zero = jnp.zeros_like(qp)
            outs = []
            for e in range(2):
                h = 2 * p + e
                qz = jnp.where(low_half, qp, zero) if e == 0 else jnp.where(low_half, zero, qp)
                logits = _dot_nt(qz, kband) + bias_ref[variant, h]
                m = jnp.max(logits, axis=1, keepdims=True)
                pr = jnp.exp(logits - m)
                den = jnp.sum(pr, axis=1, keepdims=True)
                num = _dot(pr.astype(BF16), vband)
                outs.append(num / den)
                lse_tile = lse_tile + jnp.where(lane == h, m + jnp.log(den), 0.0)
            o_ref[0, rows, lanes] = jnp.where(low_half, outs[0], outs[1])
        lse_ref[0, rows, :] = lse_tile


def _dilated_pattern(qv, kv, vv, bias_p, dil, qb_blocks):
    b, n_sub, _ = qv.shape
    tu = qb_blocks * BLOCK
    nj = n_sub // tu
    cur = pl.BlockSpec((1, tu, D_B), lambda bi, r, j: (bi, j, r))
    prev = pl.BlockSpec((1, BLOCK, D_B), lambda bi, r, j: (bi, jnp.maximum(j * qb_blocks - 1, 0), r))
    return pl.pallas_call(
        functools.partial(_dil_kernel, qb_blocks),
        grid=(b, dil, nj),
        in_specs=[cur, cur, prev, cur, prev,
                  pl.BlockSpec((2, N_HEADS_B, BLOCK, 2 * BLOCK), lambda bi, r, j: (0, 0, 0, 0),
                               pipeline_mode=pl.Buffered(1))],
        out_specs=(pl.BlockSpec((1, tu, D_B), lambda bi, r, j: (bi, j, r)),
                   pl.BlockSpec((1, tu, LANES), lambda bi, r, j: (bi, j, r))),
        out_shape=(jax.ShapeDtypeStruct((b, n_sub, dil * D_B), F32),
                   jax.ShapeDtypeStruct((b, n_sub, dil * LANES), F32)),
        compiler_params=pltpu.CompilerParams(dimension_semantics=("parallel", "parallel", "arbitrary"),
                                             vmem_limit_bytes=VMEM_LIMIT),
        name=f"dilated_d{dil}",
    )(qv, kv, kv, vv, vv, bias_p)


def _out_kernel(npat, x_ref, a_ref, gb_ref, w_ref, e_ref, *refs):
    o_refs = refs[0:npat]
    lse_refs = refs[npat:2 * npat]
    out_ref = refs[2 * npat]
    lses = [r[...] for r in lse_refs]
    top = functools.reduce(jnp.maximum, lses)
    ws = [jnp.exp(l - top) for l in lses]
    tot = functools.reduce(lambda a, c: a + c, ws)
    e = e_ref[...]
    bmix = jnp.zeros(o_refs[0].shape, F32)
    for wgt, o_ref in zip(ws, o_refs):
        wn = wgt / tot
        hi = wn.astype(BF16)
        lo = (wn - hi.astype(F32)).astype(BF16)
        bmix = bmix + (_dot(hi, e) + _dot(lo, e)) * o_ref[...]
    bg = (bmix * gb_ref[...].astype(F32)).astype(BF16)
    out_ref[...] = x_ref[...] + _dot(a_ref[...], w_ref[0:D_A, :]) + _dot(bg, w_ref[D_A:D_A + D_B, :])


def _out_proj(x2, a_g, gb, w_out, expand, os_, lses, tm):
    n, d = x2.shape
    npat = len(os_)
    row = lambda i: (i, 0)
    const = lambda i: (0, 0)
    one_buf = dict(pipeline_mode=pl.Buffered(1))
    in_specs = ([pl.BlockSpec((tm, d), row), pl.BlockSpec((tm, D_A), row), pl.BlockSpec((tm, D_B), row),
                 pl.BlockSpec((D_A + D_B, d), const, **one_buf),
                 pl.BlockSpec((LANES, D_B), const, **one_buf)]
                + [pl.BlockSpec((tm, D_B), row)] * npat + [pl.BlockSpec((tm, LANES), row)] * npat)
    return pl.pallas_call(
        functools.partial(_out_kernel, npat),
        grid=(n // tm,),
        in_specs=in_specs,
        out_specs=pl.BlockSpec((tm, d), row),
        out_shape=jax.ShapeDtypeStruct((n, d), F32),
        compiler_params=pltpu.CompilerParams(dimension_semantics=("parallel",),
                                             vmem_limit_bytes=VMEM_LIMIT),
        name="merge_outproj",
    )(x2, a_g, gb, w_out, expand, *os_, *lses)


def _pack_w_in(w):
    a_end = 4 * D_A
    qi_end = a_end + N_IDX_HEADS * IDX_DIM
    ki_end = qi_end + IDX_DIM
    wi_end = ki_end + N_IDX_HEADS
    ki = w[:, qi_end:ki_end]
    wi = jnp.pad(w[:, ki_end:wi_end], ((0, 0), (0, LANES - N_IDX_HEADS)))
    return jnp.concatenate([w[:, :a_end], w[:, a_end:qi_end], ki, ki, wi, w[:, wi_end:]], axis=1).astype(BF16)


def _dsa_bias_tiles(bias_a, tq):
    a = jnp.arange(tq)[:, None]
    c = jnp.arange(tq)[None, :]
    far = jnp.broadcast_to(bias_a[N_BUCKETS - 1][:, None, None], (N_HEADS_A, tq, tq))
    prev = jnp.transpose(bias_a[_rel_bucket(a - c + tq)], (2, 0, 1))
    diag = jnp.transpose(bias_a[_rel_bucket(a - c)], (2, 0, 1))
    return jnp.stack([far, prev, diag]).astype(F32)


def _dil_bias_tiles(bias_b, window, dil):
    steps = window // dil
    a = jnp.arange(BLOCK)[:, None]
    c = jnp.arange(2 * BLOCK)[None, :]
    diff = a - c + BLOCK
    band = (diff >= 0) & (diff <= steps)
    bias = jnp.transpose(bias_b[_rel_bucket(diff * dil)], (2, 0, 1)).astype(F32)
    normal = jnp.where(band[None], bias, NEG_BIG)
    first = jnp.where((band & (c >= BLOCK))[None], bias, NEG_BIG)
    return jnp.stack([normal, first])


def kernel(x, norm_gain, w_in, w_out, rel_bias, q_norm_a, k_norm_a, q_norm_b, k_norm_b):
    b, s, d = x.shape
    n = b * s
    depth = norm_gain.shape[0]
    tq = min(256, s)
    tm = min(512, n)
    bias_a = rel_bias[:, :N_HEADS_A]
    bias_b = rel_bias[:, N_HEADS_A:]
    eye_blocks = (jnp.arange(D_A)[:, None] // HEAD_DIM == jnp.arange(D_A)[None, :] // HEAD_DIM).astype(BF16)
    expand = (jnp.arange(LANES)[:, None] == jnp.arange(D_B)[None, :] // HEAD_DIM).astype(BF16)
    dsa_bias = _dsa_bias_tiles(bias_a, tq)
    assert tq >= MAX_DISTANCE
    dil_bias = [_dil_bias_tiles(bias_b, wnd, dil) for wnd, dil in DIL_PATTERNS]

    x2 = x.reshape(n, d)
    for layer in range(depth):
        gn = jnp.stack([jnp.tile(g[layer], N_HEADS_A) for g in (q_norm_a, k_norm_a, q_norm_b, k_norm_b)])
        (qa, ka, va, ga, qi, ki, wi, qb, kb, vb, gb) = _inproj(
            x2, norm_gain[layer][None, :], _pack_w_in(w_in[layer]), eye_blocks, gn, tm)
        r3 = lambda t: t.reshape(b, s, t.shape[-1])
        a_g = _dsa(r3(qa), r3(ka), r3(va), r3(qi), r3(ki), r3(wi), r3(ga), dsa_bias, tq)

        os_, lses = [], []
        for (wnd, dil), bias_p in zip(DIL_PATTERNS, dil_bias):
            assert wnd // dil == BLOCK and s % (dil * BLOCK) == 0
            n_sub = s // dil
            view = lambda t: t.reshape(b, n_sub, dil * D_B)
            qb_blocks = min(4, n_sub // BLOCK)
            o, lse = _dilated_pattern(view(qb), view(kb), view(vb), bias_p, dil, qb_blocks)
            os_.append(o.reshape(n, D_B))
            lses.append(lse.reshape(n, LANES))
        x2 = _out_proj(x2, a_g.reshape(n, D_A), gb, w_out[layer].astype(BF16), expand, os_, lses, tm)
    return x2.reshape(b, s, d)
```

```python
import functools
import math

import jax
import jax.numpy as jnp
from jax import lax
from jax.experimental import pallas as pl
from jax.experimental.pallas import tpu as pltpu

HEAD_DIM = 64
N_HEADS_A = 8
N_HEADS_B = 8
D_A = N_HEADS_A * HEAD_DIM
D_B = N_HEADS_B * HEAD_DIM
N_IDX_HEADS = 4
IDX_DIM = 64
D_QI = N_IDX_HEADS * IDX_DIM
TOPK_MAX = 256
DIL_PATTERNS = ((128, 1), (512, 4), (2048, 16))
BLOCK = 128
N_BUCKETS = 32
MAX_DISTANCE = 128
EPS = 1e-6

LANES = 128
SUBLANES = 8
WI_ROWS = 16
NEG_BIG = -1e30
VMEM_LIMIT = 56 * 1024 * 1024

F32 = jnp.float32
BF16 = jnp.bfloat16
NT_DIMS = (((1,), (1,)), ((), ()))


def _dot(a, b):
    return jnp.dot(a, b, preferred_element_type=F32)


def _dot_nt(a, b):
    return lax.dot_general(a, b, NT_DIMS, preferred_element_type=F32)


def _rel_bucket(dist):
    max_exact = N_BUCKETS // 2
    d = jnp.maximum(dist, 0)
    df = jnp.maximum(d, 1).astype(F32)
    large = max_exact + (jnp.log(df / max_exact) / math.log(MAX_DISTANCE / max_exact)
                         * (N_BUCKETS - max_exact)).astype(jnp.int32)
    large = jnp.minimum(large, N_BUCKETS - 1)
    return jnp.where(d < max_exact, d, large)


def _inproj_kernel(tk, x_ref, g_ref, wn_ref, wt_ref, bd_ref, gn_ref, gcol_ref,
                   qat_ref, ka_ref, vat_ref, ga_ref, qit_ref, ki_ref, wit_ref,
                   qb_ref, kb_ref, vb_ref, gb_ref):
    x = x_ref[0]
    tm = x.shape[0]
    r = lax.rsqrt(jnp.mean(x * x, axis=-1, keepdims=True) + EPS)
    xn = (x * r * g_ref[...]).astype(BF16)
    bd = bd_ref[...]

    def head_norm(q, gain, scale):
        ss = _dot((q * q).astype(BF16), bd)
        return q * lax.rsqrt(ss * (1.0 / HEAD_DIM) + EPS) * (gain * scale)

    def silu(z):
        return z * jax.nn.sigmoid(z)

    p = _dot(xn, wn_ref[:, 0:2 * D_A + LANES])
    ka_ref[0] = head_norm(p[:, 0:D_A], gn_ref[1:2, :], 1.0).astype(BF16)
    ga_ref[0] = silu(p[:, D_A:2 * D_A]).astype(BF16)
    ki_ref[0] = p[:, 2 * D_A:2 * D_A + LANES].astype(BF16)
    c0 = 2 * D_A + LANES
    p = _dot(xn, wn_ref[:, c0:c0 + 4 * D_B])
    qb_ref[0] = head_norm(p[:, 0:D_B], gn_ref[2:3, :], HEAD_DIM ** -0.5).astype(BF16)
    kb_ref[0] = head_norm(p[:, D_B:2 * D_B], gn_ref[3:4, :], 1.0).astype(BF16)
    vb_ref[0] = p[:, 2 * D_B:3 * D_B].astype(BF16)
    gb_ref[0] = silu(p[:, 3 * D_B:4 * D_B]).astype(BF16)

    qt = _dot_nt(wt_ref[0:D_A, :], xn)
    ss = _dot(bd, (qt * qt).astype(BF16))
    qt = qt * lax.rsqrt(ss * (1.0 / HEAD_DIM) + EPS) * (gcol_ref[...] * HEAD_DIM ** -0.5)
    qat_ref[0] = qt.astype(BF16)
    vt = _dot_nt(wt_ref[D_A:2 * D_A, :], xn)
    for cc in range(tm // tk):
        vat_ref[0, cc] = vt[:, cc * tk:(cc + 1) * tk].astype(BF16)
    qit_ref[0] = _dot_nt(wt_ref[2 * D_A:2 * D_A + D_QI, :], xn).astype(BF16)
    wit_ref[0] = _dot_nt(wt_ref[2 * D_A + D_QI:2 * D_A + D_QI + WI_ROWS, :], xn)


def _inproj(x, g, wn, wt, bd, gn, gcol, tm, tk):
    b, s, d = x.shape
    const = lambda bi, j: (0, 0)
    tok = lambda c: pl.BlockSpec((1, tm, c), lambda bi, j: (bi, j, 0))
    feat = lambda c: pl.BlockSpec((1, c, tm), lambda bi, j: (bi, 0, j))
    one_buf = dict(pipeline_mode=pl.Buffered(1))
    sds = jax.ShapeDtypeStruct
    out_shapes = (sds((b, D_A, s), BF16), sds((b, s, D_A), BF16), sds((b, s // tk, D_A, tk), BF16),
                  sds((b, s, D_A), BF16), sds((b, D_QI, s), BF16), sds((b, s, LANES), BF16),
                  sds((b, WI_ROWS, s), F32),
                  sds((b, s, D_B), BF16), sds((b, s, D_B), BF16), sds((b, s, D_B), BF16), sds((b, s, D_B), BF16))
    out_specs = (feat(D_A), tok(D_A),
                 pl.BlockSpec((1, tm // tk, D_A, tk), lambda bi, j: (bi, j, 0, 0)),
                 tok(D_A), feat(D_QI), tok(LANES), feat(WI_ROWS),
                 tok(D_B), tok(D_B), tok(D_B), tok(D_B))
    return pl.pallas_call(
        functools.partial(_inproj_kernel, tk),
        grid=(b, s // tm),
        in_specs=[tok(d),
                  pl.BlockSpec((1, d), const, **one_buf),
                  pl.BlockSpec(wn.shape, const, **one_buf),
                  pl.BlockSpec(wt.shape, const, **one_buf),
                  pl.BlockSpec((D_A, D_A), const, **one_buf),
                  pl.BlockSpec((4, D_A), const, **one_buf),
                  pl.BlockSpec((D_A, 1), const, **one_buf)],
        out_specs=out_specs,
        out_shape=out_shapes,
        compiler_params=pltpu.CompilerParams(dimension_semantics=("parallel", "parallel"),
                                             vmem_limit_bytes=VMEM_LIMIT),
        name="inproj",
    )(x, g, wn, wt, bd, gn, gcol)


def _ordered_bits_to_float(u):
    key = u ^ jnp.int32(-2 ** 31)
    bits = key ^ ((key >> 31) & jnp.int32(0x7FFFFFFF))
    return lax.bitcast_convert_type(bits, F32)


def _fold_rows(x):
    parts = [None] * 4
    for r in range(x.shape[0] // SUBLANES):
        blk = x[r * SUBLANES:(r + 1) * SUBLANES, :]
        parts[r % 4] = blk if parts[r % 4] is None else parts[r % 4] + blk
    return (parts[0] + parts[1]) + (parts[2] + parts[3])


def _dsa_kernel(topk, tq, nq, qat_ref, ka_ref, vat_ref, qit_ref, ki_ref, wit_ref, ga_ref, bias_ref,
                out_ref, sc_ref, madd_ref, qz_ref, m_ref, l_ref, acc_ref, jt_ref):
    tk = tq
    i = pl.program_id(1)
    nchunks = i + 1
    idx_scale = (N_IDX_HEADS * IDX_DIM) ** -0.5
    top_half = lax.broadcasted_iota(jnp.int32, (LANES, tq), 0) < HEAD_DIM

    def pad_heads(ref, npairs):
        out = []
        for p in range(npairs):
            blk = ref[0, p * LANES:(p + 1) * LANES, :]
            zero = jnp.zeros_like(blk)
            out.append(jnp.where(top_half, blk, zero))
            out.append(jnp.where(top_half, zero, blk))
        return out

    qiz = pad_heads(qit_ref, N_IDX_HEADS // 2)
    w = wit_ref[0]
    krow = lax.broadcasted_iota(jnp.int32, (tk, tq), 0)
    qcol_g = i * tq + lax.broadcasted_iota(jnp.int32, (tk, tq), 1)

    def score_chunk(c, carry):
        kc = ki_ref[0, pl.ds(pl.multiple_of(c * tk, tk), tk), :]
        acc = jnp.zeros((tk, tq), F32)
        for h in range(N_IDX_HEADS):
            acc = acc + jnp.maximum(_dot(kc, qiz[h]), 0.0) * w[h:h + 1, :]
        s = acc * idx_scale + 0.0
        sc_ref[c] = jnp.where(krow + c * tk <= qcol_g, s, -jnp.inf)
        return carry

    lax.fori_loop(0, nchunks, score_chunk, 0)

    def count_pass(pred):
        def body(c, acc):
            return acc + _fold_rows(jnp.where(pred(sc_ref[c], c), 1.0, 0.0))
        acc = lax.fori_loop(0, nchunks, body, jnp.zeros((SUBLANES, tq), F32))
        return jnp.sum(acc, axis=0, keepdims=True)

    def bit_step(it, cur):
        cand = cur | lax.shift_left(jnp.int32(1), 31 - it)
        candf = _ordered_bits_to_float(cand)
        cnt = count_pass(lambda s, c: s >= candf)
        return jnp.where(cnt >= float(topk), cand, cur)

    cur = lax.fori_loop(0, 32, bit_step, jnp.zeros((1, tq), jnp.int32))
    short = (cur & jnp.int32(-2 ** 23)) == 0
    thr = jnp.where(short, -jnp.inf, _ordered_bits_to_float(cur))
    cnt_gt = count_pass(lambda s, c: s > thr)
    cnt_ge = count_pass(lambda s, c: s >= thr)
    need_m1 = float(topk) - cnt_gt - 1.0
    tie_excess = jnp.where(short, 0.0, cnt_ge - float(topk))

    jt_ref[...] = jnp.where(short, -1, jnp.int32(2 ** 30))

    @pl.when(jnp.max(tie_excess) > 0.0)
    def _():
        nbits = max(1, (nq * tk - 1).bit_length())

        def tie_step(it, curj):
            cand = curj | lax.shift_left(jnp.int32(1), nbits - 1 - it)
            cnt = count_pass(lambda s, c: (s == thr) & (krow < cand - c * tk))
            return jnp.where(cnt <= need_m1, cand, curj)

        curj = lax.fori_loop(0, nbits, tie_step, jnp.zeros((1, tq), jnp.int32))
        jt_ref[...] = jnp.where(short, -1, curj)

    jt = jt_ref[...]

    for h, qz in enumerate(pad_heads(qat_ref, N_HEADS_A // 2)):
        qz_ref[h] = qz
    m_ref[...] = jnp.full(m_ref.shape, NEG_BIG, F32)
    l_ref[...] = jnp.zeros(l_ref.shape, F32)
    acc_ref[...] = jnp.zeros(acc_ref.shape, F32)
    ones_row = jnp.ones((SUBLANES, tk), BF16)

    def attn_chunk(c, variant):
        k0 = pl.multiple_of(c * tk, tk)
        s = sc_ref[c]
        sel = (s > thr) | ((s == thr) & (krow <= jt - c * tk))
        madd_ref[...] = jnp.where(sel, 0.0, NEG_BIG)
        for h in range(N_HEADS_A):
            p = h // 2
            kp = ka_ref[0, pl.ds(k0, tk), p * LANES:(p + 1) * LANES]
            logits = _dot(kp, qz_ref[h]) + madd_ref[...]
            if variant is not None:
                logits = logits + bias_ref[variant, h]
            hrow = slice(h, h + 1)
            m_old = m_ref[hrow, :]
            m_new = jnp.maximum(m_old, jnp.max(logits, axis=0, keepdims=True))
            alpha = jnp.exp(m_old - m_new)
            pr = jnp.exp(logits - m_new).astype(BF16)
            l_ref[hrow, :] = alpha * l_ref[hrow, :] + _dot(ones_row, pr)[0:1, :]
            m_ref[hrow, :] = m_new
            rows = slice(h * HEAD_DIM, (h + 1) * HEAD_DIM)
            acc_ref[rows, :] = alpha * acc_ref[rows, :] + _dot(vat_ref[0, c, rows, :], pr)

    n_far = jnp.maximum(i - 1, 0)

    def far_body(c, carry):
        attn_chunk(c, None)
        return carry

    def near_body(c, carry):
        attn_chunk(c, c - i + 1)
        return carry

    lax.fori_loop(0, n_far, far_body, 0)
    lax.fori_loop(n_far, nchunks, near_body, 0)

    outs = []
    for h in range(N_HEADS_A):
        rows = slice(h * HEAD_DIM, (h + 1) * HEAD_DIM)
        outs.append(acc_ref[rows, :] / l_ref[h:h + 1, :])
    o = jnp.concatenate(outs, axis=0).T
    out_ref[0] = (o * ga_ref[0].astype(F32)).astype(BF16)


def _dsa(qat, ka, vat, qit, ki, wit, ga, bias_a, tq):
    b, s, _ = ka.shape
    topk = min(TOPK_MAX, s // 4)
    nq = s // tq
    qfeat = lambda c: pl.BlockSpec((1, c, tq), lambda bi, i: (bi, 0, i))
    qtok = lambda c: pl.BlockSpec((1, tq, c), lambda bi, i: (bi, i, 0))
    full = lambda c: pl.BlockSpec((1, s, c), lambda bi, i: (bi, 0, 0))
    return pl.pallas_call(
        functools.partial(_dsa_kernel, topk, tq, nq),
        grid=(b, nq),
        in_specs=[qfeat(D_A), full(D_A),
                  pl.BlockSpec((1, nq, D_A, tq), lambda bi, i: (bi, 0, 0, 0)),
                  qfeat(D_QI), full(LANES), qfeat(WI_ROWS), qtok(D_A),
                  pl.BlockSpec((2, N_HEADS_A, tq, tq), lambda bi, i: (0, 0, 0, 0),
                               pipeline_mode=pl.Buffered(1))],
        out_specs=qtok(D_A),
        out_shape=jax.ShapeDtypeStruct((b, s, D_A), BF16),
        scratch_shapes=[pltpu.VMEM((nq, tq, tq), F32),
                        pltpu.VMEM((tq, tq), F32),
                        pltpu.VMEM((N_HEADS_A, LANES, tq), BF16),
                        pltpu.VMEM((N_HEADS_A, tq), F32),
                        pltpu.VMEM((N_HEADS_A, tq), F32),
                        pltpu.VMEM((D_A, tq), F32),
                        pltpu.VMEM((1, tq), jnp.int32)],
        compiler_params=pltpu.CompilerParams(dimension_semantics=("parallel", "arbitrary"),
                                             vmem_limit_bytes=VMEM_LIMIT),
        name="dsa",
    )(qat, ka, vat, qit, ki, wit, ga, bias_a)


def _dil_kernel(qb_blocks, q_ref, kc_ref, kp_ref, vc_ref, vp_ref, bias_ref, o_ref, lse_ref):
    j = pl.program_id(2)
    lane = lax.broadcasted_iota(jnp.int32, (BLOCK, LANES), 1)
    low_half = lane < HEAD_DIM
    for blk in range(qb_blocks):
        rows = slice(blk * BLOCK, (blk + 1) * BLOCK)
        if blk == 0:
            variant = jnp.where(j == 0, 1, 0)
        else:
            variant = 0
        lse_tile = jnp.zeros((BLOCK, LANES), F32)
        for p in range(N_HEADS_B // 2):
            lanes = slice(p * LANES, (p + 1) * LANES)
            if blk == 0:
                kband = jnp.concatenate([kp_ref[0, :, lanes], kc_ref[0, 0:BLOCK, lanes]], axis=0)
                vband = jnp.concatenate([vp_ref[0, :, lanes], vc_ref[0, 0:BLOCK, lanes]], axis=0)
            else:
                band = slice((blk - 1) * BLOCK, (blk + 1) * BLOCK)
                kband = kc_ref[0, band, lanes]
                vband = vc_ref[0, band, lanes]
            qp = q_ref[0, rows, lanes]
            zero = jnp.zeros_like(qp)
            outs = []
            for e in range(2):
                h = 2 * p + e
                qz = jnp.where(low_half, qp, zero) if e == 0 else jnp.where(low_half, zero, qp)
                logits = _dot_nt(qz, kband) + bias_ref[variant, h]
                m = jnp.max(logits, axis=1, keepdims=True)
                pr = jnp.exp(logits - m)
                den = jnp.sum(pr, axis=1, keepdims=True)
                num = _dot(pr.astype(BF16), vband)
                outs.append(num / den)
                lse_tile = lse_tile + jnp.where(lane == h, m + jnp.log(den), 0.0)
            o_ref[0, rows, lanes] = jnp.where(low_half, outs[0], outs[1])
        lse_ref[0, rows, :] = lse_tile


def _dilated_pattern(qv, kv, vv, bias_p, dil, qb_blocks):
    b, n_sub, _ = qv.shape
    tu = qb_blocks * BLOCK
    nj = n_sub // tu
    cur = pl.BlockSpec((1, tu, D_B), lambda bi, r, j: (bi, j, r))
    prev = pl.BlockSpec((1, BLOCK, D_B), lambda bi, r, j: (bi, jnp.maximum(j * qb_blocks - 1, 0), r))
    return pl.pallas_call(
        functools.partial(_dil_kernel, qb_blocks),
        grid=(b, dil, nj),
        in_specs=[cur, cur, prev, cur, prev,
                  pl.BlockSpec((2, N_HEADS_B, BLOCK, 2 * BLOCK), lambda bi, r, j: (0, 0, 0, 0),
                               pipeline_mode=pl.Buffered(1))],
        out_specs=(pl.BlockSpec((1, tu, D_B), lambda bi, r, j: (bi, j, r)),
                   pl.BlockSpec((1, tu, LANES), lambda bi, r, j: (bi, j, r))),
        out_shape=(jax.ShapeDtypeStruct((b, n_sub, dil * D_B), F32),
                   jax.ShapeDtypeStruct((b, n_sub, dil * LANES), F32)),
        compiler_params=pltpu.CompilerParams(dimension_semantics=("parallel", "parallel", "arbitrary"),
                                             vmem_limit_bytes=VMEM_LIMIT),
        name=f"dilated_d{dil}",
    )(qv, kv, kv, vv, vv, bias_p)


def _out_kernel(npat, x_ref, a_ref, gb_ref, w_ref, e_ref, *refs):
    o_refs = refs[0:npat]
    lse_refs = refs[npat:2 * npat]
    out_ref = refs[2 * npat]
    lses = [r[...] for r in lse_refs]
    top = functools.reduce(jnp.maximum, lses)
    ws = [jnp.exp(l - top) for l in lses]
    tot = functools.reduce(lambda a, c: a + c, ws)
    e = e_ref[...]
    bmix = jnp.zeros(o_refs[0].shape, F32)
    for wgt, o_ref in zip(ws, o_refs):
        wn = wgt / tot
        hi = wn.astype(BF16)
        lo = (wn - hi.astype(F32)).astype(BF16)
        bmix = bmix + (_dot(hi, e) + _dot(lo, e)) * o_ref[...]
    bg = (bmix * gb_ref[...].astype(F32)).astype(BF16)
    out_ref[...] = x_ref[...] + _dot(a_ref[...], w_ref[0:D_A, :]) + _dot(bg, w_ref[D_A:D_A + D_B, :])


def _out_proj(x2, a_g, gb, w_out, expand, os_, lses, tm):
    n, d = x2.shape
    npat = len(os_)
    row = lambda i: (i, 0)
    const = lambda i: (0, 0)
    one_buf = dict(pipeline_mode=pl.Buffered(1))
    in_specs = ([pl.BlockSpec((tm, d), row), pl.BlockSpec((tm, D_A), row), pl.BlockSpec((tm, D_B), row),
                 pl.BlockSpec((D_A + D_B, d), const, **one_buf),
                 pl.BlockSpec((LANES, D_B), const, **one_buf)]
                + [pl.BlockSpec((tm, D_B), row)] * npat + [pl.BlockSpec((tm, LANES), row)] * npat)
    return pl.pallas_call(
        functools.partial(_out_kernel, npat),
        grid=(n // tm,),
        in_specs=in_specs,
        out_specs=pl.BlockSpec((tm, d), row),
        out_shape=jax.ShapeDtypeStruct((n, d), F32),
        compiler_params=pltpu.CompilerParams(dimension_semantics=("parallel",),
                                             vmem_limit_bytes=VMEM_LIMIT),
        name="merge_outproj",
    )(x2, a_g, gb, w_out, expand, *os_, *lses)


def _pack_w_in(w):
    qa, ka, va, za = (w[:, j * D_A:(j + 1) * D_A] for j in range(4))
    o = 4 * D_A
    qi = w[:, o:o + D_QI]
    ki = w[:, o + D_QI:o + D_QI + IDX_DIM]
    wi = w[:, o + D_QI + IDX_DIM:o + D_QI + IDX_DIM + N_IDX_HEADS]
    wb = w[:, o + D_QI + IDX_DIM + N_IDX_HEADS:]
    wn = jnp.concatenate([ka, za, ki, ki, wb], axis=1).astype(BF16)
    wi = jnp.pad(wi, ((0, 0), (0, WI_ROWS - N_IDX_HEADS)))
    wt = jnp.concatenate([qa, va, qi, wi], axis=1).T.astype(BF16)
    return wn, wt


def _bias_lookup(table, bucket):
    onehot = jax.nn.one_hot(bucket, N_BUCKETS, dtype=F32)
    return jnp.einsum('...n,nh->...h', onehot, table.astype(F32), precision=lax.Precision.HIGHEST)


def _dsa_bias_tiles(bias_a, tq):
    k = jnp.arange(tq)[:, None]
    q = jnp.arange(tq)[None, :]
    far = bias_a[N_BUCKETS - 1].astype(F32)
    prev = _bias_lookup(bias_a, _rel_bucket(q - k + tq)) - far
    diag = _bias_lookup(bias_a, _rel_bucket(q - k)) - far
    return jnp.transpose(jnp.stack([prev, diag]), (0, 3, 1, 2))


def _dil_bias_tiles(bias_b, window, dil):
    steps = window // dil
    a = jnp.arange(BLOCK)[:, None]
    c = jnp.arange(2 * BLOCK)[None, :]
    diff = a - c + BLOCK
    band = (diff >= 0) & (diff <= steps)
    bias = jnp.transpose(_bias_lookup(bias_b, _rel_bucket(diff * dil)), (2, 0, 1))
    normal = jnp.where(band[None], bias, NEG_BIG)
    first = jnp.where((band & (c >= BLOCK))[None], bias, NEG_BIG)
    return jnp.stack([normal, first])


def kernel(x, norm_gain, w_in, w_out, rel_bias, q_norm_a, k_norm_a, q_norm_b, k_norm_b):
    b, s, d = x.shape
    n = b * s
    depth = norm_gain.shape[0]
    tq = min(256, s)
    tm = min(512, s)
    bias_a = rel_bias[:, :N_HEADS_A]
    bias_b = rel_bias[:, N_HEADS_A:]
    eye_blocks = (jnp.arange(D_A)[:, None] // HEAD_DIM == jnp.arange(D_A)[None, :] // HEAD_DIM).astype(BF16)
    expand = (jnp.arange(LANES)[:, None] == jnp.arange(D_B)[None, :] // HEAD_DIM).astype(BF16)
    assert tq >= MAX_DISTANCE
    dsa_bias = _dsa_bias_tiles(bias_a, tq)
    dil_bias = [_dil_bias_tiles(bias_b, wnd, dil) for wnd, dil in DIL_PATTERNS]

    for layer in range(depth):
        gn = jnp.stack([jnp.tile(g[layer], N_HEADS_A) for g in (q_norm_a, k_norm_a, q_norm_b, k_norm_b)])
        wn, wt = _pack_w_in(w_in[layer])
        (qat, ka, vat, ga, qit, ki, wit, qb, kb, vb, gb) = _inproj(
            x, norm_gain[layer][None, :], wn, wt, eye_blocks, gn, gn[0][:, None], tm, tq)
        a_g = _dsa(qat, ka, vat, qit, ki, wit, ga, dsa_bias, tq)

        os_, lses = [], []
        for (wnd, dil), bias_p in zip(DIL_PATTERNS, dil_bias):
            assert wnd // dil == BLOCK and s % (dil * BLOCK) == 0
            n_sub = s // dil
            view = lambda t: t.reshape(b, n_sub, dil * D_B)
            qb_blocks = min(4, n_sub // BLOCK)
            o, lse = _dilated_pattern(view(qb), view(kb), view(vb), bias_p, dil, qb_blocks)
            os_.append(o.reshape(n, D_B))
            lses.append(lse.reshape(n, LANES))
        x = _out_proj(x.reshape(n, d), a_g.reshape(n, D_A), gb.reshape(n, D_B), w_out[layer].astype(BF16),
                      expand, os_, lses, tm).reshape(b, s, d)
    return x
```

```python
import functools
import math

import jax
import jax.numpy as jnp
from jax import lax
from jax.experimental import pallas as pl
from jax.experimental.pallas import tpu as pltpu

HEAD_DIM = 64
N_HEADS_A = 8
N_HEADS_B = 8
D_A = N_HEADS_A * HEAD_DIM
D_B = N_HEADS_B * HEAD_DIM
N_IDX_HEADS = 4
IDX_DIM = 64
D_QI = N_IDX_HEADS * IDX_DIM
TOPK_MAX = 256
DIL_PATTERNS = ((128, 1), (512, 4), (2048, 16))
BLOCK = 128
N_BUCKETS = 32
MAX_DISTANCE = 128
EPS = 1e-6

LANES = 128
SUBLANES = 8
WI_ROWS = 16
NEG_BIG = -(2.0 ** 100)
LOG2E = math.log2(math.e)
VMEM_LIMIT = 56 * 1024 * 1024

F32 = jnp.float32
BF16 = jnp.bfloat16
NT_DIMS = (((1,), (1,)), ((), ()))


def _dot(a, b):
    return jnp.dot(a, b, preferred_element_type=F32)


def _dot_nt(a, b):
    return lax.dot_general(a, b, NT_DIMS, preferred_element_type=F32)


def _rel_bucket(dist):
    max_exact = N_BUCKETS // 2
    d = jnp.maximum(dist, 0)
    df = jnp.maximum(d, 1).astype(F32)
    large = max_exact + (jnp.log(df / max_exact) / math.log(MAX_DISTANCE / max_exact)
                         * (N_BUCKETS - max_exact)).astype(jnp.int32)
    large = jnp.minimum(large, N_BUCKETS - 1)
    return jnp.where(d < max_exact, d, large)


def _inproj_kernel(tk, x_ref, g_ref, wn_ref, wt_ref, bd_ref, gn_ref, gcol_ref,
                   qat_ref, ka_ref, vat_ref, ga_ref, qit_ref, ki_ref, wit_ref,
                   qb_ref, kb_ref, vb_ref, gb_ref):
    x = x_ref[0]
    tm = x.shape[0]
    r = lax.rsqrt(jnp.mean(x * x, axis=-1, keepdims=True) + EPS)
    xn = (x * r * g_ref[...]).astype(BF16)
    bd = bd_ref[...]

    def head_norm(q, gain, scale):
        ss = _dot((q * q).astype(BF16), bd)
        return q * lax.rsqrt(ss * (1.0 / HEAD_DIM) + EPS) * (gain * scale)

    def silu(z):
        return z * jax.nn.sigmoid(z)

    p = _dot(xn, wn_ref[:, 0:2 * D_A + LANES])
    ka_ref[0] = head_norm(p[:, 0:D_A], gn_ref[1:2, :], 1.0).astype(BF16)
    ga_ref[0] = silu(p[:, D_A:2 * D_A]).astype(BF16)
    ki_ref[0] = p[:, 2 * D_A:2 * D_A + LANES].astype(BF16)
    c0 = 2 * D_A + LANES
    p = _dot(xn, wn_ref[:, c0:c0 + 4 * D_B])
    qb_ref[0] = head_norm(p[:, 0:D_B], gn_ref[2:3, :], HEAD_DIM ** -0.5).astype(BF16)
    kb_ref[0] = head_norm(p[:, D_B:2 * D_B], gn_ref[3:4, :], 1.0).astype(BF16)
    vb_ref[0] = p[:, 2 * D_B:3 * D_B].astype(BF16)
    gb_ref[0] = silu(p[:, 3 * D_B:4 * D_B]).astype(BF16)

    qt = _dot_nt(wt_ref[0:D_A, :], xn)
    ss = _dot(bd, (qt * qt).astype(BF16))
    qt = qt * lax.rsqrt(ss * (1.0 / HEAD_DIM) + EPS) * (gcol_ref[...] * (HEAD_DIM ** -0.5 * LOG2E))
    qat_ref[0] = qt.astype(BF16)
    vt = _dot_nt(wt_ref[D_A:2 * D_A, :], xn)
    for cc in range(tm // tk):
        vat_ref[0, cc] = vt[:, cc * tk:(cc + 1) * tk].astype(BF16)
    qit_ref[0] = _dot_nt(wt_ref[2 * D_A:2 * D_A + D_QI, :], xn).astype(BF16)
    wit_ref[0] = _dot_nt(wt_ref[2 * D_A + D_QI:2 * D_A + D_QI + WI_ROWS, :], xn)


def _inproj(x, g, wn, wt, bd, gn, gcol, tm, tk):
    b, s, d = x.shape
    const = lambda bi, j: (0, 0)
    tok = lambda c: pl.BlockSpec((1, tm, c), lambda bi, j: (bi, j, 0))
    feat = lambda c: pl.BlockSpec((1, c, tm), lambda bi, j: (bi, 0, j))
    one_buf = dict(pipeline_mode=pl.Buffered(1))
    sds = jax.ShapeDtypeStruct
    out_shapes = (sds((b, D_A, s), BF16), sds((b, s, D_A), BF16), sds((b, s // tk, D_A, tk), BF16),
                  sds((b, s, D_A), BF16), sds((b, D_QI, s), BF16), sds((b, s, LANES), BF16),
                  sds((b, WI_ROWS, s), F32),
                  sds((b, s, D_B), BF16), sds((b, s, D_B), BF16), sds((b, s, D_B), BF16), sds((b, s, D_B), BF16))
    out_specs = (feat(D_A), tok(D_A),
                 pl.BlockSpec((1, tm // tk, D_A, tk), lambda bi, j: (bi, j, 0, 0)),
                 tok(D_A), feat(D_QI), tok(LANES), feat(WI_ROWS),
                 tok(D_B), tok(D_B), tok(D_B), tok(D_B))
    return pl.pallas_call(
        functools.partial(_inproj_kernel, tk),
        grid=(b, s // tm),
        in_specs=[tok(d),
                  pl.BlockSpec((1, d), const, **one_buf),
                  pl.BlockSpec(wn.shape, const, **one_buf),
                  pl.BlockSpec(wt.shape, const, **one_buf),
                  pl.BlockSpec((D_A, D_A), const, **one_buf),
                  pl.BlockSpec((4, D_A), const, **one_buf),
                  pl.BlockSpec((D_A, 1), const, **one_buf)],
        out_specs=out_specs,
        out_shape=out_shapes,
        compiler_params=pltpu.CompilerParams(dimension_semantics=("parallel", "parallel"),
                                             vmem_limit_bytes=VMEM_LIMIT),
        name="inproj",
    )(x, g, wn, wt, bd, gn, gcol)


def _ordered_bits_to_float(u):
    key = u ^ jnp.int32(-2 ** 31)
    bits = key ^ ((key >> 31) & jnp.int32(0x7FFFFFFF))
    return lax.bitcast_convert_type(bits, F32)


def _fold_rows(x):
    parts = [None] * 4
    for r in range(x.shape[0] // SUBLANES):
        blk = x[r * SUBLANES:(r + 1) * SUBLANES, :]
        parts[r % 4] = blk if parts[r % 4] is None else parts[r % 4] + blk
    return (parts[0] + parts[1]) + (parts[2] + parts[3])


def _dsa_kernel(topk, tq, nq, qat_ref, ka_ref, vat_ref, qit_ref, ki_ref, wit_ref, ga_ref, bias_ref,
                out_ref, sc_ref, madd_ref, qz_ref, m_ref, l_ref, acc_ref, jt_ref):
    tk = tq
    i = pl.program_id(1)
    nchunks = i + 1
    idx_scale = (N_IDX_HEADS * IDX_DIM) ** -0.5
    top_half = lax.broadcasted_iota(jnp.int32, (LANES, tq), 0) < HEAD_DIM

    def pad_heads(ref, npairs):
        out = []
        for p in range(npairs):
            blk = ref[0, p * LANES:(p + 1) * LANES, :]
            zero = jnp.zeros_like(blk)
            out.append(jnp.where(top_half, blk, zero))
            out.append(jnp.where(top_half, zero, blk))
        return out

    qiz = pad_heads(qit_ref, N_IDX_HEADS // 2)
    w = wit_ref[0]
    krow = lax.broadcasted_iota(jnp.int32, (tk, tq), 0)
    qcol_g = i * tq + lax.broadcasted_iota(jnp.int32, (tk, tq), 1)

    def score_chunk(c, carry):
        kc = ki_ref[0, pl.ds(pl.multiple_of(c * tk, tk), tk), :]
        acc = jnp.zeros((tk, tq), F32)
        for h in range(N_IDX_HEADS):
            acc = acc + jnp.maximum(_dot(kc, qiz[h]), 0.0) * w[h:h + 1, :]
        s = acc * idx_scale + 0.0
        sc_ref[c] = jnp.where(krow + c * tk <= qcol_g, s, -jnp.inf)
        return carry

    lax.fori_loop(0, nchunks, score_chunk, 0)

    def count_pass(pred):
        def body(c, acc):
            return acc + _fold_rows(jnp.where(pred(sc_ref[c], c), 1.0, 0.0))
        acc = lax.fori_loop(0, nchunks, body, jnp.zeros((SUBLANES, tq), F32))
        return jnp.sum(acc, axis=0, keepdims=True)

    def bit_step(it, cur):
        cand = cur | lax.shift_left(jnp.int32(1), 31 - it)
        candf = _ordered_bits_to_float(cand)
        cnt = count_pass(lambda s, c: s >= candf)
        return jnp.where(cnt >= float(topk), cand, cur)

    cur = lax.fori_loop(0, 32, bit_step, jnp.zeros((1, tq), jnp.int32))
    short = (cur & jnp.int32(-2 ** 23)) == 0
    thr = jnp.where(short, -jnp.inf, _ordered_bits_to_float(cur))
    cnt_gt = count_pass(lambda s, c: s > thr)
    cnt_ge = count_pass(lambda s, c: s >= thr)
    need_m1 = float(topk) - cnt_gt - 1.0
    tie_excess = jnp.where(short, 0.0, cnt_ge - float(topk))

    jt_ref[...] = jnp.where(short, -1, jnp.int32(2 ** 30))

    @pl.when(jnp.max(tie_excess) > 0.0)
    def _():
        nbits = max(1, (nq * tk - 1).bit_length())

        def tie_step(it, curj):
            cand = curj | lax.shift_left(jnp.int32(1), nbits - 1 - it)
            cnt = count_pass(lambda s, c: (s == thr) & (krow < cand - c * tk))
            return jnp.where(cnt <= need_m1, cand, curj)

        curj = lax.fori_loop(0, nbits, tie_step, jnp.zeros((1, tq), jnp.int32))
        jt_ref[...] = jnp.where(short, -1, curj)

    jt = jt_ref[...]

    for h, qz in enumerate(pad_heads(qat_ref, N_HEADS_A // 2)):
        qz_ref[h] = qz
    m_ref[...] = jnp.full(m_ref.shape, NEG_BIG, F32)
    l_ref[...] = jnp.zeros(l_ref.shape, F32)
    acc_ref[...] = jnp.zeros(acc_ref.shape, F32)
    ones_row = jnp.ones((SUBLANES, tk), BF16)

    def attn_chunk(c, variant):
        k0 = pl.multiple_of(c * tk, tk)
        s = sc_ref[c]
        sel = (s > thr) | ((s == thr) & (krow <= jt - c * tk))
        madd_ref[...] = jnp.where(sel, 0.0, NEG_BIG).astype(BF16)
        for h in range(N_HEADS_A):
            p = h // 2
            kp = ka_ref[0, pl.ds(k0, tk), p * LANES:(p + 1) * LANES]
            logits = _dot(kp, qz_ref[h]).astype(BF16) + madd_ref[...]
            if variant is not None:
                logits = logits + bias_ref[variant, h]
            hrow = slice(h, h + 1)
            m_old = m_ref[hrow, :]
            m_new = jnp.maximum(m_old, jnp.max(logits, axis=0, keepdims=True).astype(F32))
            alpha = jnp.exp2(m_old - m_new)
            pr = jnp.exp2(logits - m_new.astype(BF16))
            l_ref[hrow, :] = alpha * l_ref[hrow, :] + _dot(ones_row, pr)[0:1, :]
            m_ref[hrow, :] = m_new
            rows = slice(h * HEAD_DIM, (h + 1) * HEAD_DIM)
            acc_ref[rows, :] = alpha * acc_ref[rows, :] + _dot(vat_ref[0, c, rows, :], pr)

    n_far = jnp.maximum(i - 1, 0)

    def far_body(c, carry):
        attn_chunk(c, None)
        return carry

    def near_body(c, carry):
        attn_chunk(c, c - i + 1)
        return carry

    lax.fori_loop(0, n_far, far_body, 0)
    lax.fori_loop(n_far, nchunks, near_body, 0)

    outs = []
    for h in range(N_HEADS_A):
        rows = slice(h * HEAD_DIM, (h + 1) * HEAD_DIM)
        outs.append(acc_ref[rows, :] / l_ref[h:h + 1, :])
    o = jnp.concatenate(outs, axis=0).T
    out_ref[0] = (o * ga_ref[0].astype(F32)).astype(BF16)


def _dsa(qat, ka, vat, qit, ki, wit, ga, bias_a, tq):
    b, s, _ = ka.shape
    topk = min(TOPK_MAX, s // 4)
    nq = s // tq
    qfeat = lambda c: pl.BlockSpec((1, c, tq), lambda bi, i: (bi, 0, i))
    qtok = lambda c: pl.BlockSpec((1, tq, c), lambda bi, i: (bi, i, 0))
    full = lambda c: pl.BlockSpec((1, s, c), lambda bi, i: (bi, 0, 0))
    return pl.pallas_call(
        functools.partial(_dsa_kernel, topk, tq, nq),
        grid=(b, nq),
        in_specs=[qfeat(D_A), full(D_A),
                  pl.BlockSpec((1, nq, D_A, tq), lambda bi, i: (bi, 0, 0, 0)),
                  qfeat(D_QI), full(LANES), qfeat(WI_ROWS), qtok(D_A),
                  pl.BlockSpec((2, N_HEADS_A, tq, tq), lambda bi, i: (0, 0, 0, 0),
                               pipeline_mode=pl.Buffered(1))],
        out_specs=qtok(D_A),
        out_shape=jax.ShapeDtypeStruct((b, s, D_A), BF16),
        scratch_shapes=[pltpu.VMEM((nq, tq, tq), F32),
                        pltpu.VMEM((tq, tq), BF16),
                        pltpu.VMEM((N_HEADS_A, LANES, tq), BF16),
                        pltpu.VMEM((N_HEADS_A, tq), F32),
                        pltpu.VMEM((N_HEADS_A, tq), F32),
                        pltpu.VMEM((D_A, tq), F32),
                        pltpu.VMEM((1, tq), jnp.int32)],
        compiler_params=pltpu.CompilerParams(dimension_semantics=("parallel", "arbitrary"),
                                             vmem_limit_bytes=VMEM_LIMIT),
        name="dsa",
    )(qat, ka, vat, qit, ki, wit, ga, bias_a)


def _dil_kernel(qb_blocks, q_ref, kc_ref, kp_ref, vc_ref, vp_ref, bias_ref, o_ref, lse_ref):
    j = pl.program_id(2)
    lane = lax.broadcasted_iota(jnp.int32, (BLOCK, LANES), 1)
    low_half = lane < HEAD_DIM
    for blk in range(qb_blocks):
        rows = slice(blk * BLOCK, (blk + 1) * BLOCK)
        if blk == 0:
            variant = jnp.where(j == 0, 1, 0)
        else:
            variant = 0
        lse_tile = jnp.zeros((BLOCK, LANES), F32)
        for p in range(N_HEADS_B // 2):
            lanes = slice(p * LANES, (p + 1) * LANES)
            if blk == 0:
                kband = jnp.concatenate([kp_ref[0, :, lanes], kc_ref[0, 0:BLOCK, lanes]], axis=0)
                vband = jnp.concatenate([vp_ref[0, :, lanes], vc_ref[0, 0:BLOCK, lanes]], axis=0)
            else:
                band = slice((blk - 1) * BLOCK, (blk + 1) * BLOCK)
                kband = kc_ref[0, band, lanes]
                vband = vc_ref[0, band, lanes]
            qp = q_ref[0, rows, lanes]
            zero = jnp.zeros_like(qp)
            outs = []
            for e in range(2):
                h = 2 * p + e
                qz = jnp.where(low_half, qp, zero) if e == 0 else jnp.where(low_half, zero, qp)
                logits = _dot_nt(qz, kband) + bias_ref[variant, h]
                m = jnp.max(logits, axis=1, keepdims=True)
                pr = jnp.exp(logits - m)
                den = jnp.sum(pr, axis=1, keepdims=True)
                num = _dot(pr.astype(BF16), vband)
                outs.append(num / den)
                lse_tile = lse_tile + jnp.where(lane == h, m + jnp.log(den), 0.0)
            o_ref[0, rows, lanes] = jnp.where(low_half, outs[0], outs[1])
        lse_ref[0, rows, :] = lse_tile


def _dilated_pattern(qv, kv, vv, bias_p, dil, qb_blocks):
    b, n_sub, _ = qv.shape
    tu = qb_blocks * BLOCK
    nj = n_sub // tu
    cur = pl.BlockSpec((1, tu, D_B), lambda bi, r, j: (bi, j, r))
    prev = pl.BlockSpec((1, BLOCK, D_B), lambda bi, r, j: (bi, jnp.maximum(j * qb_blocks - 1, 0), r))
    return pl.pallas_call(
        functools.partial(_dil_kernel, qb_blocks),
        grid=(b, dil, nj),
        in_specs=[cur, cur, prev, cur, prev,
                  pl.BlockSpec((2, N_HEADS_B, BLOCK, 2 * BLOCK), lambda bi, r, j: (0, 0, 0, 0),
                               pipeline_mode=pl.Buffered(1))],
        out_specs=(pl.BlockSpec((1, tu, D_B), lambda bi, r, j: (bi, j, r)),
                   pl.BlockSpec((1, tu, LANES), lambda bi, r, j: (bi, j, r))),
        out_shape=(jax.ShapeDtypeStruct((b, n_sub, dil * D_B), F32),
                   jax.ShapeDtypeStruct((b, n_sub, dil * LANES), F32)),
        compiler_params=pltpu.CompilerParams(dimension_semantics=("parallel", "parallel", "arbitrary"),
                                             vmem_limit_bytes=VMEM_LIMIT),
        name=f"dilated_d{dil}",
    )(qv, kv, kv, vv, vv, bias_p)


def _out_kernel(npat, x_ref, a_ref, gb_ref, w_ref, e_ref, *refs):
    o_refs = refs[0:npat]
    lse_refs = refs[npat:2 * npat]
    out_ref = refs[2 * npat]
    lses = [r[...] for r in lse_refs]
    top = functools.reduce(jnp.maximum, lses)
    ws = [jnp.exp(l - top) for l in lses]
    tot = functools.reduce(lambda a, c: a + c, ws)
    e = e_ref[...]
    bmix = jnp.zeros(o_refs[0].shape, F32)
    for wgt, o_ref in zip(ws, o_refs):
        wn = wgt / tot
        hi = wn.astype(BF16)
        lo = (wn - hi.astype(F32)).astype(BF16)
        bmix = bmix + (_dot(hi, e) + _dot(lo, e)) * o_ref[...]
    bg = (bmix * gb_ref[...].astype(F32)).astype(BF16)
    out_ref[...] = x_ref[...] + _dot(a_ref[...], w_ref[0:D_A, :]) + _dot(bg, w_ref[D_A:D_A + D_B, :])


def _out_proj(x2, a_g, gb, w_out, expand, os_, lses, tm):
    n, d = x2.shape
    npat = len(os_)
    row = lambda i: (i, 0)
    const = lambda i: (0, 0)
    one_buf = dict(pipeline_mode=pl.Buffered(1))
    in_specs = ([pl.BlockSpec((tm, d), row), pl.BlockSpec((tm, D_A), row), pl.BlockSpec((tm, D_B), row),
                 pl.BlockSpec((D_A + D_B, d), const, **one_buf),
                 pl.BlockSpec((LANES, D_B), const, **one_buf)]
                + [pl.BlockSpec((tm, D_B), row)] * npat + [pl.BlockSpec((tm, LANES), row)] * npat)
    return pl.pallas_call(
        functools.partial(_out_kernel, npat),
        grid=(n // tm,),
        in_specs=in_specs,
        out_specs=pl.BlockSpec((tm, d), row),
        out_shape=jax.ShapeDtypeStruct((n, d), F32),
        compiler_params=pltpu.CompilerParams(dimension_semantics=("parallel",),
                                             vmem_limit_bytes=VMEM_LIMIT),
        name="merge_outproj",
    )(x2, a_g, gb, w_out, expand, *os_, *lses)


def _pack_w_in(w):
    qa, ka, va, za = (w[:, j * D_A:(j + 1) * D_A] for j in range(4))
    o = 4 * D_A
    qi = w[:, o:o + D_QI]
    ki = w[:, o + D_QI:o + D_QI + IDX_DIM]
    wi = w[:, o + D_QI + IDX_DIM:o + D_QI + IDX_DIM + N_IDX_HEADS]
    wb = w[:, o + D_QI + IDX_DIM + N_IDX_HEADS:]
    wn = jnp.concatenate([ka, za, ki, ki, wb], axis=1).astype(BF16)
    wi = jnp.pad(wi, ((0, 0), (0, WI_ROWS - N_IDX_HEADS)))
    wt = jnp.concatenate([qa, va, qi, wi], axis=1).T.astype(BF16)
    return wn, wt


def _bias_lookup(table, bucket):
    onehot = jax.nn.one_hot(bucket, N_BUCKETS, dtype=F32)
    return jnp.einsum('...n,nh->...h', onehot, table.astype(F32), precision=lax.Precision.HIGHEST)


def _dsa_bias_tiles(bias_a, tq):
    k = jnp.arange(tq)[:, None]
    q = jnp.arange(tq)[None, :]
    far = bias_a[N_BUCKETS - 1].astype(F32)
    prev = _bias_lookup(bias_a, _rel_bucket(q - k + tq)) - far
    diag = _bias_lookup(bias_a, _rel_bucket(q - k)) - far
    return (jnp.transpose(jnp.stack([prev, diag]), (0, 3, 1, 2)) * LOG2E).astype(BF16)


def _dil_bias_tiles(bias_b, window, dil):
    steps = window // dil
    a = jnp.arange(BLOCK)[:, None]
    c = jnp.arange(2 * BLOCK)[None, :]
    diff = a - c + BLOCK
    band = (diff >= 0) & (diff <= steps)
    bias = jnp.transpose(_bias_lookup(bias_b, _rel_bucket(diff * dil)), (2, 0, 1))
    normal = jnp.where(band[None], bias, NEG_BIG)
    first = jnp.where((band & (c >= BLOCK))[None], bias, NEG_BIG)
    return jnp.stack([normal, first])


def kernel(x, norm_gain, w_in, w_out, rel_bias, q_norm_a, k_norm_a, q_norm_b, k_norm_b):
    b, s, d = x.shape
    n = b * s
    depth = norm_gain.shape[0]
    tq = min(256, s)
    tm = min(512, s)
    bias_a = rel_bias[:, :N_HEADS_A]
    bias_b = rel_bias[:, N_HEADS_A:]
    eye_blocks = (jnp.arange(D_A)[:, None] // HEAD_DIM == jnp.arange(D_A)[None, :] // HEAD_DIM).astype(BF16)
    expand = (jnp.arange(LANES)[:, None] == jnp.arange(D_B)[None, :] // HEAD_DIM).astype(BF16)
    assert tq >= MAX_DISTANCE
    dsa_bias = _dsa_bias_tiles(bias_a, tq)
    dil_bias = [_dil_bias_tiles(bias_b, wnd, dil) for wnd, dil in DIL_PATTERNS]

    for layer in range(depth):
        gn = jnp.stack([jnp.tile(g[layer], N_HEADS_A) for g in (q_norm_a, k_norm_a, q_norm_b, k_norm_b)])
        wn, wt = _pack_w_in(w_in[layer])
        (qat, ka, vat, ga, qit, ki, wit, qb, kb, vb, gb) = _inproj(
            x, norm_gain[layer][None, :], wn, wt, eye_blocks, gn, gn[0][:, None], tm, tq)
        a_g = _dsa(qat, ka, vat, qit, ki, wit, ga, dsa_bias, tq)

        os_, lses = [], []
        for (wnd, dil), bias_p in zip(DIL_PATTERNS, dil_bias):
            assert wnd // dil == BLOCK and s % (dil * BLOCK) == 0
            n_sub = s // dil
            view = lambda t: t.reshape(b, n_sub, dil * D_B)
            qb_blocks = min(4, n_sub // BLOCK)
            o, lse = _dilated_pattern(view(qb), view(kb), view(vb), bias_p, dil, qb_blocks)
            os_.append(o.reshape(n, D_B))
            lses.append(lse.reshape(n, LANES))
        x = _out_proj(x.reshape(n, d), a_g.reshape(n, D_A), gb.reshape(n, D_B), w_out[layer].astype(BF16),
                      expand, os_, lses, tm).reshape(b, s, d)
    return x
```

```python
import functools
import math

import jax
import jax.numpy as jnp
from jax import lax
from jax.experimental import pallas as pl
from jax.experimental.pallas import tpu as pltpu

HEAD_DIM = 64
N_HEADS_A = 8
N_HEADS_B = 8
D_A = N_HEADS_A * HEAD_DIM
D_B = N_HEADS_B * HEAD_DIM
N_IDX_HEADS = 4
IDX_DIM = 64
D_QI = N_IDX_HEADS * IDX_DIM
TOPK_MAX = 256
DIL_PATTERNS = ((128, 1), (512, 4), (2048, 16))
BLOCK = 128
N_BUCKETS = 32
MAX_DISTANCE = 128
EPS = 1e-6

LANES = 128
SUBLANES = 8
WI_ROWS = 16
NEG_BIG = -(2.0 ** 100)
LOG2E = math.log2(math.e)
DSA_TQ = 256
DSA_TK = 256
VMEM_LIMIT = 56 * 1024 * 1024

F32 = jnp.float32
BF16 = jnp.bfloat16
NT_DIMS = (((1,), (1,)), ((), ()))


def _dot(a, b):
    return jnp.dot(a, b, preferred_element_type=F32)


def _dot_nt(a, b):
    return lax.dot_general(a, b, NT_DIMS, preferred_element_type=F32)


def _rel_bucket(dist):
    max_exact = N_BUCKETS // 2
    d = jnp.maximum(dist, 0)
    df = jnp.maximum(d, 1).astype(F32)
    large = max_exact + (jnp.log(df / max_exact) / math.log(MAX_DISTANCE / max_exact)
                         * (N_BUCKETS - max_exact)).astype(jnp.int32)
    large = jnp.minimum(large, N_BUCKETS - 1)
    return jnp.where(d < max_exact, d, large)


def _inproj_kernel(tk, dils, x_ref, g_ref, wn_ref, wt_ref, bd_ref, gn_ref, gcol_ref,
                   qat_ref, ka_ref, vat_ref, ga_ref, qit_ref, ki_ref, wit_ref, gb_ref, *rest):
    n_b = 3 * len(dils)
    b_refs = [rest[j * len(dils):(j + 1) * len(dils)] for j in range(3)]
    slab_ref = rest[n_b]
    x = x_ref[0]
    tm = x.shape[0]
    r = lax.rsqrt(jnp.mean(x * x, axis=-1, keepdims=True) + EPS)
    xn = (x * r * g_ref[...]).astype(BF16)
    bd = bd_ref[...]

    def head_norm(q, gain, scale):
        ss = _dot((q * q).astype(BF16), bd)
        return q * lax.rsqrt(ss * (1.0 / HEAD_DIM) + EPS) * (gain * scale)

    def silu(z):
        return z * jax.nn.sigmoid(z)

    p = _dot(xn, wn_ref[:, 0:2 * D_A + LANES])
    ka_ref[0] = head_norm(p[:, 0:D_A], gn_ref[1:2, :], 1.0).astype(BF16)
    ga_ref[0] = silu(p[:, D_A:2 * D_A]).astype(BF16)
    ki_ref[0] = p[:, 2 * D_A:2 * D_A + LANES].astype(BF16)
    c0 = 2 * D_A + LANES
    p = _dot(xn, wn_ref[:, c0:c0 + 4 * D_B])
    gb_ref[0] = silu(p[:, 3 * D_B:4 * D_B]).astype(BF16)

    def emit_views(val, refs):
        for d, ref in zip(dils, refs):
            if d == 1:
                ref[0] = val.astype(BF16)
        if all(d == 1 for d in dils):
            return
        for pr in range(D_B // LANES):
            slab_ref[pr] = val[:, pr * LANES:(pr + 1) * LANES]
        for d, ref in zip(dils, refs):
            if d == 1:
                continue
            rows = tm // d
            for r in range(d):
                for pr in range(D_B // LANES):
                    lane0 = r * D_B + pr * LANES
                    ref[0, :, lane0:lane0 + LANES] = slab_ref[pr, pl.ds(r, rows, stride=d), :].astype(BF16)

    emit_views(head_norm(p[:, 0:D_B], gn_ref[2:3, :], HEAD_DIM ** -0.5), b_refs[0])
    emit_views(head_norm(p[:, D_B:2 * D_B], gn_ref[3:4, :], 1.0), b_refs[1])
    emit_views(p[:, 2 * D_B:3 * D_B], b_refs[2])

    qt = _dot_nt(wt_ref[0:D_A, :], xn)
    ss = _dot(bd, (qt * qt).astype(BF16))
    qt = qt * lax.rsqrt(ss * (1.0 / HEAD_DIM) + EPS) * (gcol_ref[...] * (HEAD_DIM ** -0.5 * LOG2E))
    qat_ref[0] = qt.astype(BF16)
    vt = _dot_nt(wt_ref[D_A:2 * D_A, :], xn)
    for cc in range(tm // tk):
        vat_ref[0, cc] = vt[:, cc * tk:(cc + 1) * tk].astype(BF16)
    qit_ref[0] = _dot_nt(wt_ref[2 * D_A:2 * D_A + D_QI, :], xn).astype(BF16)
    wit_ref[0] = _dot_nt(wt_ref[2 * D_A + D_QI:2 * D_A + D_QI + WI_ROWS, :], xn)


def _inproj(x, g, wn, wt, bd, gn, gcol, tm, tk, dils):
    b, s, d = x.shape
    const = lambda bi, j: (0, 0)
    tok = lambda c: pl.BlockSpec((1, tm, c), lambda bi, j: (bi, j, 0))
    feat = lambda c: pl.BlockSpec((1, c, tm), lambda bi, j: (bi, 0, j))
    one_buf = dict(pipeline_mode=pl.Buffered(1))
    sds = jax.ShapeDtypeStruct
    view_shapes = tuple(sds((b, s // dl, dl * D_B), BF16) for dl in dils)
    view_specs = tuple(pl.BlockSpec((1, tm // dl, dl * D_B), lambda bi, j: (bi, j, 0)) for dl in dils)
    out_shapes = (sds((b, D_A, s), BF16), sds((b, s, D_A), BF16), sds((b, s // tk, D_A, tk), BF16),
                  sds((b, s, D_A), BF16), sds((b, D_QI, s), BF16), sds((b, s, LANES), BF16),
                  sds((b, WI_ROWS, s), F32), sds((b, s, D_B), BF16)) + view_shapes * 3
    out_specs = (feat(D_A), tok(D_A),
                 pl.BlockSpec((1, tm // tk, D_A, tk), lambda bi, j: (bi, j, 0, 0)),
                 tok(D_A), feat(D_QI), tok(LANES), feat(WI_ROWS), tok(D_B)) + view_specs * 3
    return pl.pallas_call(
        functools.partial(_inproj_kernel, tk, dils),
        grid=(b, s // tm),
        scratch_shapes=[pltpu.VMEM((D_B // LANES, tm, LANES), F32)],
        in_specs=[tok(d),
                  pl.BlockSpec((1, d), const, **one_buf),
                  pl.BlockSpec(wn.shape, const, **one_buf),
                  pl.BlockSpec(wt.shape, const, **one_buf),
                  pl.BlockSpec((D_A, D_A), const, **one_buf),
                  pl.BlockSpec((4, D_A), const, **one_buf),
                  pl.BlockSpec((D_A, 1), const, **one_buf)],
        out_specs=out_specs,
        out_shape=out_shapes,
        compiler_params=pltpu.CompilerParams(dimension_semantics=("parallel", "parallel"),
                                             vmem_limit_bytes=VMEM_LIMIT),
        name="inproj",
    )(x, g, wn, wt, bd, gn, gcol)


def _ordered_bits_to_float(u):
    key = u ^ jnp.int32(-2 ** 31)
    bits = key ^ ((key >> 31) & jnp.int32(0x7FFFFFFF))
    return lax.bitcast_convert_type(bits, F32)


def _fold_rows(x):
    parts = [None] * 4
    for r in range(x.shape[0] // SUBLANES):
        blk = x[r * SUBLANES:(r + 1) * SUBLANES, :]
        parts[r % 4] = blk if parts[r % 4] is None else parts[r % 4] + blk
    return (parts[0] + parts[1]) + (parts[2] + parts[3])


def _dsa_kernel(topk, tq, tk, nkc, qat_ref, ka_ref, vat_ref, qit_ref, ki_ref, wit_ref, ga_ref, bias_ref,
                out_ref, sc_ref, madd_ref, qz_ref, m_ref, l_ref, acc_ref, jt_ref):
    i = pl.program_id(1)
    ratio = tq // tk
    nchunks = (i + 1) * ratio
    idx_scale = (N_IDX_HEADS * IDX_DIM) ** -0.5
    top_half = lax.broadcasted_iota(jnp.int32, (LANES, tq), 0) < HEAD_DIM

    def pad_heads(ref, npairs):
        out = []
        for p in range(npairs):
            blk = ref[0, p * LANES:(p + 1) * LANES, :]
            zero = jnp.zeros_like(blk)
            out.append(jnp.where(top_half, blk, zero))
            out.append(jnp.where(top_half, zero, blk))
        return out

    qiz = pad_heads(qit_ref, N_IDX_HEADS // 2)
    w = wit_ref[0]
    krow = lax.broadcasted_iota(jnp.int32, (tk, tq), 0)
    qcol_g = i * tq + lax.broadcasted_iota(jnp.int32, (tk, tq), 1)

    def score_chunk(c, carry):
        kc = ki_ref[0, pl.ds(pl.multiple_of(c * tk, tk), tk), :]
        acc = jnp.zeros((tk, tq), F32)
        for h in range(N_IDX_HEADS):
            acc = acc + jnp.maximum(_dot(kc, qiz[h]), 0.0) * w[h:h + 1, :]
        s = acc * idx_scale + 0.0
        sc_ref[c] = jnp.where(krow + c * tk <= qcol_g, s, -jnp.inf)
        return carry

    lax.fori_loop(0, nchunks, score_chunk, 0)

    def count_pass(pred):
        def body(c, acc):
            return acc + _fold_rows(jnp.where(pred(sc_ref[c], c), 1.0, 0.0))
        acc = lax.fori_loop(0, nchunks, body, jnp.zeros((SUBLANES, tq), F32))
        return jnp.sum(acc, axis=0, keepdims=True)

    def bit_step(it, cur):
        cand = cur | lax.shift_left(jnp.int32(1), 31 - it)
        candf = _ordered_bits_to_float(cand)
        cnt = count_pass(lambda s, c: s >= candf)
        return jnp.where(cnt >= float(topk), cand, cur)

    cur = lax.fori_loop(0, 32, bit_step, jnp.zeros((1, tq), jnp.int32))
    short = (cur & jnp.int32(-2 ** 23)) == 0
    thr = jnp.where(short, -jnp.inf, _ordered_bits_to_float(cur))
    cnt_gt = count_pass(lambda s, c: s > thr)
    cnt_ge = count_pass(lambda s, c: s >= thr)
    need_m1 = float(topk) - cnt_gt - 1.0
    tie_excess = jnp.where(short, 0.0, cnt_ge - float(topk))

    jt_ref[...] = jnp.where(short, -1, jnp.int32(2 ** 30))

    @pl.when(jnp.max(tie_excess) > 0.0)
    def _():
        nbits = max(1, (nkc * tk - 1).bit_length())

        def tie_step(it, curj):
            cand = curj | lax.shift_left(jnp.int32(1), nbits - 1 - it)
            cnt = count_pass(lambda s, c: (s == thr) & (krow < cand - c * tk))
            return jnp.where(cnt <= need_m1, cand, curj)

        curj = lax.fori_loop(0, nbits, tie_step, jnp.zeros((1, tq), jnp.int32))
        jt_ref[...] = jnp.where(short, -1, curj)

    jt = jt_ref[...]

    for h, qz in enumerate(pad_heads(qat_ref, N_HEADS_A // 2)):
        qz_ref[h] = qz
    m_ref[...] = jnp.full(m_ref.shape, NEG_BIG, F32)
    l_ref[...] = jnp.zeros(l_ref.shape, F32)
    acc_ref[...] = jnp.zeros(acc_ref.shape, F32)
    ones_row = jnp.ones((SUBLANES, tk), BF16)

    def attn_chunk(c, variant):
        k0 = pl.multiple_of(c * tk, tk)
        s = sc_ref[c]
        sel = (s > thr) | ((s == thr) & (krow <= jt - c * tk))
        madd_ref[...] = jnp.where(sel, 0.0, NEG_BIG).astype(BF16)
        for h in range(N_HEADS_A):
            p = h // 2
            kp = ka_ref[0, pl.ds(k0, tk), p * LANES:(p + 1) * LANES]
            logits = _dot(kp, qz_ref[h]).astype(BF16) + madd_ref[...]
            if variant is not None:
                logits = logits + bias_ref[variant, h]
            hrow = slice(h, h + 1)
            m_old = m_ref[hrow, :]
            m_new = jnp.maximum(m_old, jnp.max(logits, axis=0, keepdims=True).astype(F32))
            alpha = jnp.exp2(m_old - m_new)
            pr = jnp.exp2(logits - m_new.astype(BF16))
            l_ref[hrow, :] = alpha * l_ref[hrow, :] + _dot(ones_row, pr)[0:1, :]
            m_ref[hrow, :] = m_new
            rows = slice(h * HEAD_DIM, (h + 1) * HEAD_DIM)
            acc_ref[rows, :] = alpha * acc_ref[rows, :] + _dot(vat_ref[0, c, rows, :], pr)

    first_near = i * ratio - 1
    n_far = jnp.maximum(first_near, 0)

    def far_body(c, carry):
        attn_chunk(c, None)
        return carry

    def near_body(c, carry):
        attn_chunk(c, c - first_near)
        return carry

    lax.fori_loop(0, n_far, far_body, 0)
    lax.fori_loop(n_far, nchunks, near_body, 0)

    outs = []
    for h in range(N_HEADS_A):
        rows = slice(h * HEAD_DIM, (h + 1) * HEAD_DIM)
        outs.append(acc_ref[rows, :] / l_ref[h:h + 1, :])
    o = jnp.concatenate(outs, axis=0).T
    out_ref[0] = (o * ga_ref[0].astype(F32)).astype(BF16)


def _dsa(qat, ka, vat, qit, ki, wit, ga, bias_a, tq, tk):
    b, s, _ = ka.shape
    topk = min(TOPK_MAX, s // 4)
    nq = s // tq
    nkc = s // tk
    qfeat = lambda c: pl.BlockSpec((1, c, tq), lambda bi, i: (bi, 0, i))
    qtok = lambda c: pl.BlockSpec((1, tq, c), lambda bi, i: (bi, i, 0))
    full = lambda c: pl.BlockSpec((1, s, c), lambda bi, i: (bi, 0, 0))
    return pl.pallas_call(
        functools.partial(_dsa_kernel, topk, tq, tk, nkc),
        grid=(b, nq),
        in_specs=[qfeat(D_A), full(D_A),
                  pl.BlockSpec((1, nkc, D_A, tk), lambda bi, i: (bi, 0, 0, 0)),
                  qfeat(D_QI), full(LANES), qfeat(WI_ROWS), qtok(D_A),
                  pl.BlockSpec(bias_a.shape, lambda bi, i: (0, 0, 0, 0), pipeline_mode=pl.Buffered(1))],
        out_specs=qtok(D_A),
        out_shape=jax.ShapeDtypeStruct((b, s, D_A), BF16),
        scratch_shapes=[pltpu.VMEM((nkc, tk, tq), F32),
                        pltpu.VMEM((tk, tq), BF16),
                        pltpu.VMEM((N_HEADS_A, LANES, tq), BF16),
                        pltpu.VMEM((N_HEADS_A, tq), F32),
                        pltpu.VMEM((N_HEADS_A, tq), F32),
                        pltpu.VMEM((D_A, tq), F32),
                        pltpu.VMEM((1, tq), jnp.int32)],
        compiler_params=pltpu.CompilerParams(dimension_semantics=("parallel", "arbitrary"),
                                             vmem_limit_bytes=VMEM_LIMIT),
        name="dsa",
    )(qat, ka, vat, qit, ki, wit, ga, bias_a)


def _dil_kernel(qb_blocks, q_ref, kc_ref, kp_ref, vc_ref, vp_ref, bias_ref, o_ref, lse_ref):
    j = pl.program_id(2)
    lane = lax.broadcasted_iota(jnp.int32, (BLOCK, LANES), 1)
    low_half = lane < HEAD_DIM
    for blk in range(qb_blocks):
        rows = slice(blk * BLOCK, (blk + 1) * BLOCK)
        if blk == 0:
            variant = jnp.where(j == 0, 1, 0)
        else:
            variant = 0
        lse_tile = jnp.zeros((BLOCK, LANES), F32)
        for p in range(N_HEADS_B // 2):
            lanes = slice(p * LANES, (p + 1) * LANES)
            if blk == 0:
                kband = jnp.concatenate([kp_ref[0, :, lanes], kc_ref[0, 0:BLOCK, lanes]], axis=0)
                vband = jnp.concatenate([vp_ref[0, :, lanes], vc_ref[0, 0:BLOCK, lanes]], axis=0)
            else:
                band = slice((blk - 1) * BLOCK, (blk + 1) * BLOCK)
                kband = kc_ref[0, band, lanes]
                vband = vc_ref[0, band, lanes]
            qp = q_ref[0, rows, lanes]
            zero = jnp.zeros_like(qp)
            outs = []
            for e in range(2):
                h = 2 * p + e
                qz = jnp.where(low_half, qp, zero) if e == 0 else jnp.where(low_half, zero, qp)
                logits = _dot_nt(qz, kband) + bias_ref[variant, h]
                m = jnp.max(logits, axis=1, keepdims=True)
                pr = jnp.exp(logits - m)
                den = jnp.sum(pr, axis=1, keepdims=True)
                num = _dot(pr.astype(BF16), vband)
                outs.append(num / den)
                lse_tile = lse_tile + jnp.where(lane == h, m + jnp.log(den), 0.0)
            o_ref[0, rows, lanes] = jnp.where(low_half, outs[0], outs[1])
        lse_ref[0, rows, :] = lse_tile


def _dilated_pattern(qv, kv, vv, bias_p, dil, qb_blocks):
    b, n_sub, _ = qv.shape
    tu = qb_blocks * BLOCK
    nj = n_sub // tu
    cur = pl.BlockSpec((1, tu, D_B), lambda bi, r, j: (bi, j, r))
    prev = pl.BlockSpec((1, BLOCK, D_B), lambda bi, r, j: (bi, jnp.maximum(j * qb_blocks - 1, 0), r))
    return pl.pallas_call(
        functools.partial(_dil_kernel, qb_blocks),
        grid=(b, dil, nj),
        in_specs=[cur, cur, prev, cur, prev,
                  pl.BlockSpec((2, N_HEADS_B, BLOCK, 2 * BLOCK), lambda bi, r, j: (0, 0, 0, 0),
                               pipeline_mode=pl.Buffered(1))],
        out_specs=(pl.BlockSpec((1, tu, D_B), lambda bi, r, j: (bi, j, r)),
                   pl.BlockSpec((1, tu, LANES), lambda bi, r, j: (bi, j, r))),
        out_shape=(jax.ShapeDtypeStruct((b, n_sub, dil * D_B), F32),
                   jax.ShapeDtypeStruct((b, n_sub, dil * LANES), F32)),
        compiler_params=pltpu.CompilerParams(dimension_semantics=("parallel", "parallel", "arbitrary"),
                                             vmem_limit_bytes=VMEM_LIMIT),
        name=f"dilated_d{dil}",
    )(qv, kv, kv, vv, vv, bias_p)


def _out_kernel(dils, x_ref, a_ref, gb_ref, w_ref, e_ref, *refs):
    npat = len(dils)
    o_refs = refs[0:npat]
    lse_refs = refs[npat:2 * npat]
    out_ref = refs[2 * npat]
    oslab_ref, lslab_ref = refs[2 * npat + 1:2 * npat + 3]
    tm = x_ref.shape[0]

    def natural_lse(j, lse_ref, d):
        if d == 1:
            return lse_ref[...]
        for r in range(d):
            lslab_ref[j, pl.ds(r, tm // d, stride=d), :] = lse_ref[:, r * LANES:(r + 1) * LANES]
        return lslab_ref[j]

    def natural_o(o_ref, d):
        if d == 1:
            return o_ref[...]
        for r in range(d):
            for pr in range(D_B // LANES):
                lane0 = r * D_B + pr * LANES
                oslab_ref[pr, pl.ds(r, tm // d, stride=d), :] = o_ref[:, lane0:lane0 + LANES]
        return jnp.concatenate([oslab_ref[pr] for pr in range(D_B // LANES)], axis=1)

    lses = [natural_lse(j, lse_ref, d) for j, (lse_ref, d) in enumerate(zip(lse_refs, dils))]
    top = functools.reduce(jnp.maximum, lses)
    ws = [jnp.exp(l - top) for l in lses]
    tot = functools.reduce(lambda a, c: a + c, ws)
    e = e_ref[...]
    bmix = jnp.zeros((tm, D_B), F32)
    for wgt, o_ref, d in zip(ws, o_refs, dils):
        wn = wgt / tot
        hi = wn.astype(BF16)
        lo = (wn - hi.astype(F32)).astype(BF16)
        bmix = bmix + (_dot(hi, e) + _dot(lo, e)) * natural_o(o_ref, d)
    bg = (bmix * gb_ref[...].astype(F32)).astype(BF16)
    out_ref[...] = x_ref[...] + _dot(a_ref[...], w_ref[0:D_A, :]) + _dot(bg, w_ref[D_A:D_A + D_B, :])


def _out_proj(x2, a_g, gb, w_out, expand, os_, lses, dils, tm):
    n, d = x2.shape
    row = lambda i: (i, 0)
    const = lambda i: (0, 0)
    one_buf = dict(pipeline_mode=pl.Buffered(1))
    in_specs = ([pl.BlockSpec((tm, d), row), pl.BlockSpec((tm, D_A), row), pl.BlockSpec((tm, D_B), row),
                 pl.BlockSpec((D_A + D_B, d), const, **one_buf),
                 pl.BlockSpec((LANES, D_B), const, **one_buf)]
                + [pl.BlockSpec((tm // dl, dl * D_B), row) for dl in dils]
                + [pl.BlockSpec((tm // dl, dl * LANES), row) for dl in dils])
    return pl.pallas_call(
        functools.partial(_out_kernel, dils),
        grid=(n // tm,),
        in_specs=in_specs,
        out_specs=pl.BlockSpec((tm, d), row),
        out_shape=jax.ShapeDtypeStruct((n, d), F32),
        scratch_shapes=[pltpu.VMEM((D_B // LANES, tm, LANES), F32), pltpu.VMEM((len(dils), tm, LANES), F32)],
        compiler_params=pltpu.CompilerParams(dimension_semantics=("parallel",),
                                             vmem_limit_bytes=VMEM_LIMIT),
        name="merge_outproj",
    )(x2, a_g, gb, w_out, expand, *os_, *lses)


def _pack_w_in(w):
    qa, ka, va, za = (w[:, j * D_A:(j + 1) * D_A] for j in range(4))
    o = 4 * D_A
    qi = w[:, o:o + D_QI]
    ki = w[:, o + D_QI:o + D_QI + IDX_DIM]
    wi = w[:, o + D_QI + IDX_DIM:o + D_QI + IDX_DIM + N_IDX_HEADS]
    wb = w[:, o + D_QI + IDX_DIM + N_IDX_HEADS:]
    wn = jnp.concatenate([ka, za, ki, ki, wb], axis=1).astype(BF16)
    wi = jnp.pad(wi, ((0, 0), (0, WI_ROWS - N_IDX_HEADS)))
    wt = jnp.concatenate([qa, va, qi, wi], axis=1).T.astype(BF16)
    return wn, wt


def _bias_lookup(table, bucket):
    onehot = jax.nn.one_hot(bucket, N_BUCKETS, dtype=F32)
    return jnp.einsum('...n,nh->...h', onehot, table.astype(F32), precision=lax.Precision.HIGHEST)


def _dsa_bias_tiles(bias_a, tq, tk):
    k = jnp.arange(tk)[:, None]
    q = jnp.arange(tq)[None, :]
    far = bias_a[N_BUCKETS - 1].astype(F32)
    tiles = [_bias_lookup(bias_a, _rel_bucket(q - k + tk * (1 - v))) - far for v in range(1 + tq // tk)]
    return (jnp.transpose(jnp.stack(tiles), (0, 3, 1, 2)) * LOG2E).astype(BF16)


def _dil_bias_tiles(bias_b, window, dil):
    steps = window // dil
    a = jnp.arange(BLOCK)[:, None]
    c = jnp.arange(2 * BLOCK)[None, :]
    diff = a - c + BLOCK
    band = (diff >= 0) & (diff <= steps)
    bias = jnp.transpose(_bias_lookup(bias_b, _rel_bucket(diff * dil)), (2, 0, 1))
    normal = jnp.where(band[None], bias, NEG_BIG)
    first = jnp.where((band & (c >= BLOCK))[None], bias, NEG_BIG)
    return jnp.stack([normal, first])


def kernel(x, norm_gain, w_in, w_out, rel_bias, q_norm_a, k_norm_a, q_norm_b, k_norm_b):
    b, s, d = x.shape
    n = b * s
    depth = norm_gain.shape[0]
    tq = min(DSA_TQ, s)
    tk = min(DSA_TK, s)
    tm = min(512, s)
    bias_a = rel_bias[:, :N_HEADS_A]
    bias_b = rel_bias[:, N_HEADS_A:]
    eye_blocks = (jnp.arange(D_A)[:, None] // HEAD_DIM == jnp.arange(D_A)[None, :] // HEAD_DIM).astype(BF16)
    expand = (jnp.arange(LANES)[:, None] == jnp.arange(D_B)[None, :] // HEAD_DIM).astype(BF16)
    assert tk >= MAX_DISTANCE and tq % tk == 0
    dsa_bias = _dsa_bias_tiles(bias_a, tq, tk)
    dil_bias = [_dil_bias_tiles(bias_b, wnd, dil) for wnd, dil in DIL_PATTERNS]
    dils = tuple(dil for _, dil in DIL_PATTERNS)

    for layer in range(depth):
        gn = jnp.stack([jnp.tile(g[layer], N_HEADS_A) for g in (q_norm_a, k_norm_a, q_norm_b, k_norm_b)])
        wn, wt = _pack_w_in(w_in[layer])
        npat = len(dils)
        outs = _inproj(x, norm_gain[layer][None, :], wn, wt, eye_blocks, gn, gn[0][:, None], tm, tk, dils)
        qat, ka, vat, ga, qit, ki, wit, gb = outs[:8]
        qviews, kviews, vviews = (outs[8 + j * npat:8 + (j + 1) * npat] for j in range(3))
        a_g = _dsa(qat, ka, vat, qit, ki, wit, ga, dsa_bias, tq, tk)

        os_, lses = [], []
        for j, ((wnd, dil), bias_p) in enumerate(zip(DIL_PATTERNS, dil_bias)):
            assert wnd // dil == BLOCK and s % (dil * BLOCK) == 0 and tm % (dil * 2 * SUBLANES) == 0
            n_sub = s // dil
            qb_blocks = min(4, n_sub // BLOCK)
            o, lse = _dilated_pattern(qviews[j], kviews[j], vviews[j], bias_p, dil, qb_blocks)
            os_.append(o.reshape(n // dil, dil * D_B))
            lses.append(lse.reshape(n // dil, dil * LANES))
        x = _out_proj(x.reshape(n, d), a_g.reshape(n, D_A), gb.reshape(n, D_B), w_out[layer].astype(BF16),
                      expand, os_, lses, dils, tm).reshape(b, s, d)
    return x
```

```python
import functools
import math

import jax
import jax.numpy as jnp
from jax import lax
from jax.experimental import pallas as pl
from jax.experimental.pallas import tpu as pltpu

HEAD_DIM = 64
N_HEADS_A = 8
N_HEADS_B = 8
D_A = N_HEADS_A * HEAD_DIM
D_B = N_HEADS_B * HEAD_DIM
N_IDX_HEADS = 4
IDX_DIM = 64
D_QI = N_IDX_HEADS * IDX_DIM
TOPK_MAX = 256
DIL_PATTERNS = ((128, 1), (512, 4), (2048, 16))
BLOCK = 128
N_BUCKETS = 32
MAX_DISTANCE = 128
EPS = 1e-6

LANES = 128
SUBLANES = 8
WI_ROWS = 16
NEG_BIG = -(2.0 ** 100)
LOG2E = math.log2(math.e)
DSA_TQ = 256
DSA_TK = 256
VMEM_LIMIT = 56 * 1024 * 1024

F32 = jnp.float32
BF16 = jnp.bfloat16
NT_DIMS = (((1,), (1,)), ((), ()))


def _dot(a, b):
    return jnp.dot(a, b, preferred_element_type=F32)


def _dot_nt(a, b):
    return lax.dot_general(a, b, NT_DIMS, preferred_element_type=F32)


def _rel_bucket(dist):
    max_exact = N_BUCKETS // 2
    d = jnp.maximum(dist, 0)
    df = jnp.maximum(d, 1).astype(F32)
    large = max_exact + (jnp.log(df / max_exact) / math.log(MAX_DISTANCE / max_exact)
                         * (N_BUCKETS - max_exact)).astype(jnp.int32)
    large = jnp.minimum(large, N_BUCKETS - 1)
    return jnp.where(d < max_exact, d, large)


def _inproj_kernel(tk, dils, x_ref, g_ref, wn_ref, wt_ref, bd_ref, gn_ref, gcol_ref,
                   qat_ref, ka_ref, vat_ref, ga_ref, qit_ref, ki_ref, wit_ref, gb_ref, *rest):
    n_b = 3 * len(dils)
    b_refs = [rest[j * len(dils):(j + 1) * len(dils)] for j in range(3)]
    slab_ref = rest[n_b]
    x = x_ref[0]
    tm = x.shape[0]
    r = lax.rsqrt(jnp.mean(x * x, axis=-1, keepdims=True) + EPS)
    xn = (x * r * g_ref[...]).astype(BF16)
    bd = bd_ref[...]

    def head_norm(q, gain, scale):
        ss = _dot((q * q).astype(BF16), bd)
        return q * lax.rsqrt(ss * (1.0 / HEAD_DIM) + EPS) * (gain * scale)

    def silu(z):
        return z * jax.nn.sigmoid(z)

    p = _dot(xn, wn_ref[:, 0:2 * D_A + LANES])
    ka_ref[0] = head_norm(p[:, 0:D_A], gn_ref[1:2, :], 1.0).astype(BF16)
    ga_ref[0] = silu(p[:, D_A:2 * D_A]).astype(BF16)
    ki_ref[0] = p[:, 2 * D_A:2 * D_A + LANES].astype(BF16)
    c0 = 2 * D_A + LANES
    p = _dot(xn, wn_ref[:, c0:c0 + 4 * D_B])
    gb_ref[0] = silu(p[:, 3 * D_B:4 * D_B]).astype(BF16)

    def emit_views(val, refs):
        for d, ref in zip(dils, refs):
            if d == 1:
                ref[0] = val.astype(BF16)
        if all(d == 1 for d in dils):
            return
        for pr in range(D_B // LANES):
            slab_ref[pr] = val[:, pr * LANES:(pr + 1) * LANES]
        for d, ref in zip(dils, refs):
            if d == 1:
                continue
            rows = tm // d
            for r in range(d):
                for pr in range(D_B // LANES):
                    lane0 = r * D_B + pr * LANES
                    ref[0, :, lane0:lane0 + LANES] = slab_ref[pr, pl.ds(r, rows, stride=d), :].astype(BF16)

    emit_views(head_norm(p[:, 0:D_B], gn_ref[2:3, :], HEAD_DIM ** -0.5), b_refs[0])
    emit_views(head_norm(p[:, D_B:2 * D_B], gn_ref[3:4, :], 1.0), b_refs[1])
    emit_views(p[:, 2 * D_B:3 * D_B], b_refs[2])

    qt = _dot_nt(wt_ref[0:D_A, :], xn)
    ss = _dot(bd, (qt * qt).astype(BF16))
    qt = qt * lax.rsqrt(ss * (1.0 / HEAD_DIM) + EPS) * (gcol_ref[...] * (HEAD_DIM ** -0.5 * LOG2E))
    qat_ref[0] = qt.astype(BF16)
    vt = _dot_nt(wt_ref[D_A:2 * D_A, :], xn)
    for cc in range(tm // tk):
        vat_ref[0, cc] = vt[:, cc * tk:(cc + 1) * tk].astype(BF16)
    qit_ref[0] = _dot_nt(wt_ref[2 * D_A:2 * D_A + D_QI, :], xn).astype(BF16)
    wit_ref[0] = _dot_nt(wt_ref[2 * D_A + D_QI:2 * D_A + D_QI + WI_ROWS, :], xn)


def _inproj(x, g, wn, wt, bd, gn, gcol, tm, tk, dils):
    b, s, d = x.shape
    const = lambda bi, j: (0, 0)
    tok = lambda c: pl.BlockSpec((1, tm, c), lambda bi, j: (bi, j, 0))
    feat = lambda c: pl.BlockSpec((1, c, tm), lambda bi, j: (bi, 0, j))
    one_buf = dict(pipeline_mode=pl.Buffered(1))
    sds = jax.ShapeDtypeStruct
    view_shapes = tuple(sds((b, s // dl, dl * D_B), BF16) for dl in dils)
    view_specs = tuple(pl.BlockSpec((1, tm // dl, dl * D_B), lambda bi, j: (bi, j, 0)) for dl in dils)
    out_shapes = (sds((b, D_A, s), BF16), sds((b, s, D_A), BF16), sds((b, s // tk, D_A, tk), BF16),
                  sds((b, s, D_A), BF16), sds((b, D_QI, s), BF16), sds((b, s, LANES), BF16),
                  sds((b, WI_ROWS, s), F32), sds((b, s, D_B), BF16)) + view_shapes * 3
    out_specs = (feat(D_A), tok(D_A),
                 pl.BlockSpec((1, tm // tk, D_A, tk), lambda bi, j: (bi, j, 0, 0)),
                 tok(D_A), feat(D_QI), tok(LANES), feat(WI_ROWS), tok(D_B)) + view_specs * 3
    return pl.pallas_call(
        functools.partial(_inproj_kernel, tk, dils),
        grid=(b, s // tm),
        scratch_shapes=[pltpu.VMEM((D_B // LANES, tm, LANES), F32)],
        in_specs=[tok(d),
                  pl.BlockSpec((1, d), const, **one_buf),
                  pl.BlockSpec(wn.shape, const, **one_buf),
                  pl.BlockSpec(wt.shape, const, **one_buf),
                  pl.BlockSpec((D_A, D_A), const, **one_buf),
                  pl.BlockSpec((4, D_A), const, **one_buf),
                  pl.BlockSpec((D_A, 1), const, **one_buf)],
        out_specs=out_specs,
        out_shape=out_shapes,
        compiler_params=pltpu.CompilerParams(dimension_semantics=("parallel", "parallel"),
                                             vmem_limit_bytes=VMEM_LIMIT),
        name="inproj",
    )(x, g, wn, wt, bd, gn, gcol)


def _ordered_bits_to_float(u):
    key = u ^ jnp.int32(-2 ** 31)
    bits = key ^ ((key >> 31) & jnp.int32(0x7FFFFFFF))
    return lax.bitcast_convert_type(bits, F32)


def _fold_rows(x):
    parts = [None] * 4
    for r in range(x.shape[0] // SUBLANES):
        blk = x[r * SUBLANES:(r + 1) * SUBLANES, :]
        parts[r % 4] = blk if parts[r % 4] is None else parts[r % 4] + blk
    return (parts[0] + parts[1]) + (parts[2] + parts[3])


def _dsa_kernel(topk, tq, tk, qat_ref, ka_ref, vat_ref, qit_ref, ki_ref, wit_ref, ga_ref, bias_ref,
                out_ref, sc_ref, madd_ref, qz_ref, m_ref, l_ref, acc_ref):
    i = pl.program_id(1)
    ratio = tq // tk
    nchunks = (i + 1) * ratio
    idx_scale = (N_IDX_HEADS * IDX_DIM) ** -0.5
    top_half = lax.broadcasted_iota(jnp.int32, (LANES, tq), 0) < HEAD_DIM

    def pad_heads(ref, npairs):
        out = []
        for p in range(npairs):
            blk = ref[0, p * LANES:(p + 1) * LANES, :]
            zero = jnp.zeros_like(blk)
            out.append(jnp.where(top_half, blk, zero))
            out.append(jnp.where(top_half, zero, blk))
        return out

    qiz = pad_heads(qit_ref, N_IDX_HEADS // 2)
    w = wit_ref[0]
    krow = lax.broadcasted_iota(jnp.int32, (tk, tq), 0)
    qcol_g = i * tq + lax.broadcasted_iota(jnp.int32, (tk, tq), 1)

    def score_chunk(c, carry):
        kc = ki_ref[0, pl.ds(pl.multiple_of(c * tk, tk), tk), :]
        acc = jnp.zeros((tk, tq), F32)
        for h in range(N_IDX_HEADS):
            acc = acc + jnp.maximum(_dot(kc, qiz[h]), 0.0) * w[h:h + 1, :]
        s = acc * idx_scale + 0.0
        sc_ref[c] = jnp.where(krow + c * tk <= qcol_g, s, -jnp.inf)
        return carry

    lax.fori_loop(0, nchunks, score_chunk, 0)

    def count_pass(pred):
        def body(c, acc):
            return acc + _fold_rows(jnp.where(pred(sc_ref[c], c), 1.0, 0.0))
        acc = lax.fori_loop(0, nchunks, body, jnp.zeros((SUBLANES, tq), F32))
        return jnp.sum(acc, axis=0, keepdims=True)

    def bit_step(it, cur):
        cand = cur | lax.shift_left(jnp.int32(1), 31 - it)
        candf = _ordered_bits_to_float(cand)
        cnt = count_pass(lambda s, c: s >= candf)
        return jnp.where(cnt >= float(topk), cand, cur)

    cur = lax.fori_loop(0, 32, bit_step, jnp.zeros((1, tq), jnp.int32))
    short = (cur & jnp.int32(-2 ** 23)) == 0
    thr = jnp.where(short, -jnp.inf, _ordered_bits_to_float(cur))
    cnt_gt = count_pass(lambda s, c: s > thr)
    n_tie = jnp.where(short, 0.0, float(topk) - cnt_gt)
    below = (lax.broadcasted_iota(jnp.int32, (tk, tk), 1)
             < lax.broadcasted_iota(jnp.int32, (tk, tk), 0)).astype(BF16)
    ones_row = jnp.ones((SUBLANES, tk), BF16)

    def mask_chunk(c, seen):
        s = sc_ref[c]
        tie = s == thr
        tie01 = jnp.where(tie, 1.0, 0.0).astype(BF16)
        rank = _dot(below, tie01) + seen
        sel = (s > thr) | (tie & (rank < n_tie))
        madd_ref[c] = jnp.where(sel, 0.0, NEG_BIG).astype(BF16)
        return seen + _dot(ones_row, tie01)[0:1, :]

    lax.fori_loop(0, nchunks, mask_chunk, jnp.zeros((1, tq), F32))

    for h, qz in enumerate(pad_heads(qat_ref, N_HEADS_A // 2)):
        qz_ref[h] = qz
    m_ref[...] = jnp.full(m_ref.shape, NEG_BIG, F32)
    l_ref[...] = jnp.zeros(l_ref.shape, F32)
    acc_ref[...] = jnp.zeros(acc_ref.shape, F32)

    def attn_chunk(c, variant):
        k0 = pl.multiple_of(c * tk, tk)
        for h in range(N_HEADS_A):
            p = h // 2
            kp = ka_ref[0, pl.ds(k0, tk), p * LANES:(p + 1) * LANES]
            logits = _dot(kp, qz_ref[h]).astype(BF16) + madd_ref[c]
            if variant is not None:
                logits = logits + bias_ref[variant, h]
            hrow = slice(h, h + 1)
            m_old = m_ref[hrow, :]
            m_new = jnp.maximum(m_old, jnp.max(logits, axis=0, keepdims=True).astype(F32))
            alpha = jnp.exp2(m_old - m_new)
            pr = jnp.exp2(logits - m_new.astype(BF16))
            l_ref[hrow, :] = alpha * l_ref[hrow, :] + _dot(ones_row, pr)[0:1, :]
            m_ref[hrow, :] = m_new
            rows = slice(h * HEAD_DIM, (h + 1) * HEAD_DIM)
            acc_ref[rows, :] = alpha * acc_ref[rows, :] + _dot(vat_ref[0, c, rows, :], pr)

    first_near = i * ratio - 1
    n_far = jnp.maximum(first_near, 0)

    def far_body(c, carry):
        attn_chunk(c, None)
        return carry

    def near_body(c, carry):
        attn_chunk(c, c - first_near)
        return carry

    lax.fori_loop(0, n_far, far_body, 0)
    lax.fori_loop(n_far, nchunks, near_body, 0)

    outs = []
    for h in range(N_HEADS_A):
        rows = slice(h * HEAD_DIM, (h + 1) * HEAD_DIM)
        outs.append(acc_ref[rows, :] / l_ref[h:h + 1, :])
    o = jnp.concatenate(outs, axis=0).T
    out_ref[0] = (o * ga_ref[0].astype(F32)).astype(BF16)


def _dsa(qat, ka, vat, qit, ki, wit, ga, bias_a, tq, tk):
    b, s, _ = ka.shape
    topk = min(TOPK_MAX, s // 4)
    nq = s // tq
    nkc = s // tk
    qfeat = lambda c: pl.BlockSpec((1, c, tq), lambda bi, i: (bi, 0, i))
    qtok = lambda c: pl.BlockSpec((1, tq, c), lambda bi, i: (bi, i, 0))
    full = lambda c: pl.BlockSpec((1, s, c), lambda bi, i: (bi, 0, 0))
    return pl.pallas_call(
        functools.partial(_dsa_kernel, topk, tq, tk),
        grid=(b, nq),
        in_specs=[qfeat(D_A), full(D_A),
                  pl.BlockSpec((1, nkc, D_A, tk), lambda bi, i: (bi, 0, 0, 0)),
                  qfeat(D_QI), full(LANES), qfeat(WI_ROWS), qtok(D_A),
                  pl.BlockSpec(bias_a.shape, lambda bi, i: (0, 0, 0, 0), pipeline_mode=pl.Buffered(1))],
        out_specs=qtok(D_A),
        out_shape=jax.ShapeDtypeStruct((b, s, D_A), BF16),
        scratch_shapes=[pltpu.VMEM((nkc, tk, tq), F32),
                        pltpu.VMEM((nkc, tk, tq), BF16),
                        pltpu.VMEM((N_HEADS_A, LANES, tq), BF16),
                        pltpu.VMEM((N_HEADS_A, tq), F32),
                        pltpu.VMEM((N_HEADS_A, tq), F32),
                        pltpu.VMEM((D_A, tq), F32)],
        compiler_params=pltpu.CompilerParams(dimension_semantics=("parallel", "arbitrary"),
                                             vmem_limit_bytes=VMEM_LIMIT),
        name="dsa",
    )(qat, ka, vat, qit, ki, wit, ga, bias_a)


def _dil_kernel(qb_blocks, q_ref, kc_ref, kp_ref, vc_ref, vp_ref, bias_ref, o_ref, lse_ref):
    j = pl.program_id(2)
    lane = lax.broadcasted_iota(jnp.int32, (BLOCK, LANES), 1)
    low_half = lane < HEAD_DIM
    for blk in range(qb_blocks):
        rows = slice(blk * BLOCK, (blk + 1) * BLOCK)
        if blk == 0:
            variant = jnp.where(j == 0, 1, 0)
        else:
            variant = 0
        lse_tile = jnp.zeros((BLOCK, LANES), F32)
        for p in range(N_HEADS_B // 2):
            lanes = slice(p * LANES, (p + 1) * LANES)
            if blk == 0:
                kband = jnp.concatenate([kp_ref[0, :, lanes], kc_ref[0, 0:BLOCK, lanes]], axis=0)
                vband = jnp.concatenate([vp_ref[0, :, lanes], vc_ref[0, 0:BLOCK, lanes]], axis=0)
            else:
                band = slice((blk - 1) * BLOCK, (blk + 1) * BLOCK)
                kband = kc_ref[0, band, lanes]
                vband = vc_ref[0, band, lanes]
            qp = q_ref[0, rows, lanes]
            zero = jnp.zeros_like(qp)
            outs = []
            for e in range(2):
                h = 2 * p + e
                qz = jnp.where(low_half, qp, zero) if e == 0 else jnp.where(low_half, zero, qp)
                logits = _dot_nt(qz, kband) + bias_ref[variant, h]
                m = jnp.max(logits, axis=1, keepdims=True)
                pr = jnp.exp(logits - m)
                den = jnp.sum(pr, axis=1, keepdims=True)
                num = _dot(pr.astype(BF16), vband)
                outs.append(num / den)
                lse_tile = lse_tile + jnp.where(lane == h, m + jnp.log(den), 0.0)
            o_ref[0, rows, lanes] = jnp.where(low_half, outs[0], outs[1])
        lse_ref[0, rows, :] = lse_tile


def _dilated_pattern(qv, kv, vv, bias_p, dil, qb_blocks):
    b, n_sub, _ = qv.shape
    tu = qb_blocks * BLOCK
    nj = n_sub // tu
    cur = pl.BlockSpec((1, tu, D_B), lambda bi, r, j: (bi, j, r))
    prev = pl.BlockSpec((1, BLOCK, D_B), lambda bi, r, j: (bi, jnp.maximum(j * qb_blocks - 1, 0), r))
    return pl.pallas_call(
        functools.partial(_dil_kernel, qb_blocks),
        grid=(b, dil, nj),
        in_specs=[cur, cur, prev, cur, prev,
                  pl.BlockSpec((2, N_HEADS_B, BLOCK, 2 * BLOCK), lambda bi, r, j: (0, 0, 0, 0),
                               pipeline_mode=pl.Buffered(1))],
        out_specs=(pl.BlockSpec((1, tu, D_B), lambda bi, r, j: (bi, j, r)),
                   pl.BlockSpec((1, tu, LANES), lambda bi, r, j: (bi, j, r))),
        out_shape=(jax.ShapeDtypeStruct((b, n_sub, dil * D_B), F32),
                   jax.ShapeDtypeStruct((b, n_sub, dil * LANES), F32)),
        compiler_params=pltpu.CompilerParams(dimension_semantics=("parallel", "parallel", "arbitrary"),
                                             vmem_limit_bytes=VMEM_LIMIT),
        name=f"dilated_d{dil}",
    )(qv, kv, kv, vv, vv, bias_p)


def _out_kernel(dils, x_ref, a_ref, gb_ref, w_ref, e_ref, *refs):
    npat = len(dils)
    o_refs = refs[0:npat]
    lse_refs = refs[npat:2 * npat]
    out_ref = refs[2 * npat]
    oslab_ref, lslab_ref = refs[2 * npat + 1:2 * npat + 3]
    tm = x_ref.shape[0]

    def natural_lse(j, lse_ref, d):
        if d == 1:
            return lse_ref[...]
        for r in range(d):
            lslab_ref[j, pl.ds(r, tm // d, stride=d), :] = lse_ref[:, r * LANES:(r + 1) * LANES]
        return lslab_ref[j]

    def natural_o(o_ref, d):
        if d == 1:
            return o_ref[...]
        for r in range(d):
            for pr in range(D_B // LANES):
                lane0 = r * D_B + pr * LANES
                oslab_ref[pr, pl.ds(r, tm // d, stride=d), :] = o_ref[:, lane0:lane0 + LANES]
        return jnp.concatenate([oslab_ref[pr] for pr in range(D_B // LANES)], axis=1)

    lses = [natural_lse(j, lse_ref, d) for j, (lse_ref, d) in enumerate(zip(lse_refs, dils))]
    top = functools.reduce(jnp.maximum, lses)
    ws = [jnp.exp(l - top) for l in lses]
    tot = functools.reduce(lambda a, c: a + c, ws)
    e = e_ref[...]
    bmix = jnp.zeros((tm, D_B), F32)
    for wgt, o_ref, d in zip(ws, o_refs, dils):
        wn = wgt / tot
        hi = wn.astype(BF16)
        lo = (wn - hi.astype(F32)).astype(BF16)
        bmix = bmix + (_dot(hi, e) + _dot(lo, e)) * natural_o(o_ref, d)
    bg = (bmix * gb_ref[...].astype(F32)).astype(BF16)
    out_ref[...] = x_ref[...] + _dot(a_ref[...], w_ref[0:D_A, :]) + _dot(bg, w_ref[D_A:D_A + D_B, :])


def _out_proj(x2, a_g, gb, w_out, expand, os_, lses, dils, tm):
    n, d = x2.shape
    row = lambda i: (i, 0)
    const = lambda i: (0, 0)
    one_buf = dict(pipeline_mode=pl.Buffered(1))
    in_specs = ([pl.BlockSpec((tm, d), row), pl.BlockSpec((tm, D_A), row), pl.BlockSpec((tm, D_B), row),
                 pl.BlockSpec((D_A + D_B, d), const, **one_buf),
                 pl.BlockSpec((LANES, D_B), const, **one_buf)]
                + [pl.BlockSpec((tm // dl, dl * D_B), row) for dl in dils]
                + [pl.BlockSpec((tm // dl, dl * LANES), row) for dl in dils])
    return pl.pallas_call(
        functools.partial(_out_kernel, dils),
        grid=(n // tm,),
        in_specs=in_specs,
        out_specs=pl.BlockSpec((tm, d), row),
        out_shape=jax.ShapeDtypeStruct((n, d), F32),
        scratch_shapes=[pltpu.VMEM((D_B // LANES, tm, LANES), F32), pltpu.VMEM((len(dils), tm, LANES), F32)],
        compiler_params=pltpu.CompilerParams(dimension_semantics=("parallel",),
                                             vmem_limit_bytes=VMEM_LIMIT),
        name="merge_outproj",
    )(x2, a_g, gb, w_out, expand, *os_, *lses)


def _pack_w_in(w):
    qa, ka, va, za = (w[:, j * D_A:(j + 1) * D_A] for j in range(4))
    o = 4 * D_A
    qi = w[:, o:o + D_QI]
    ki = w[:, o + D_QI:o + D_QI + IDX_DIM]
    wi = w[:, o + D_QI + IDX_DIM:o + D_QI + IDX_DIM + N_IDX_HEADS]
    wb = w[:, o + D_QI + IDX_DIM + N_IDX_HEADS:]
    wn = jnp.concatenate([ka, za, ki, ki, wb], axis=1).astype(BF16)
    wi = jnp.pad(wi, ((0, 0), (0, WI_ROWS - N_IDX_HEADS)))
    wt = jnp.concatenate([qa, va, qi, wi], axis=1).T.astype(BF16)
    return wn, wt


def _bias_lookup(table, bucket):
    onehot = jax.nn.one_hot(bucket, N_BUCKETS, dtype=F32)
    return jnp.einsum('...n,nh->...h', onehot, table.astype(F32), precision=lax.Precision.HIGHEST)


def _dsa_bias_tiles(bias_a, tq, tk):
    k = jnp.arange(tk)[:, None]
    q = jnp.arange(tq)[None, :]
    far = bias_a[N_BUCKETS - 1].astype(F32)
    tiles = [_bias_lookup(bias_a, _rel_bucket(q - k + tk * (1 - v))) - far for v in range(1 + tq // tk)]
    return (jnp.transpose(jnp.stack(tiles), (0, 3, 1, 2)) * LOG2E).astype(BF16)


def _dil_bias_tiles(bias_b, window, dil):
    steps = window // dil
    a = jnp.arange(BLOCK)[:, None]
    c = jnp.arange(2 * BLOCK)[None, :]
    diff = a - c + BLOCK
    band = (diff >= 0) & (diff <= steps)
    bias = jnp.transpose(_bias_lookup(bias_b, _rel_bucket(diff * dil)), (2, 0, 1))
    normal = jnp.where(band[None], bias, NEG_BIG)
    first = jnp.where((band & (c >= BLOCK))[None], bias, NEG_BIG)
    return jnp.stack([normal, first])


def kernel(x, norm_gain, w_in, w_out, rel_bias, q_norm_a, k_norm_a, q_norm_b, k_norm_b):
    b, s, d = x.shape
    n = b * s
    depth = norm_gain.shape[0]
    tq = min(DSA_TQ, s)
    tk = min(DSA_TK, s)
    tm = min(512, s)
    bias_a = rel_bias[:, :N_HEADS_A]
    bias_b = rel_bias[:, N_HEADS_A:]
    eye_blocks = (jnp.arange(D_A)[:, None] // HEAD_DIM == jnp.arange(D_A)[None, :] // HEAD_DIM).astype(BF16)
    expand = (jnp.arange(LANES)[:, None] == jnp.arange(D_B)[None, :] // HEAD_DIM).astype(BF16)
    assert tk >= MAX_DISTANCE and tq % tk == 0
    dsa_bias = _dsa_bias_tiles(bias_a, tq, tk)
    dil_bias = [_dil_bias_tiles(bias_b, wnd, dil) for wnd, dil in DIL_PATTERNS]
    dils = tuple(dil for _, dil in DIL_PATTERNS)

    for layer in range(depth):
        gn = jnp.stack([jnp.tile(g[layer], N_HEADS_A) for g in (q_norm_a, k_norm_a, q_norm_b, k_norm_b)])
        wn, wt = _pack_w_in(w_in[layer])
        npat = len(dils)
        outs = _inproj(x, norm_gain[layer][None, :], wn, wt, eye_blocks, gn, gn[0][:, None], tm, tk, dils)
        qat, ka, vat, ga, qit, ki, wit, gb = outs[:8]
        qviews, kviews, vviews = (outs[8 + j * npat:8 + (j + 1) * npat] for j in range(3))
        a_g = _dsa(qat, ka, vat, qit, ki, wit, ga, dsa_bias, tq, tk)

        os_, lses = [], []
        for j, ((wnd, dil), bias_p) in enumerate(zip(DIL_PATTERNS, dil_bias)):
            assert wnd // dil == BLOCK and s % (dil * BLOCK) == 0 and tm % (dil * 2 * SUBLANES) == 0
            n_sub = s // dil
            qb_blocks = min(4, n_sub // BLOCK)
            o, lse = _dilated_pattern(qviews[j], kviews[j], vviews[j], bias_p, dil, qb_blocks)
            os_.append(o.reshape(n // dil, dil * D_B))
            lses.append(lse.reshape(n // dil, dil * LANES))
        x = _out_proj(x.reshape(n, d), a_g.reshape(n, D_A), gb.reshape(n, D_B), w_out[layer].astype(BF16),
                      expand, os_, lses, dils, tm).reshape(b, s, d)
    return x
```

```python
import functools
import math

import jax
import jax.numpy as jnp
from jax import lax
from jax.experimental import pallas as pl
from jax.experimental.pallas import tpu as pltpu

HEAD_DIM = 64
N_HEADS_A = 8
N_HEADS_B = 8
D_A = N_HEADS_A * HEAD_DIM
D_B = N_HEADS_B * HEAD_DIM
N_IDX_HEADS = 4
IDX_DIM = 64
D_QI = N_IDX_HEADS * IDX_DIM
TOPK_MAX = 256
DIL_PATTERNS = ((128, 1), (512, 4), (2048, 16))
BLOCK = 128
N_BUCKETS = 32
MAX_DISTANCE = 128
EPS = 1e-6

LANES = 128
SUBLANES = 8
WI_ROWS = 16
NEG_BIG = -(2.0 ** 100)
LOG2E = math.log2(math.e)
DSA_TQ = 256
DSA_TK = 256
VMEM_LIMIT = 56 * 1024 * 1024

F32 = jnp.float32
BF16 = jnp.bfloat16
NT_DIMS = (((1,), (1,)), ((), ()))


def _dot(a, b):
    return jnp.dot(a, b, preferred_element_type=F32)


def _dot_nt(a, b):
    return lax.dot_general(a, b, NT_DIMS, preferred_element_type=F32)


def _rel_bucket(dist):
    max_exact = N_BUCKETS // 2
    d = jnp.maximum(dist, 0)
    df = jnp.maximum(d, 1).astype(F32)
    large = max_exact + (jnp.log(df / max_exact) / math.log(MAX_DISTANCE / max_exact)
                         * (N_BUCKETS - max_exact)).astype(jnp.int32)
    large = jnp.minimum(large, N_BUCKETS - 1)
    return jnp.where(d < max_exact, d, large)


def _inproj_kernel(tk, dils, x_ref, g_ref, wn_ref, wt_ref, bd_ref, gn_ref, gcol_ref,
                   qat_ref, ka_ref, vat_ref, ga_ref, qit_ref, ki_ref, wit_ref, gb_ref, *rest):
    n_b = 3 * len(dils)
    b_refs = [rest[j * len(dils):(j + 1) * len(dils)] for j in range(3)]
    slab_ref = rest[n_b]
    x = x_ref[0]
    tm = x.shape[0]
    r = lax.rsqrt(jnp.mean(x * x, axis=-1, keepdims=True) + EPS)
    xn = (x * r * g_ref[...]).astype(BF16)
    bd = bd_ref[...]

    def head_norm(q, gain, scale):
        ss = _dot((q * q).astype(BF16), bd)
        return q * lax.rsqrt(ss * (1.0 / HEAD_DIM) + EPS) * (gain * scale)

    def silu(z):
        return z * jax.nn.sigmoid(z)

    p = _dot(xn, wn_ref[:, 0:2 * D_A + LANES])
    ka_ref[0] = head_norm(p[:, 0:D_A], gn_ref[1:2, :], 1.0).astype(BF16)
    ga_ref[0] = silu(p[:, D_A:2 * D_A]).astype(BF16)
    ki_ref[0] = p[:, 2 * D_A:2 * D_A + LANES].astype(BF16)
    c0 = 2 * D_A + LANES
    p = _dot(xn, wn_ref[:, c0:c0 + 4 * D_B])
    gb_ref[0] = silu(p[:, 3 * D_B:4 * D_B]).astype(BF16)

    def emit_views(val, refs):
        for d, ref in zip(dils, refs):
            if d == 1:
                ref[0] = val.astype(BF16)
        if all(d == 1 for d in dils):
            return
        for pr in range(D_B // LANES):
            slab_ref[pr] = val[:, pr * LANES:(pr + 1) * LANES]
        for d, ref in zip(dils, refs):
            if d == 1:
                continue
            rows = tm // d
            for r in range(d):
                for pr in range(D_B // LANES):
                    lane0 = r * D_B + pr * LANES
                    ref[0, :, lane0:lane0 + LANES] = slab_ref[pr, pl.ds(r, rows, stride=d), :].astype(BF16)

    emit_views(head_norm(p[:, 0:D_B], gn_ref[2:3, :], HEAD_DIM ** -0.5), b_refs[0])
    emit_views(head_norm(p[:, D_B:2 * D_B], gn_ref[3:4, :], 1.0), b_refs[1])
    emit_views(p[:, 2 * D_B:3 * D_B], b_refs[2])

    qt = _dot_nt(wt_ref[0:D_A, :], xn)
    ss = _dot(bd, (qt * qt).astype(BF16))
    qt = qt * lax.rsqrt(ss * (1.0 / HEAD_DIM) + EPS) * (gcol_ref[...] * (HEAD_DIM ** -0.5 * LOG2E))
    qat_ref[0] = qt.astype(BF16)
    vt = _dot_nt(wt_ref[D_A:2 * D_A, :], xn)
    for cc in range(tm // tk):
        vat_ref[0, cc] = vt[:, cc * tk:(cc + 1) * tk].astype(BF16)
    qit_ref[0] = _dot_nt(wt_ref[2 * D_A:2 * D_A + D_QI, :], xn).astype(BF16)
    wit_ref[0] = _dot_nt(wt_ref[2 * D_A + D_QI:2 * D_A + D_QI + WI_ROWS, :], xn)


def _inproj(x, g, wn, wt, bd, gn, gcol, tm, tk, dils):
    b, s, d = x.shape
    const = lambda bi, j: (0, 0)
    tok = lambda c: pl.BlockSpec((1, tm, c), lambda bi, j: (bi, j, 0))
    feat = lambda c: pl.BlockSpec((1, c, tm), lambda bi, j: (bi, 0, j))
    one_buf = dict(pipeline_mode=pl.Buffered(1))
    sds = jax.ShapeDtypeStruct
    view_shapes = tuple(sds((b, s // dl, dl * D_B), BF16) for dl in dils)
    view_specs = tuple(pl.BlockSpec((1, tm // dl, dl * D_B), lambda bi, j: (bi, j, 0)) for dl in dils)
    out_shapes = (sds((b, D_A, s), BF16), sds((b, s, D_A), BF16), sds((b, s // tk, D_A, tk), BF16),
                  sds((b, s, D_A), BF16), sds((b, D_QI, s), BF16), sds((b, s, LANES), BF16),
                  sds((b, WI_ROWS, s), F32), sds((b, s, D_B), BF16)) + view_shapes * 3
    out_specs = (feat(D_A), tok(D_A),
                 pl.BlockSpec((1, tm // tk, D_A, tk), lambda bi, j: (bi, j, 0, 0)),
                 tok(D_A), feat(D_QI), tok(LANES), feat(WI_ROWS), tok(D_B)) + view_specs * 3
    return pl.pallas_call(
        functools.partial(_inproj_kernel, tk, dils),
        grid=(b, s // tm),
        scratch_shapes=[pltpu.VMEM((D_B // LANES, tm, LANES), F32)],
        in_specs=[tok(d),
                  pl.BlockSpec((1, d), const, **one_buf),
                  pl.BlockSpec(wn.shape, const, **one_buf),
                  pl.BlockSpec(wt.shape, const, **one_buf),
                  pl.BlockSpec((D_A, D_A), const, **one_buf),
                  pl.BlockSpec((4, D_A), const, **one_buf),
                  pl.BlockSpec((D_A, 1), const, **one_buf)],
        out_specs=out_specs,
        out_shape=out_shapes,
        compiler_params=pltpu.CompilerParams(dimension_semantics=("parallel", "parallel"),
                                             vmem_limit_bytes=VMEM_LIMIT),
        name="inproj",
    )(x, g, wn, wt, bd, gn, gcol)


def _ordered_bits_to_float(u):
    key = u ^ jnp.int32(-2 ** 31)
    bits = key ^ ((key >> 31) & jnp.int32(0x7FFFFFFF))
    return lax.bitcast_convert_type(bits, F32)


def _fold_rows(x, group):
    parts = [None] * 4
    for r in range(x.shape[0] // group):
        blk = x[r * group:(r + 1) * group, :]
        parts[r % 4] = blk if parts[r % 4] is None else parts[r % 4] + blk
    return (parts[0] + parts[1]) + (parts[2] + parts[3])


def _truncate_to_bf16(x):
    bits = lax.bitcast_convert_type(x, jnp.int32) & jnp.int32(-2 ** 16)
    return lax.bitcast_convert_type(bits, F32).astype(BF16)


def _dsa_kernel(topk, tq, tk, nkc, qat_ref, ka_ref, vat_ref, qit_ref, ki_ref, wit_ref, ga_ref, bias_ref,
                out_ref, sc_ref, sc16_ref, madd_ref, qz_ref, m_ref, l_ref, acc_ref):
    i = pl.program_id(1)
    ratio = tq // tk
    nchunks = (i + 1) * ratio
    idx_scale = (N_IDX_HEADS * IDX_DIM) ** -0.5
    top_half = lax.broadcasted_iota(jnp.int32, (LANES, tq), 0) < HEAD_DIM

    def pad_heads(ref, npairs):
        out = []
        for p in range(npairs):
            blk = ref[0, p * LANES:(p + 1) * LANES, :]
            zero = jnp.zeros_like(blk)
            out.append(jnp.where(top_half, blk, zero))
            out.append(jnp.where(top_half, zero, blk))
        return out

    qiz = pad_heads(qit_ref, N_IDX_HEADS // 2)
    w = wit_ref[0]
    krow = lax.broadcasted_iota(jnp.int32, (tk, tq), 0)
    qcol_g = i * tq + lax.broadcasted_iota(jnp.int32, (tk, tq), 1)

    def score_chunk(c, carry):
        kc = ki_ref[0, pl.ds(pl.multiple_of(c * tk, tk), tk), :]
        acc = jnp.zeros((tk, tq), F32)
        for h in range(N_IDX_HEADS):
            acc = acc + jnp.maximum(_dot(kc, qiz[h]), 0.0) * w[h:h + 1, :]
        s = acc * idx_scale + 0.0
        s = jnp.where(krow + c * tk <= qcol_g, s, -jnp.inf)
        sc_ref[c] = s
        sc16_ref[c] = _truncate_to_bf16(s)
        return carry

    lax.fori_loop(0, nchunks, score_chunk, 0)

    def count_pass(pred):
        def body(c, acc):
            return acc + _fold_rows(jnp.where(pred(sc_ref[c], c), 1.0, 0.0), SUBLANES)
        acc = lax.fori_loop(0, nchunks, body, jnp.zeros((SUBLANES, tq), F32))
        return jnp.sum(acc, axis=0, keepdims=True)

    def count_pass16(cand16):
        one, zero = jnp.ones((), BF16), jnp.zeros((), BF16)

        def body(c, acc):
            return acc + _fold_rows(jnp.where(sc16_ref[c] >= cand16, one, zero), 2 * SUBLANES)
        acc = lax.fori_loop(0, nchunks, body, jnp.zeros((2 * SUBLANES, tq), BF16))
        return jnp.sum(acc.astype(F32), axis=0, keepdims=True)

    def bit_step16(it, cur):
        cand = cur | lax.shift_left(jnp.int32(1), 31 - it)
        cnt = count_pass16(_truncate_to_bf16(_ordered_bits_to_float(cand)))
        return jnp.where(cnt >= float(topk), cand, cur)

    def bit_step(it, cur):
        cand = cur | lax.shift_left(jnp.int32(1), 31 - it)
        candf = _ordered_bits_to_float(cand)
        cnt = count_pass(lambda s, c: s >= candf)
        return jnp.where(cnt >= float(topk), cand, cur)

    assert nkc * tk <= 256 * 2 * SUBLANES
    cur = lax.fori_loop(0, 16, bit_step16, jnp.zeros((1, tq), jnp.int32))
    cur = lax.fori_loop(16, 32, bit_step, cur)
    short = (cur & jnp.int32(-2 ** 23)) == 0
    thr = jnp.where(short, -jnp.inf, _ordered_bits_to_float(cur))
    cnt_gt = count_pass(lambda s, c: s > thr)
    n_tie = jnp.where(short, 0.0, float(topk) - cnt_gt)
    below = (lax.broadcasted_iota(jnp.int32, (tk, tk), 1)
             < lax.broadcasted_iota(jnp.int32, (tk, tk), 0)).astype(BF16)
    ones_row = jnp.ones((SUBLANES, tk), BF16)

    def mask_chunk(c, seen):
        s = sc_ref[c]
        tie = s == thr
        tie01 = jnp.where(tie, 1.0, 0.0).astype(BF16)
        rank = _dot(below, tie01) + seen
        sel = (s > thr) | (tie & (rank < n_tie))
        madd_ref[c] = jnp.where(sel, 0.0, NEG_BIG).astype(BF16)
        return seen + _dot(ones_row, tie01)[0:1, :]

    lax.fori_loop(0, nchunks, mask_chunk, jnp.zeros((1, tq), F32))

    for h, qz in enumerate(pad_heads(qat_ref, N_HEADS_A // 2)):
        qz_ref[h] = qz
    m_ref[...] = jnp.full(m_ref.shape, NEG_BIG, F32)
    l_ref[...] = jnp.zeros(l_ref.shape, F32)
    acc_ref[...] = jnp.zeros(acc_ref.shape, F32)

    def attn_chunk(c, variant):
        k0 = pl.multiple_of(c * tk, tk)
        for h in range(N_HEADS_A):
            p = h // 2
            kp = ka_ref[0, pl.ds(k0, tk), p * LANES:(p + 1) * LANES]
            logits = _dot(kp, qz_ref[h]).astype(BF16) + madd_ref[c]
            if variant is not None:
                logits = logits + bias_ref[variant, h]
            hrow = slice(h, h + 1)
            m_old = m_ref[hrow, :]
            m_new = jnp.maximum(m_old, jnp.max(logits, axis=0, keepdims=True).astype(F32))
            alpha = jnp.exp2(m_old - m_new)
            pr = jnp.exp2(logits - m_new.astype(BF16))
            l_ref[hrow, :] = alpha * l_ref[hrow, :] + _dot(ones_row, pr)[0:1, :]
            m_ref[hrow, :] = m_new
            rows = slice(h * HEAD_DIM, (h + 1) * HEAD_DIM)
            acc_ref[rows, :] = alpha * acc_ref[rows, :] + _dot(vat_ref[0, c, rows, :], pr)

    first_near = i * ratio - 1
    n_far = jnp.maximum(first_near, 0)

    def far_body(c, carry):
        attn_chunk(c, None)
        return carry

    def near_body(c, carry):
        attn_chunk(c, c - first_near)
        return carry

    lax.fori_loop(0, n_far, far_body, 0)
    lax.fori_loop(n_far, nchunks, near_body, 0)

    outs = []
    for h in range(N_HEADS_A):
        rows = slice(h * HEAD_DIM, (h + 1) * HEAD_DIM)
        outs.append(acc_ref[rows, :] / l_ref[h:h + 1, :])
    o = jnp.concatenate(outs, axis=0).T
    out_ref[0] = (o * ga_ref[0].astype(F32)).astype(BF16)


def _dsa(qat, ka, vat, qit, ki, wit, ga, bias_a, tq, tk):
    b, s, _ = ka.shape
    topk = min(TOPK_MAX, s // 4)
    nq = s // tq
    nkc = s // tk
    qfeat = lambda c: pl.BlockSpec((1, c, tq), lambda bi, i: (bi, 0, i))
    qtok = lambda c: pl.BlockSpec((1, tq, c), lambda bi, i: (bi, i, 0))
    full = lambda c: pl.BlockSpec((1, s, c), lambda bi, i: (bi, 0, 0))
    return pl.pallas_call(
        functools.partial(_dsa_kernel, topk, tq, tk, nkc),
        grid=(b, nq),
        in_specs=[qfeat(D_A), full(D_A),
                  pl.BlockSpec((1, nkc, D_A, tk), lambda bi, i: (bi, 0, 0, 0)),
                  qfeat(D_QI), full(LANES), qfeat(WI_ROWS), qtok(D_A),
                  pl.BlockSpec(bias_a.shape, lambda bi, i: (0, 0, 0, 0), pipeline_mode=pl.Buffered(1))],
        out_specs=qtok(D_A),
        out_shape=jax.ShapeDtypeStruct((b, s, D_A), BF16),
        scratch_shapes=[pltpu.VMEM((nkc, tk, tq), F32),
                        pltpu.VMEM((nkc, tk, tq), BF16),
                        pltpu.VMEM((nkc, tk, tq), BF16),
                        pltpu.VMEM((N_HEADS_A, LANES, tq), BF16),
                        pltpu.VMEM((N_HEADS_A, tq), F32),
                        pltpu.VMEM((N_HEADS_A, tq), F32),
                        pltpu.VMEM((D_A, tq), F32)],
        compiler_params=pltpu.CompilerParams(dimension_semantics=("parallel", "arbitrary"),
                                             vmem_limit_bytes=VMEM_LIMIT),
        name="dsa",
    )(qat, ka, vat, qit, ki, wit, ga, bias_a)


def _dil_kernel(qb_blocks, q_ref, kc_ref, kp_ref, vc_ref, vp_ref, bias_ref, o_ref, lse_ref):
    j = pl.program_id(2)
    lane = lax.broadcasted_iota(jnp.int32, (BLOCK, LANES), 1)
    low_half = lane < HEAD_DIM
    for blk in range(qb_blocks):
        rows = slice(blk * BLOCK, (blk + 1) * BLOCK)
        if blk == 0:
            variant = jnp.where(j == 0, 1, 0)
        else:
            variant = 0
        lse_tile = jnp.zeros((BLOCK, LANES), F32)
        for p in range(N_HEADS_B // 2):
            lanes = slice(p * LANES, (p + 1) * LANES)
            if blk == 0:
                kband = jnp.concatenate([kp_ref[0, :, lanes], kc_ref[0, 0:BLOCK, lanes]], axis=0)
                vband = jnp.concatenate([vp_ref[0, :, lanes], vc_ref[0, 0:BLOCK, lanes]], axis=0)
            else:
                band = slice((blk - 1) * BLOCK, (blk + 1) * BLOCK)
                kband = kc_ref[0, band, lanes]
                vband = vc_ref[0, band, lanes]
            qp = q_ref[0, rows, lanes]
            zero = jnp.zeros_like(qp)
            outs = []
            for e in range(2):
                h = 2 * p + e
                qz = jnp.where(low_half, qp, zero) if e == 0 else jnp.where(low_half, zero, qp)
                logits = _dot_nt(qz, kband) + bias_ref[variant, h]
                m = jnp.max(logits, axis=1, keepdims=True)
                pr = jnp.exp(logits - m)
                den = jnp.sum(pr, axis=1, keepdims=True)
                num = _dot(pr.astype(BF16), vband)
                outs.append(num / den)
                lse_tile = lse_tile + jnp.where(lane == h, m + jnp.log(den), 0.0)
            o_ref[0, rows, lanes] = jnp.where(low_half, outs[0], outs[1])
        lse_ref[0, rows, :] = lse_tile


def _dilated_pattern(qv, kv, vv, bias_p, dil, qb_blocks):
    b, n_sub, _ = qv.shape
    tu = qb_blocks * BLOCK
    nj = n_sub // tu
    cur = pl.BlockSpec((1, tu, D_B), lambda bi, r, j: (bi, j, r))
    prev = pl.BlockSpec((1, BLOCK, D_B), lambda bi, r, j: (bi, jnp.maximum(j * qb_blocks - 1, 0), r))
    return pl.pallas_call(
        functools.partial(_dil_kernel, qb_blocks),
        grid=(b, dil, nj),
        in_specs=[cur, cur, prev, cur, prev,
                  pl.BlockSpec((2, N_HEADS_B, BLOCK, 2 * BLOCK), lambda bi, r, j: (0, 0, 0, 0),
                               pipeline_mode=pl.Buffered(1))],
        out_specs=(pl.BlockSpec((1, tu, D_B), lambda bi, r, j: (bi, j, r)),
                   pl.BlockSpec((1, tu, LANES), lambda bi, r, j: (bi, j, r))),
        out_shape=(jax.ShapeDtypeStruct((b, n_sub, dil * D_B), F32),
                   jax.ShapeDtypeStruct((b, n_sub, dil * LANES), F32)),
        compiler_params=pltpu.CompilerParams(dimension_semantics=("parallel", "parallel", "arbitrary"),
                                             vmem_limit_bytes=VMEM_LIMIT),
        name=f"dilated_d{dil}",
    )(qv, kv, kv, vv, vv, bias_p)


def _out_kernel(dils, x_ref, a_ref, gb_ref, w_ref, e_ref, *refs):
    npat = len(dils)
    o_refs = refs[0:npat]
    lse_refs = refs[npat:2 * npat]
    out_ref = refs[2 * npat]
    oslab_ref, lslab_ref = refs[2 * npat + 1:2 * npat + 3]
    tm = x_ref.shape[0]

    def natural_lse(j, lse_ref, d):
        if d == 1:
            return lse_ref[...]
        for r in range(d):
            lslab_ref[j, pl.ds(r, tm // d, stride=d), :] = lse_ref[:, r * LANES:(r + 1) * LANES]
        return lslab_ref[j]

    def natural_o(o_ref, d):
        if d == 1:
            return o_ref[...]
        for r in range(d):
            for pr in range(D_B // LANES):
                lane0 = r * D_B + pr * LANES
                oslab_ref[pr, pl.ds(r, tm // d, stride=d), :] = o_ref[:, lane0:lane0 + LANES]
        return jnp.concatenate([oslab_ref[pr] for pr in range(D_B // LANES)], axis=1)

    lses = [natural_lse(j, lse_ref, d) for j, (lse_ref, d) in enumerate(zip(lse_refs, dils))]
    top = functools.reduce(jnp.maximum, lses)
    ws = [jnp.exp(l - top) for l in lses]
    tot = functools.reduce(lambda a, c: a + c, ws)
    e = e_ref[...]
    bmix = jnp.zeros((tm, D_B), F32)
    for wgt, o_ref, d in zip(ws, o_refs, dils):
        wn = wgt / tot
        hi = wn.astype(BF16)
        lo = (wn - hi.astype(F32)).astype(BF16)
        bmix = bmix + (_dot(hi, e) + _dot(lo, e)) * natural_o(o_ref, d)
    bg = (bmix * gb_ref[...].astype(F32)).astype(BF16)
    out_ref[...] = x_ref[...] + _dot(a_ref[...], w_ref[0:D_A, :]) + _dot(bg, w_ref[D_A:D_A + D_B, :])


def _out_proj(x2, a_g, gb, w_out, expand, os_, lses, dils, tm):
    n, d = x2.shape
    row = lambda i: (i, 0)
    const = lambda i: (0, 0)
    one_buf = dict(pipeline_mode=pl.Buffered(1))
    in_specs = ([pl.BlockSpec((tm, d), row), pl.BlockSpec((tm, D_A), row), pl.BlockSpec((tm, D_B), row),
                 pl.BlockSpec((D_A + D_B, d), const, **one_buf),
                 pl.BlockSpec((LANES, D_B), const, **one_buf)]
                + [pl.BlockSpec((tm // dl, dl * D_B), row) for dl in dils]
                + [pl.BlockSpec((tm // dl, dl * LANES), row) for dl in dils])
    return pl.pallas_call(
        functools.partial(_out_kernel, dils),
        grid=(n // tm,),
        in_specs=in_specs,
        out_specs=pl.BlockSpec((tm, d), row),
        out_shape=jax.ShapeDtypeStruct((n, d), F32),
        scratch_shapes=[pltpu.VMEM((D_B // LANES, tm, LANES), F32), pltpu.VMEM((len(dils), tm, LANES), F32)],
        compiler_params=pltpu.CompilerParams(dimension_semantics=("parallel",),
                                             vmem_limit_bytes=VMEM_LIMIT),
        name="merge_outproj",
    )(x2, a_g, gb, w_out, expand, *os_, *lses)


def _pack_w_in(w):
    qa, ka, va, za = (w[:, j * D_A:(j + 1) * D_A] for j in range(4))
    o = 4 * D_A
    qi = w[:, o:o + D_QI]
    ki = w[:, o + D_QI:o + D_QI + IDX_DIM]
    wi = w[:, o + D_QI + IDX_DIM:o + D_QI + IDX_DIM + N_IDX_HEADS]
    wb = w[:, o + D_QI + IDX_DIM + N_IDX_HEADS:]
    wn = jnp.concatenate([ka, za, ki, ki, wb], axis=1).astype(BF16)
    wi = jnp.pad(wi, ((0, 0), (0, WI_ROWS - N_IDX_HEADS)))
    wt = jnp.concatenate([qa, va, qi, wi], axis=1).T.astype(BF16)
    return wn, wt


def _bias_lookup(table, bucket):
    onehot = jax.nn.one_hot(bucket, N_BUCKETS, dtype=F32)
    return jnp.einsum('...n,nh->...h', onehot, table.astype(F32), precision=lax.Precision.HIGHEST)


def _dsa_bias_tiles(bias_a, tq, tk):
    k = jnp.arange(tk)[:, None]
    q = jnp.arange(tq)[None, :]
    far = bias_a[N_BUCKETS - 1].astype(F32)
    tiles = [_bias_lookup(bias_a, _rel_bucket(q - k + tk * (1 - v))) - far for v in range(1 + tq // tk)]
    return (jnp.transpose(jnp.stack(tiles), (0, 3, 1, 2)) * LOG2E).astype(BF16)


def _dil_bias_tiles(bias_b, window, dil):
    steps = window // dil
    a = jnp.arange(BLOCK)[:, None]
    c = jnp.arange(2 * BLOCK)[None, :]
    diff = a - c + BLOCK
    band = (diff >= 0) & (diff <= steps)
    bias = jnp.transpose(_bias_lookup(bias_b, _rel_bucket(diff * dil)), (2, 0, 1))
    normal = jnp.where(band[None], bias, NEG_BIG)
    first = jnp.where((band & (c >= BLOCK))[None], bias, NEG_BIG)
    return jnp.stack([normal, first])


def kernel(x, norm_gain, w_in, w_out, rel_bias, q_norm_a, k_norm_a, q_norm_b, k_norm_b):
    b, s, d = x.shape
    n = b * s
    depth = norm_gain.shape[0]
    tq = min(DSA_TQ, s)
    tk = min(DSA_TK, s)
    tm = min(512, s)
    bias_a = rel_bias[:, :N_HEADS_A]
    bias_b = rel_bias[:, N_HEADS_A:]
    eye_blocks = (jnp.arange(D_A)[:, None] // HEAD_DIM == jnp.arange(D_A)[None, :] // HEAD_DIM).astype(BF16)
    expand = (jnp.arange(LANES)[:, None] == jnp.arange(D_B)[None, :] // HEAD_DIM).astype(BF16)
    assert tk >= MAX_DISTANCE and tq % tk == 0
    dsa_bias = _dsa_bias_tiles(bias_a, tq, tk)
    dil_bias = [_dil_bias_tiles(bias_b, wnd, dil) for wnd, dil in DIL_PATTERNS]
    dils = tuple(dil for _, dil in DIL_PATTERNS)

    for layer in range(depth):
        gn = jnp.stack([jnp.tile(g[layer], N_HEADS_A) for g in (q_norm_a, k_norm_a, q_norm_b, k_norm_b)])
        wn, wt = _pack_w_in(w_in[layer])
        npat = len(dils)
        outs = _inproj(x, norm_gain[layer][None, :], wn, wt, eye_blocks, gn, gn[0][:, None], tm, tk, dils)
        qat, ka, vat, ga, qit, ki, wit, gb = outs[:8]
        qviews, kviews, vviews = (outs[8 + j * npat:8 + (j + 1) * npat] for j in range(3))
        a_g = _dsa(qat, ka, vat, qit, ki, wit, ga, dsa_bias, tq, tk)

        os_, lses = [], []
        for j, ((wnd, dil), bias_p) in enumerate(zip(DIL_PATTERNS, dil_bias)):
            assert wnd // dil == BLOCK and s % (dil * BLOCK) == 0 and tm % (dil * 2 * SUBLANES) == 0
            n_sub = s // dil
            qb_blocks = min(4, n_sub // BLOCK)
            o, lse = _dilated_pattern(qviews[j], kviews[j], vviews[j], bias_p, dil, qb_blocks)
            os_.append(o.reshape(n // dil, dil * D_B))
            lses.append(lse.reshape(n // dil, dil * LANES))
        x = _out_proj(x.reshape(n, d), a_g.reshape(n, D_A), gb.reshape(n, D_B), w_out[layer].astype(BF16),
                      expand, os_, lses, dils, tm).reshape(b, s, d)
    return x
```

```python
import functools
import math

import jax
import jax.numpy as jnp
from jax import lax
from jax.experimental import pallas as pl
from jax.experimental.pallas import tpu as pltpu

HEAD_DIM = 64
N_HEADS_A = 8
N_HEADS_B = 8
D_A = N_HEADS_A * HEAD_DIM
D_B = N_HEADS_B * HEAD_DIM
N_IDX_HEADS = 4
IDX_DIM = 64
D_QI = N_IDX_HEADS * IDX_DIM
TOPK_MAX = 256
DIL_PATTERNS = ((128, 1), (512, 4), (2048, 16))
BLOCK = 128
N_BUCKETS = 32
MAX_DISTANCE = 128
EPS = 1e-6

LANES = 128
SUBLANES = 8
WI_ROWS = 16
NEG_BIG = -(2.0 ** 100)
LOG2E = math.log2(math.e)
DSA_TQ = 256
DSA_TK = 256
FAR_UNROLL = 4
VMEM_LIMIT = 56 * 1024 * 1024

F32 = jnp.float32
BF16 = jnp.bfloat16
NT_DIMS = (((1,), (1,)), ((), ()))


def _dot(a, b):
    return jnp.dot(a, b, preferred_element_type=F32)


def _dot_nt(a, b):
    return lax.dot_general(a, b, NT_DIMS, preferred_element_type=F32)


def _rel_bucket(dist):
    max_exact = N_BUCKETS // 2
    d = jnp.maximum(dist, 0)
    df = jnp.maximum(d, 1).astype(F32)
    large = max_exact + (jnp.log(df / max_exact) / math.log(MAX_DISTANCE / max_exact)
                         * (N_BUCKETS - max_exact)).astype(jnp.int32)
    large = jnp.minimum(large, N_BUCKETS - 1)
    return jnp.where(d < max_exact, d, large)


def _inproj_kernel(tk, dils, x_ref, g_ref, wn_ref, wt_ref, bd_ref, gn_ref, gcol_ref,
                   qat_ref, ka_ref, vat_ref, ga_ref, qit_ref, ki_ref, wit_ref, gb_ref, *rest):
    n_b = 3 * len(dils)
    b_refs = [rest[j * len(dils):(j + 1) * len(dils)] for j in range(3)]
    slab_ref = rest[n_b]
    x = x_ref[0]
    tm = x.shape[0]
    r = lax.rsqrt(jnp.mean(x * x, axis=-1, keepdims=True) + EPS)
    xn = (x * r * g_ref[...]).astype(BF16)
    bd = bd_ref[...]

    def head_norm(q, gain, scale):
        ss = _dot((q * q).astype(BF16), bd)
        return q * lax.rsqrt(ss * (1.0 / HEAD_DIM) + EPS) * (gain * scale)

    def silu(z):
        return z * jax.nn.sigmoid(z)

    p = _dot(xn, wn_ref[:, 0:2 * D_A + LANES])
    ka_ref[0] = head_norm(p[:, 0:D_A], gn_ref[1:2, :], 1.0).astype(BF16)
    ga_ref[0] = silu(p[:, D_A:2 * D_A]).astype(BF16)
    ki_ref[0] = p[:, 2 * D_A:2 * D_A + LANES].astype(BF16)
    c0 = 2 * D_A + LANES
    p = _dot(xn, wn_ref[:, c0:c0 + 4 * D_B])
    gb_ref[0] = silu(p[:, 3 * D_B:4 * D_B]).astype(BF16)

    def emit_views(val, refs):
        for d, ref in zip(dils, refs):
            if d == 1:
                ref[0] = val.astype(BF16)
        if all(d == 1 for d in dils):
            return
        for pr in range(D_B // LANES):
            slab_ref[pr] = val[:, pr * LANES:(pr + 1) * LANES]
        for d, ref in zip(dils, refs):
            if d == 1:
                continue
            rows = tm // d
            for r in range(d):
                for pr in range(D_B // LANES):
                    lane0 = r * D_B + pr * LANES
                    ref[0, :, lane0:lane0 + LANES] = slab_ref[pr, pl.ds(r, rows, stride=d), :].astype(BF16)

    emit_views(head_norm(p[:, 0:D_B], gn_ref[2:3, :], HEAD_DIM ** -0.5), b_refs[0])
    emit_views(head_norm(p[:, D_B:2 * D_B], gn_ref[3:4, :], 1.0), b_refs[1])
    emit_views(p[:, 2 * D_B:3 * D_B], b_refs[2])

    qt = _dot_nt(wt_ref[0:D_A, :], xn)
    ss = _dot(bd, (qt * qt).astype(BF16))
    qt = qt * lax.rsqrt(ss * (1.0 / HEAD_DIM) + EPS) * (gcol_ref[...] * (HEAD_DIM ** -0.5 * LOG2E))
    qat_ref[0] = qt.astype(BF16)
    vt = _dot_nt(wt_ref[D_A:2 * D_A, :], xn)
    for cc in range(tm // tk):
        vat_ref[0, cc] = vt[:, cc * tk:(cc + 1) * tk].astype(BF16)
    qit_ref[0] = _dot_nt(wt_ref[2 * D_A:2 * D_A + D_QI, :], xn).astype(BF16)
    wit_ref[0] = _dot_nt(wt_ref[2 * D_A + D_QI:2 * D_A + D_QI + WI_ROWS, :], xn)


def _inproj(x, g, wn, wt, bd, gn, gcol, tm, tk, dils):
    b, s, d = x.shape
    const = lambda bi, j: (0, 0)
    tok = lambda c: pl.BlockSpec((1, tm, c), lambda bi, j: (bi, j, 0))
    feat = lambda c: pl.BlockSpec((1, c, tm), lambda bi, j: (bi, 0, j))
    one_buf = dict(pipeline_mode=pl.Buffered(1))
    sds = jax.ShapeDtypeStruct
    view_shapes = tuple(sds((b, s // dl, dl * D_B), BF16) for dl in dils)
    view_specs = tuple(pl.BlockSpec((1, tm // dl, dl * D_B), lambda bi, j: (bi, j, 0)) for dl in dils)
    out_shapes = (sds((b, D_A, s), BF16), sds((b, s, D_A), BF16), sds((b, s // tk, D_A, tk), BF16),
                  sds((b, s, D_A), BF16), sds((b, D_QI, s), BF16), sds((b, s, LANES), BF16),
                  sds((b, WI_ROWS, s), F32), sds((b, s, D_B), BF16)) + view_shapes * 3
    out_specs = (feat(D_A), tok(D_A),
                 pl.BlockSpec((1, tm // tk, D_A, tk), lambda bi, j: (bi, j, 0, 0)),
                 tok(D_A), feat(D_QI), tok(LANES), feat(WI_ROWS), tok(D_B)) + view_specs * 3
    return pl.pallas_call(
        functools.partial(_inproj_kernel, tk, dils),
        grid=(b, s // tm),
        scratch_shapes=[pltpu.VMEM((D_B // LANES, tm, LANES), F32)],
        in_specs=[tok(d),
                  pl.BlockSpec((1, d), const, **one_buf),
                  pl.BlockSpec(wn.shape, const, **one_buf),
                  pl.BlockSpec(wt.shape, const, **one_buf),
                  pl.BlockSpec((D_A, D_A), const, **one_buf),
                  pl.BlockSpec((4, D_A), const, **one_buf),
                  pl.BlockSpec((D_A, 1), const, **one_buf)],
        out_specs=out_specs,
        out_shape=out_shapes,
        compiler_params=pltpu.CompilerParams(dimension_semantics=("parallel", "parallel"),
                                             vmem_limit_bytes=VMEM_LIMIT),
        name="inproj",
    )(x, g, wn, wt, bd, gn, gcol)


def _ordered_bits_to_float(u):
    key = u ^ jnp.int32(-2 ** 31)
    bits = key ^ ((key >> 31) & jnp.int32(0x7FFFFFFF))
    return lax.bitcast_convert_type(bits, F32)


def _fold_rows(x, group):
    parts = [None] * 4
    for r in range(x.shape[0] // group):
        blk = x[r * group:(r + 1) * group, :]
        parts[r % 4] = blk if parts[r % 4] is None else parts[r % 4] + blk
    return (parts[0] + parts[1]) + (parts[2] + parts[3])


def _truncate_to_bf16(x):
    bits = lax.bitcast_convert_type(x, jnp.int32) & jnp.int32(-2 ** 16)
    return lax.bitcast_convert_type(bits, F32).astype(BF16)


def _dsa_kernel(topk, tq, tk, nkc, qat_ref, ka_ref, vat_ref, qit_ref, ki_ref, wit_ref, ga_ref, bias_ref,
                out_ref, sc_ref, sc16_ref, madd_ref, qz_ref, m_ref, l_ref, acc_ref):
    i = pl.program_id(1)
    ratio = tq // tk
    nchunks = (i + 1) * ratio
    idx_scale = (N_IDX_HEADS * IDX_DIM) ** -0.5
    top_half = lax.broadcasted_iota(jnp.int32, (LANES, tq), 0) < HEAD_DIM

    def pad_heads(ref, npairs):
        out = []
        for p in range(npairs):
            blk = ref[0, p * LANES:(p + 1) * LANES, :]
            zero = jnp.zeros_like(blk)
            out.append(jnp.where(top_half, blk, zero))
            out.append(jnp.where(top_half, zero, blk))
        return out

    qiz = pad_heads(qit_ref, N_IDX_HEADS // 2)
    w = wit_ref[0]
    krow = lax.broadcasted_iota(jnp.int32, (tk, tq), 0)
    qcol_g = i * tq + lax.broadcasted_iota(jnp.int32, (tk, tq), 1)

    def score_chunk(c, carry):
        kc = ki_ref[0, pl.ds(pl.multiple_of(c * tk, tk), tk), :]
        acc = jnp.zeros((tk, tq), F32)
        for h in range(N_IDX_HEADS):
            acc = acc + jnp.maximum(_dot(kc, qiz[h]), 0.0) * w[h:h + 1, :]
        s = acc * idx_scale + 0.0
        s = jnp.where(krow + c * tk <= qcol_g, s, -jnp.inf)
        sc_ref[c] = s
        sc16_ref[c] = _truncate_to_bf16(s)
        return carry

    lax.fori_loop(0, nchunks, score_chunk, 0)

    def count_pass(pred):
        def body(c, acc):
            return acc + _fold_rows(jnp.where(pred(sc_ref[c], c), 1.0, 0.0), SUBLANES)
        acc = lax.fori_loop(0, nchunks, body, jnp.zeros((SUBLANES, tq), F32))
        return jnp.sum(acc, axis=0, keepdims=True)

    def count_pass16(cand16):
        one, zero = jnp.ones((), BF16), jnp.zeros((), BF16)

        def body(c, acc):
            return acc + _fold_rows(jnp.where(sc16_ref[c] >= cand16, one, zero), 2 * SUBLANES)
        acc = lax.fori_loop(0, nchunks, body, jnp.zeros((2 * SUBLANES, tq), BF16))
        return jnp.sum(acc.astype(F32), axis=0, keepdims=True)

    def bit_step16(it, cur):
        cand = cur | lax.shift_left(jnp.int32(1), 31 - it)
        cnt = count_pass16(_truncate_to_bf16(_ordered_bits_to_float(cand)))
        return jnp.where(cnt >= float(topk), cand, cur)

    def bit_step(it, cur):
        cand = cur | lax.shift_left(jnp.int32(1), 31 - it)
        candf = _ordered_bits_to_float(cand)
        cnt = count_pass(lambda s, c: s >= candf)
        return jnp.where(cnt >= float(topk), cand, cur)

    assert nkc * tk <= 256 * 2 * SUBLANES
    cur = lax.fori_loop(0, 16, bit_step16, jnp.zeros((1, tq), jnp.int32))
    cur = lax.fori_loop(16, 32, bit_step, cur)
    short = (cur & jnp.int32(-2 ** 23)) == 0
    thr = jnp.where(short, -jnp.inf, _ordered_bits_to_float(cur))
    cnt_gt = count_pass(lambda s, c: s > thr)
    n_tie = jnp.where(short, 0.0, float(topk) - cnt_gt)
    below = (lax.broadcasted_iota(jnp.int32, (tk, tk), 1)
             < lax.broadcasted_iota(jnp.int32, (tk, tk), 0)).astype(BF16)
    ones_row = jnp.ones((SUBLANES, tk), BF16)

    def mask_chunk(c, seen):
        s = sc_ref[c]
        tie = s == thr
        tie01 = jnp.where(tie, 1.0, 0.0).astype(BF16)
        rank = _dot(below, tie01) + seen
        sel = (s > thr) | (tie & (rank < n_tie))
        madd_ref[c] = jnp.where(sel, 0.0, NEG_BIG).astype(BF16)
        return seen + _dot(ones_row, tie01)[0:1, :]

    lax.fori_loop(0, nchunks, mask_chunk, jnp.zeros((1, tq), F32))

    for h, qz in enumerate(pad_heads(qat_ref, N_HEADS_A // 2)):
        qz_ref[h] = qz
    m_ref[...] = jnp.full(m_ref.shape, NEG_BIG, F32)
    l_ref[...] = jnp.zeros(l_ref.shape, F32)
    acc_ref[...] = jnp.zeros(acc_ref.shape, F32)

    def attn_chunk(c, variant):
        k0 = pl.multiple_of(c * tk, tk)
        for h in range(N_HEADS_A):
            p = h // 2
            kp = ka_ref[0, pl.ds(k0, tk), p * LANES:(p + 1) * LANES]
            logits = _dot(kp, qz_ref[h]).astype(BF16) + madd_ref[c]
            if variant is not None:
                logits = logits + bias_ref[variant, h]
            hrow = slice(h, h + 1)
            m_old = m_ref[hrow, :]
            m_new = jnp.maximum(m_old, jnp.max(logits, axis=0, keepdims=True).astype(F32))
            alpha = jnp.exp2(m_old - m_new)
            pr = jnp.exp2(logits - m_new.astype(BF16))
            l_ref[hrow, :] = alpha * l_ref[hrow, :] + _dot(ones_row, pr)[0:1, :]
            m_ref[hrow, :] = m_new
            rows = slice(h * HEAD_DIM, (h + 1) * HEAD_DIM)
            acc_ref[rows, :] = alpha * acc_ref[rows, :] + _dot(vat_ref[0, c, rows, :], pr)

    first_near = i * ratio - 1
    n_far = jnp.maximum(first_near, 0)

    def far_group(j, carry):
        for u in range(FAR_UNROLL):
            attn_chunk(FAR_UNROLL * j + u, None)
        return carry

    def far_body(c, carry):
        attn_chunk(c, None)
        return carry

    n_grouped = (n_far // FAR_UNROLL) * FAR_UNROLL
    lax.fori_loop(0, n_far // FAR_UNROLL, far_group, 0)
    lax.fori_loop(n_grouped, n_far, far_body, 0)

    @pl.when(i > 0)
    def _():
        for v in range(ratio + 1):
            attn_chunk(first_near + v, v)

    @pl.when(i == 0)
    def _():
        for v in range(1, ratio + 1):
            attn_chunk(v - 1, v)

    outs = []
    for h in range(N_HEADS_A):
        rows = slice(h * HEAD_DIM, (h + 1) * HEAD_DIM)
        outs.append(acc_ref[rows, :] / l_ref[h:h + 1, :])
    o = jnp.concatenate(outs, axis=0).T
    out_ref[0] = (o * ga_ref[0].astype(F32)).astype(BF16)


def _dsa(qat, ka, vat, qit, ki, wit, ga, bias_a, tq, tk):
    b, s, _ = ka.shape
    topk = min(TOPK_MAX, s // 4)
    nq = s // tq
    nkc = s // tk
    qfeat = lambda c: pl.BlockSpec((1, c, tq), lambda bi, i: (bi, 0, i))
    qtok = lambda c: pl.BlockSpec((1, tq, c), lambda bi, i: (bi, i, 0))
    full = lambda c: pl.BlockSpec((1, s, c), lambda bi, i: (bi, 0, 0))
    return pl.pallas_call(
        functools.partial(_dsa_kernel, topk, tq, tk, nkc),
        grid=(b, nq),
        in_specs=[qfeat(D_A), full(D_A),
                  pl.BlockSpec((1, nkc, D_A, tk), lambda bi, i: (bi, 0, 0, 0)),
                  qfeat(D_QI), full(LANES), qfeat(WI_ROWS), qtok(D_A),
                  pl.BlockSpec(bias_a.shape, lambda bi, i: (0, 0, 0, 0), pipeline_mode=pl.Buffered(1))],
        out_specs=qtok(D_A),
        out_shape=jax.ShapeDtypeStruct((b, s, D_A), BF16),
        scratch_shapes=[pltpu.VMEM((nkc, tk, tq), F32),
                        pltpu.VMEM((nkc, tk, tq), BF16),
                        pltpu.VMEM((nkc, tk, tq), BF16),
                        pltpu.VMEM((N_HEADS_A, LANES, tq), BF16),
                        pltpu.VMEM((N_HEADS_A, tq), F32),
                        pltpu.VMEM((N_HEADS_A, tq), F32),
                        pltpu.VMEM((D_A, tq), F32)],
        compiler_params=pltpu.CompilerParams(dimension_semantics=("parallel", "arbitrary"),
                                             vmem_limit_bytes=VMEM_LIMIT),
        name="dsa",
    )(qat, ka, vat, qit, ki, wit, ga, bias_a)


def _dil_kernel(qb_blocks, q_ref, kc_ref, kp_ref, vc_ref, vp_ref, bias_ref, o_ref, lse_ref):
    j = pl.program_id(2)
    lane = lax.broadcasted_iota(jnp.int32, (BLOCK, LANES), 1)
    low_half = lane < HEAD_DIM
    for blk in range(qb_blocks):
        rows = slice(blk * BLOCK, (blk + 1) * BLOCK)
        if blk == 0:
            variant = jnp.where(j == 0, 1, 0)
        else:
            variant = 0
        lse_tile = jnp.zeros((BLOCK, LANES), F32)
        for p in range(N_HEADS_B // 2):
            lanes = slice(p * LANES, (p + 1) * LANES)
            if blk == 0:
                kband = jnp.concatenate([kp_ref[0, :, lanes], kc_ref[0, 0:BLOCK, lanes]], axis=0)
                vband = jnp.concatenate([vp_ref[0, :, lanes], vc_ref[0, 0:BLOCK, lanes]], axis=0)
            else:
                band = slice((blk - 1) * BLOCK, (blk + 1) * BLOCK)
                kband = kc_ref[0, band, lanes]
                vband = vc_ref[0, band, lanes]
            qp = q_ref[0, rows, lanes]
            zero = jnp.zeros_like(qp)
            outs = []
            for e in range(2):
                h = 2 * p + e
                qz = jnp.where(low_half, qp, zero) if e == 0 else jnp.where(low_half, zero, qp)
                logits = _dot_nt(qz, kband) + bias_ref[variant, h]
                m = jnp.max(logits, axis=1, keepdims=True)
                pr = jnp.exp(logits - m)
                den = jnp.sum(pr, axis=1, keepdims=True)
                num = _dot(pr.astype(BF16), vband)
                outs.append(num / den)
                lse_tile = lse_tile + jnp.where(lane == h, m + jnp.log(den), 0.0)
            o_ref[0, rows, lanes] = jnp.where(low_half, outs[0], outs[1])
        lse_ref[0, rows, :] = lse_tile


def _dilated_pattern(qv, kv, vv, bias_p, dil, qb_blocks):
    b, n_sub, _ = qv.shape
    tu = qb_blocks * BLOCK
    nj = n_sub // tu
    cur = pl.BlockSpec((1, tu, D_B), lambda bi, r, j: (bi, j, r))
    prev = pl.BlockSpec((1, BLOCK, D_B), lambda bi, r, j: (bi, jnp.maximum(j * qb_blocks - 1, 0), r))
    return pl.pallas_call(
        functools.partial(_dil_kernel, qb_blocks),
        grid=(b, dil, nj),
        in_specs=[cur, cur, prev, cur, prev,
                  pl.BlockSpec((2, N_HEADS_B, BLOCK, 2 * BLOCK), lambda bi, r, j: (0, 0, 0, 0),
                               pipeline_mode=pl.Buffered(1))],
        out_specs=(pl.BlockSpec((1, tu, D_B), lambda bi, r, j: (bi, j, r)),
                   pl.BlockSpec((1, tu, LANES), lambda bi, r, j: (bi, j, r))),
        out_shape=(jax.ShapeDtypeStruct((b, n_sub, dil * D_B), F32),
                   jax.ShapeDtypeStruct((b, n_sub, dil * LANES), F32)),
        compiler_params=pltpu.CompilerParams(dimension_semantics=("parallel", "parallel", "arbitrary"),
                                             vmem_limit_bytes=VMEM_LIMIT),
        name=f"dilated_d{dil}",
    )(qv, kv, kv, vv, vv, bias_p)


def _out_kernel(dils, x_ref, a_ref, gb_ref, w_ref, e_ref, *refs):
    npat = len(dils)
    o_refs = refs[0:npat]
    lse_refs = refs[npat:2 * npat]
    out_ref = refs[2 * npat]
    oslab_ref, lslab_ref = refs[2 * npat + 1:2 * npat + 3]
    tm = x_ref.shape[0]

    def natural_lse(j, lse_ref, d):
        if d == 1:
            return lse_ref[...]
        for r in range(d):
            lslab_ref[j, pl.ds(r, tm // d, stride=d), :] = lse_ref[:, r * LANES:(r + 1) * LANES]
        return lslab_ref[j]

    def natural_o(o_ref, d):
        if d == 1:
            return o_ref[...]
        for r in range(d):
            for pr in range(D_B // LANES):
                lane0 = r * D_B + pr * LANES
                oslab_ref[pr, pl.ds(r, tm // d, stride=d), :] = o_ref[:, lane0:lane0 + LANES]
        return jnp.concatenate([oslab_ref[pr] for pr in range(D_B // LANES)], axis=1)

    lses = [natural_lse(j, lse_ref, d) for j, (lse_ref, d) in enumerate(zip(lse_refs, dils))]
    top = functools.reduce(jnp.maximum, lses)
    ws = [jnp.exp(l - top) for l in lses]
    tot = functools.reduce(lambda a, c: a + c, ws)
    e = e_ref[...]
    bmix = jnp.zeros((tm, D_B), F32)
    for wgt, o_ref, d in zip(ws, o_refs, dils):
        wn = wgt / tot
        hi = wn.astype(BF16)
        lo = (wn - hi.astype(F32)).astype(BF16)
        bmix = bmix + (_dot(hi, e) + _dot(lo, e)) * natural_o(o_ref, d)
    bg = (bmix * gb_ref[...].astype(F32)).astype(BF16)
    out_ref[...] = x_ref[...] + _dot(a_ref[...], w_ref[0:D_A, :]) + _dot(bg, w_ref[D_A:D_A + D_B, :])


def _out_proj(x2, a_g, gb, w_out, expand, os_, lses, dils, tm):
    n, d = x2.shape
    row = lambda i: (i, 0)
    const = lambda i: (0, 0)
    one_buf = dict(pipeline_mode=pl.Buffered(1))
    in_specs = ([pl.BlockSpec((tm, d), row), pl.BlockSpec((tm, D_A), row), pl.BlockSpec((tm, D_B), row),
                 pl.BlockSpec((D_A + D_B, d), const, **one_buf),
                 pl.BlockSpec((LANES, D_B), const, **one_buf)]
                + [pl.BlockSpec((tm // dl, dl * D_B), row) for dl in dils]
                + [pl.BlockSpec((tm // dl, dl * LANES), row) for dl in dils])
    return pl.pallas_call(
        functools.partial(_out_kernel, dils),
        grid=(n // tm,),
        in_specs=in_specs,
        out_specs=pl.BlockSpec((tm, d), row),
        out_shape=jax.ShapeDtypeStruct((n, d), F32),
        scratch_shapes=[pltpu.VMEM((D_B // LANES, tm, LANES), F32), pltpu.VMEM((len(dils), tm, LANES), F32)],
        compiler_params=pltpu.CompilerParams(dimension_semantics=("parallel",),
                                             vmem_limit_bytes=VMEM_LIMIT),
        name="merge_outproj",
    )(x2, a_g, gb, w_out, expand, *os_, *lses)


def _pack_w_in(w):
    qa, ka, va, za = (w[:, j * D_A:(j + 1) * D_A] for j in range(4))
    o = 4 * D_A
    qi = w[:, o:o + D_QI]
    ki = w[:, o + D_QI:o + D_QI + IDX_DIM]
    wi = w[:, o + D_QI + IDX_DIM:o + D_QI + IDX_DIM + N_IDX_HEADS]
    wb = w[:, o + D_QI + IDX_DIM + N_IDX_HEADS:]
    wn = jnp.concatenate([ka, za, ki, ki, wb], axis=1).astype(BF16)
    wi = jnp.pad(wi, ((0, 0), (0, WI_ROWS - N_IDX_HEADS)))
    wt = jnp.concatenate([qa, va, qi, wi], axis=1).T.astype(BF16)
    return wn, wt


def _bias_lookup(table, bucket):
    onehot = jax.nn.one_hot(bucket, N_BUCKETS, dtype=F32)
    return jnp.einsum('...n,nh->...h', onehot, table.astype(F32), precision=lax.Precision.HIGHEST)


def _dsa_bias_tiles(bias_a, tq, tk):
    k = jnp.arange(tk)[:, None]
    q = jnp.arange(tq)[None, :]
    far = bias_a[N_BUCKETS - 1].astype(F32)
    tiles = [_bias_lookup(bias_a, _rel_bucket(q - k + tk * (1 - v))) - far for v in range(1 + tq // tk)]
    return (jnp.transpose(jnp.stack(tiles), (0, 3, 1, 2)) * LOG2E).astype(BF16)


def _dil_bias_tiles(bias_b, window, dil):
    steps = window // dil
    a = jnp.arange(BLOCK)[:, None]
    c = jnp.arange(2 * BLOCK)[None, :]
    diff = a - c + BLOCK
    band = (diff >= 0) & (diff <= steps)
    bias = jnp.transpose(_bias_lookup(bias_b, _rel_bucket(diff * dil)), (2, 0, 1))
    normal = jnp.where(band[None], bias, NEG_BIG)
    first = jnp.where((band & (c >= BLOCK))[None], bias, NEG_BIG)
    return jnp.stack([normal, first])


def kernel(x, norm_gain, w_in, w_out, rel_bias, q_norm_a, k_norm_a, q_norm_b, k_norm_b):
    b, s, d = x.shape
    n = b * s
    depth = norm_gain.shape[0]
    tq = min(DSA_TQ, s)
    tk = min(DSA_TK, s)
    tm = min(512, s)
    bias_a = rel_bias[:, :N_HEADS_A]
    bias_b = rel_bias[:, N_HEADS_A:]
    eye_blocks = (jnp.arange(D_A)[:, None] // HEAD_DIM == jnp.arange(D_A)[None, :] // HEAD_DIM).astype(BF16)
    expand = (jnp.arange(LANES)[:, None] == jnp.arange(D_B)[None, :] // HEAD_DIM).astype(BF16)
    assert tk >= MAX_DISTANCE and tq % tk == 0
    dsa_bias = _dsa_bias_tiles(bias_a, tq, tk)
    dil_bias = [_dil_bias_tiles(bias_b, wnd, dil) for wnd, dil in DIL_PATTERNS]
    dils = tuple(dil for _, dil in DIL_PATTERNS)

    for layer in range(depth):
        gn = jnp.stack([jnp.tile(g[layer], N_HEADS_A) for g in (q_norm_a, k_norm_a, q_norm_b, k_norm_b)])
        wn, wt = _pack_w_in(w_in[layer])
        npat = len(dils)
        outs = _inproj(x, norm_gain[layer][None, :], wn, wt, eye_blocks, gn, gn[0][:, None], tm, tk, dils)
        qat, ka, vat, ga, qit, ki, wit, gb = outs[:8]
        qviews, kviews, vviews = (outs[8 + j * npat:8 + (j + 1) * npat] for j in range(3))
        a_g = _dsa(qat, ka, vat, qit, ki, wit, ga, dsa_bias, tq, tk)

        os_, lses = [], []
        for j, ((wnd, dil), bias_p) in enumerate(zip(DIL_PATTERNS, dil_bias)):
            assert wnd // dil == BLOCK and s % (dil * BLOCK) == 0 and tm % (dil * 2 * SUBLANES) == 0
            n_sub = s // dil
            qb_blocks = min(4, n_sub // BLOCK)
            o, lse = _dilated_pattern(qviews[j], kviews[j], vviews[j], bias_p, dil, qb_blocks)
            os_.append(o.reshape(n // dil, dil * D_B))
            lses.append(lse.reshape(n // dil, dil * LANES))
        x = _out_proj(x.reshape(n, d), a_g.reshape(n, D_A), gb.reshape(n, D_B), w_out[layer].astype(BF16),
                      expand, os_, lses, dils, tm).reshape(b, s, d)
    return x
```

```python
import functools
import math

import jax
import jax.numpy as jnp
from jax import lax
from jax.experimental import pallas as pl
from jax.experimental.pallas import tpu as pltpu

HEAD_DIM = 64
N_HEADS_A = 8
N_HEADS_B = 8
D_A = N_HEADS_A * HEAD_DIM
D_B = N_HEADS_B * HEAD_DIM
N_IDX_HEADS = 4
IDX_DIM = 64
D_QI = N_IDX_HEADS * IDX_DIM
TOPK_MAX = 256
DIL_PATTERNS = ((128, 1), (512, 4), (2048, 16))
BLOCK = 128
N_BUCKETS = 32
MAX_DISTANCE = 128
EPS = 1e-6

LANES = 128
SUBLANES = 8
WI_ROWS = 16
NEG_BIG = -(2.0 ** 100)
LOG2E = math.log2(math.e)
DSA_TQ = 256
DSA_TK = 256
FAR_UNROLL = 4
VMEM_LIMIT = 56 * 1024 * 1024

F32 = jnp.float32
BF16 = jnp.bfloat16
NT_DIMS = (((1,), (1,)), ((), ()))


def _dot(a, b):
    return jnp.dot(a, b, preferred_element_type=F32)


def _dot_nt(a, b):
    return lax.dot_general(a, b, NT_DIMS, preferred_element_type=F32)


def _rel_bucket(dist):
    max_exact = N_BUCKETS // 2
    d = jnp.maximum(dist, 0)
    df = jnp.maximum(d, 1).astype(F32)
    large = max_exact + (jnp.log(df / max_exact) / math.log(MAX_DISTANCE / max_exact)
                         * (N_BUCKETS - max_exact)).astype(jnp.int32)
    large = jnp.minimum(large, N_BUCKETS - 1)
    return jnp.where(d < max_exact, d, large)


def _inproj_kernel(tk, dils, x_ref, g_ref, wn_ref, wt_ref, bd_ref, gn_ref, gcol_ref,
                   qat_ref, ka_ref, vat_ref, ga_ref, qit_ref, ki_ref, wit_ref, gb_ref, *rest):
    n_b = 3 * len(dils)
    b_refs = [rest[j * len(dils):(j + 1) * len(dils)] for j in range(3)]
    slab_refs = rest[n_b:]
    x = x_ref[0]
    tm = x.shape[0]
    r = lax.rsqrt(jnp.mean(x * x, axis=-1, keepdims=True) + EPS)
    xn = (x * r * g_ref[...]).astype(BF16)
    bd = bd_ref[...]

    def head_norm(q, gain, scale):
        ss = _dot((q * q).astype(BF16), bd)
        return q * lax.rsqrt(ss * (1.0 / HEAD_DIM) + EPS) * (gain * scale)

    def silu(z):
        return z * jax.nn.sigmoid(z)

    c0 = 2 * D_A + LANES
    p = _dot(xn, wn_ref[:, c0:c0 + 4 * D_B])
    gb_ref[0] = silu(p[:, 3 * D_B:4 * D_B]).astype(BF16)

    def emit_views(t, val, refs):
        assert dils[0] == 1
        refs[0][0] = val.astype(BF16)
        if len(dils) == 1:
            return
        npair = D_B // LANES
        for pr in range(npair):
            slab_refs[0][t, pr, 0] = val[:, pr * LANES:(pr + 1) * LANES]
        for lvl in range(1, len(dils)):
            d, d_prev = dils[lvl], dils[lvl - 1]
            f = d // d_prev
            for r in range(d):
                for pr in range(npair):
                    blk = slab_refs[lvl - 1][t, pr, r % d_prev, pl.ds(r // d_prev, tm // d, stride=f), :]
                    lane0 = r * D_B + pr * LANES
                    refs[lvl][0, :, lane0:lane0 + LANES] = blk.astype(BF16)
                    if lvl + 1 < len(dils):
                        slab_refs[lvl][t, pr, r] = blk

    emit_views(0, head_norm(p[:, 0:D_B], gn_ref[2:3, :], HEAD_DIM ** -0.5), b_refs[0])
    emit_views(1, head_norm(p[:, D_B:2 * D_B], gn_ref[3:4, :], 1.0), b_refs[1])
    emit_views(2, p[:, 2 * D_B:3 * D_B], b_refs[2])

    p = _dot(xn, wn_ref[:, 0:c0])
    ka_ref[0] = head_norm(p[:, 0:D_A], gn_ref[1:2, :], 1.0).astype(BF16)
    ga_ref[0] = silu(p[:, D_A:2 * D_A]).astype(BF16)
    ki_ref[0] = p[:, 2 * D_A:2 * D_A + LANES].astype(BF16)

    qt = _dot_nt(wt_ref[0:D_A, :], xn)
    ss = _dot(bd, (qt * qt).astype(BF16))
    qt = qt * lax.rsqrt(ss * (1.0 / HEAD_DIM) + EPS) * (gcol_ref[...] * (HEAD_DIM ** -0.5 * LOG2E))
    qat_ref[0] = qt.astype(BF16)
    vt = _dot_nt(wt_ref[D_A:2 * D_A, :], xn)
    for cc in range(tm // tk):
        vat_ref[0, cc] = vt[:, cc * tk:(cc + 1) * tk].astype(BF16)
    qit_ref[0] = _dot_nt(wt_ref[2 * D_A:2 * D_A + D_QI, :], xn).astype(BF16)
    wit_ref[0] = _dot_nt(wt_ref[2 * D_A + D_QI:2 * D_A + D_QI + WI_ROWS, :], xn)


def _inproj(x, g, wn, wt, bd, gn, gcol, tm, tk, dils):
    b, s, d = x.shape
    const = lambda bi, j: (0, 0)
    tok = lambda c: pl.BlockSpec((1, tm, c), lambda bi, j: (bi, j, 0))
    feat = lambda c: pl.BlockSpec((1, c, tm), lambda bi, j: (bi, 0, j))
    one_buf = dict(pipeline_mode=pl.Buffered(1))
    sds = jax.ShapeDtypeStruct
    view_shapes = tuple(sds((b, s // dl, dl * D_B), BF16) for dl in dils)
    view_specs = tuple(pl.BlockSpec((1, tm // dl, dl * D_B), lambda bi, j: (bi, j, 0)) for dl in dils)
    out_shapes = (sds((b, D_A, s), BF16), sds((b, s, D_A), BF16), sds((b, s // tk, D_A, tk), BF16),
                  sds((b, s, D_A), BF16), sds((b, D_QI, s), BF16), sds((b, s, LANES), BF16),
                  sds((b, WI_ROWS, s), F32), sds((b, s, D_B), BF16)) + view_shapes * 3
    out_specs = (feat(D_A), tok(D_A),
                 pl.BlockSpec((1, tm // tk, D_A, tk), lambda bi, j: (bi, j, 0, 0)),
                 tok(D_A), feat(D_QI), tok(LANES), feat(WI_ROWS), tok(D_B)) + view_specs * 3
    return pl.pallas_call(
        functools.partial(_inproj_kernel, tk, dils),
        grid=(b, s // tm),
        scratch_shapes=[pltpu.VMEM((3, D_B // LANES, dl, tm // dl, LANES), F32) for dl in dils[:-1]],
        in_specs=[tok(d),
                  pl.BlockSpec((1, d), const, **one_buf),
                  pl.BlockSpec(wn.shape, const, **one_buf),
                  pl.BlockSpec(wt.shape, const, **one_buf),
                  pl.BlockSpec((D_A, D_A), const, **one_buf),
                  pl.BlockSpec((4, D_A), const, **one_buf),
                  pl.BlockSpec((D_A, 1), const, **one_buf)],
        out_specs=out_specs,
        out_shape=out_shapes,
        compiler_params=pltpu.CompilerParams(dimension_semantics=("parallel", "parallel"),
                                             vmem_limit_bytes=VMEM_LIMIT),
        name="inproj",
    )(x, g, wn, wt, bd, gn, gcol)


def _ordered_bits_to_float(u):
    key = u ^ jnp.int32(-2 ** 31)
    bits = key ^ ((key >> 31) & jnp.int32(0x7FFFFFFF))
    return lax.bitcast_convert_type(bits, F32)


def _fold_rows(x, group):
    parts = [None] * 4
    for r in range(x.shape[0] // group):
        blk = x[r * group:(r + 1) * group, :]
        parts[r % 4] = blk if parts[r % 4] is None else parts[r % 4] + blk
    return (parts[0] + parts[1]) + (parts[2] + parts[3])


def _chunk_loop(n, body, init, unroll):
    def group(j, carry):
        for u in range(unroll):
            carry = body(unroll * j + u, carry)
        return carry
    carry = lax.fori_loop(0, n // unroll, group, init)
    return lax.fori_loop((n // unroll) * unroll, n, body, carry)


def _truncate_to_bf16(x):
    bits = lax.bitcast_convert_type(x, jnp.int32) & jnp.int32(-2 ** 16)
    return lax.bitcast_convert_type(bits, F32).astype(BF16)


def _dsa_kernel(topk, tq, tk, nkc, qat_ref, ka_ref, vat_ref, qit_ref, ki_ref, wit_ref, ga_ref, bias_ref,
                out_ref, sc_ref, sc16_ref, madd_ref, qz_ref, m_ref, l_ref, acc_ref):
    i = pl.program_id(1)
    ratio = tq // tk
    nchunks = (i + 1) * ratio
    idx_scale = (N_IDX_HEADS * IDX_DIM) ** -0.5
    top_half = lax.broadcasted_iota(jnp.int32, (LANES, tq), 0) < HEAD_DIM

    def pad_heads(ref, npairs):
        out = []
        for p in range(npairs):
            blk = ref[0, p * LANES:(p + 1) * LANES, :]
            zero = jnp.zeros_like(blk)
            out.append(jnp.where(top_half, blk, zero))
            out.append(jnp.where(top_half, zero, blk))
        return out

    qiz = pad_heads(qit_ref, N_IDX_HEADS // 2)
    w = wit_ref[0]
    krow = lax.broadcasted_iota(jnp.int32, (tk, tq), 0)
    qcol_g = i * tq + lax.broadcasted_iota(jnp.int32, (tk, tq), 1)

    def score_chunk(c, carry):
        kc = ki_ref[0, pl.ds(pl.multiple_of(c * tk, tk), tk), :]
        acc = jnp.zeros((tk, tq), F32)
        for h in range(N_IDX_HEADS):
            acc = acc + jnp.maximum(_dot(kc, qiz[h]), 0.0) * w[h:h + 1, :]
        s = acc * idx_scale + 0.0
        s = jnp.where(krow + c * tk <= qcol_g, s, -jnp.inf)
        sc_ref[c] = s
        sc16_ref[c] = _truncate_to_bf16(s)
        return carry

    _chunk_loop(nchunks, score_chunk, 0, 2)

    def count_pass(pred):
        def body(c, acc):
            return acc + _fold_rows(jnp.where(pred(sc_ref[c], c), 1.0, 0.0), SUBLANES)
        acc = _chunk_loop(nchunks, body, jnp.zeros((SUBLANES, tq), F32), 2)
        return jnp.sum(acc, axis=0, keepdims=True)

    def count_pass16(cand16):
        one, zero = jnp.ones((), BF16), jnp.zeros((), BF16)

        def body(c, acc):
            return acc + _fold_rows(jnp.where(sc16_ref[c] >= cand16, one, zero), 2 * SUBLANES)
        acc = _chunk_loop(nchunks, body, jnp.zeros((2 * SUBLANES, tq), BF16), 2)
        return jnp.sum(acc.astype(F32), axis=0, keepdims=True)

    def bit_step16(it, cur):
        cand = cur | lax.shift_left(jnp.int32(1), 31 - it)
        cnt = count_pass16(_truncate_to_bf16(_ordered_bits_to_float(cand)))
        return jnp.where(cnt >= float(topk), cand, cur)

    def bit_step(it, cur):
        cand = cur | lax.shift_left(jnp.int32(1), 31 - it)
        candf = _ordered_bits_to_float(cand)
        cnt = count_pass(lambda s, c: s >= candf)
        return jnp.where(cnt >= float(topk), cand, cur)

    assert nkc * tk <= 256 * 2 * SUBLANES
    cur = lax.fori_loop(0, 16, bit_step16, jnp.zeros((1, tq), jnp.int32))
    cur = lax.fori_loop(16, 32, bit_step, cur)
    short = (cur & jnp.int32(-2 ** 23)) == 0
    thr = jnp.where(short, -jnp.inf, _ordered_bits_to_float(cur))
    cnt_gt = count_pass(lambda s, c: s > thr)
    n_tie = jnp.where(short, 0.0, float(topk) - cnt_gt)
    below = (lax.broadcasted_iota(jnp.int32, (tk, tk), 1)
             < lax.broadcasted_iota(jnp.int32, (tk, tk), 0)).astype(BF16)
    ones_row = jnp.ones((SUBLANES, tk), BF16)

    def mask_chunk(c, seen):
        s = sc_ref[c]
        tie = s == thr
        tie01 = jnp.where(tie, 1.0, 0.0).astype(BF16)
        rank = _dot(below, tie01) + seen
        sel = (s > thr) | (tie & (rank < n_tie))
        madd_ref[c] = jnp.where(sel, 0.0, NEG_BIG).astype(BF16)
        return seen + _dot(ones_row, tie01)[0:1, :]

    _chunk_loop(nchunks, mask_chunk, jnp.zeros((1, tq), F32), 2)

    for h, qz in enumerate(pad_heads(qat_ref, N_HEADS_A // 2)):
        qz_ref[h] = qz
    m_ref[...] = jnp.full(m_ref.shape, NEG_BIG, F32)
    l_ref[...] = jnp.zeros(l_ref.shape, F32)
    acc_ref[...] = jnp.zeros(acc_ref.shape, F32)

    def attn_chunk(c, variant):
        k0 = pl.multiple_of(c * tk, tk)
        for h in range(N_HEADS_A):
            p = h // 2
            kp = ka_ref[0, pl.ds(k0, tk), p * LANES:(p + 1) * LANES]
            logits = _dot(kp, qz_ref[h]).astype(BF16) + madd_ref[c]
            if variant is not None:
                logits = logits + bias_ref[variant, h]
            hrow = slice(h, h + 1)
            m_old = m_ref[hrow, :]
            m_new = jnp.maximum(m_old, jnp.max(logits, axis=0, keepdims=True).astype(F32))
            alpha = jnp.exp2(m_old - m_new)
            pr = jnp.exp2(logits - m_new.astype(BF16))
            l_ref[hrow, :] = alpha * l_ref[hrow, :] + _dot(ones_row, pr)[0:1, :]
            m_ref[hrow, :] = m_new
            rows = slice(h * HEAD_DIM, (h + 1) * HEAD_DIM)
            acc_ref[rows, :] = alpha * acc_ref[rows, :] + _dot(vat_ref[0, c, rows, :], pr)

    first_near = i * ratio - 1
    n_far = jnp.maximum(first_near, 0)

    def far_group(j, carry):
        for u in range(FAR_UNROLL):
            attn_chunk(FAR_UNROLL * j + u, None)
        return carry

    def far_body(c, carry):
        attn_chunk(c, None)
        return carry

    n_grouped = (n_far // FAR_UNROLL) * FAR_UNROLL
    lax.fori_loop(0, n_far // FAR_UNROLL, far_group, 0)
    lax.fori_loop(n_grouped, n_far, far_body, 0)

    @pl.when(i > 0)
    def _():
        for v in range(ratio + 1):
            attn_chunk(first_near + v, v)

    @pl.when(i == 0)
    def _():
        for v in range(1, ratio + 1):
            attn_chunk(v - 1, v)

    outs = []
    for h in range(N_HEADS_A):
        rows = slice(h * HEAD_DIM, (h + 1) * HEAD_DIM)
        outs.append(acc_ref[rows, :] / l_ref[h:h + 1, :])
    o = jnp.concatenate(outs, axis=0).T
    out_ref[0] = (o * ga_ref[0].astype(F32)).astype(BF16)


def _dsa(qat, ka, vat, qit, ki, wit, ga, bias_a, tq, tk):
    b, s, _ = ka.shape
    topk = min(TOPK_MAX, s // 4)
    nq = s // tq
    nkc = s // tk
    qfeat = lambda c: pl.BlockSpec((1, c, tq), lambda bi, i: (bi, 0, i))
    qtok = lambda c: pl.BlockSpec((1, tq, c), lambda bi, i: (bi, i, 0))
    full = lambda c: pl.BlockSpec((1, s, c), lambda bi, i: (bi, 0, 0))
    return pl.pallas_call(
        functools.partial(_dsa_kernel, topk, tq, tk, nkc),
        grid=(b, nq),
        in_specs=[qfeat(D_A), full(D_A),
                  pl.BlockSpec((1, nkc, D_A, tk), lambda bi, i: (bi, 0, 0, 0)),
                  qfeat(D_QI), full(LANES), qfeat(WI_ROWS), qtok(D_A),
                  pl.BlockSpec(bias_a.shape, lambda bi, i: (0, 0, 0, 0), pipeline_mode=pl.Buffered(1))],
        out_specs=qtok(D_A),
        out_shape=jax.ShapeDtypeStruct((b, s, D_A), BF16),
        scratch_shapes=[pltpu.VMEM((nkc, tk, tq), F32),
                        pltpu.VMEM((nkc, tk, tq), BF16),
                        pltpu.VMEM((nkc, tk, tq), BF16),
                        pltpu.VMEM((N_HEADS_A, LANES, tq), BF16),
                        pltpu.VMEM((N_HEADS_A, tq), F32),
                        pltpu.VMEM((N_HEADS_A, tq), F32),
                        pltpu.VMEM((D_A, tq), F32)],
        compiler_params=pltpu.CompilerParams(dimension_semantics=("parallel", "arbitrary"),
                                             vmem_limit_bytes=VMEM_LIMIT),
        name="dsa",
    )(qat, ka, vat, qit, ki, wit, ga, bias_a)


def _dil_kernel(qb_blocks, q_ref, kc_ref, kp_ref, vc_ref, vp_ref, bias_ref, o_ref, lse_ref):
    j = pl.program_id(2)
    lane = lax.broadcasted_iota(jnp.int32, (BLOCK, LANES), 1)
    low_half = lane < HEAD_DIM
    for blk in range(qb_blocks):
        rows = slice(blk * BLOCK, (blk + 1) * BLOCK)
        if blk == 0:
            variant = jnp.where(j == 0, 1, 0)
        else:
            variant = 0
        lse_tile = jnp.zeros((BLOCK, LANES), F32)
        for p in range(N_HEADS_B // 2):
            lanes = slice(p * LANES, (p + 1) * LANES)
            if blk == 0:
                kband = jnp.concatenate([kp_ref[0, :, lanes], kc_ref[0, 0:BLOCK, lanes]], axis=0)
                vband = jnp.concatenate([vp_ref[0, :, lanes], vc_ref[0, 0:BLOCK, lanes]], axis=0)
            else:
                band = slice((blk - 1) * BLOCK, (blk + 1) * BLOCK)
                kband = kc_ref[0, band, lanes]
                vband = vc_ref[0, band, lanes]
            qp = q_ref[0, rows, lanes]
            zero = jnp.zeros_like(qp)
            outs = []
            for e in range(2):
                h = 2 * p + e
                qz = jnp.where(low_half, qp, zero) if e == 0 else jnp.where(low_half, zero, qp)
                logits = _dot_nt(qz, kband) + bias_ref[variant, h]
                m = jnp.max(logits, axis=1, keepdims=True)
                pr = jnp.exp(logits - m)
                den = jnp.sum(pr, axis=1, keepdims=True)
                num = _dot(pr.astype(BF16), vband)
                outs.append(num / den)
                lse_tile = lse_tile + jnp.where(lane == h, m + jnp.log(den), 0.0)
            o_ref[0, rows, lanes] = jnp.where(low_half, outs[0], outs[1])
        lse_ref[0, rows, :] = lse_tile


def _dilated_pattern(qv, kv, vv, bias_p, dil, qb_blocks):
    b, n_sub, _ = qv.shape
    tu = qb_blocks * BLOCK
    nj = n_sub // tu
    cur = pl.BlockSpec((1, tu, D_B), lambda bi, r, j: (bi, j, r))
    prev = pl.BlockSpec((1, BLOCK, D_B), lambda bi, r, j: (bi, jnp.maximum(j * qb_blocks - 1, 0), r))
    return pl.pallas_call(
        functools.partial(_dil_kernel, qb_blocks),
        grid=(b, dil, nj),
        in_specs=[cur, cur, prev, cur, prev,
                  pl.BlockSpec((2, N_HEADS_B, BLOCK, 2 * BLOCK), lambda bi, r, j: (0, 0, 0, 0),
                               pipeline_mode=pl.Buffered(1))],
        out_specs=(pl.BlockSpec((1, tu, D_B), lambda bi, r, j: (bi, j, r)),
                   pl.BlockSpec((1, tu, LANES), lambda bi, r, j: (bi, j, r))),
        out_shape=(jax.ShapeDtypeStruct((b, n_sub, dil * D_B), F32),
                   jax.ShapeDtypeStruct((b, n_sub, dil * LANES), F32)),
        compiler_params=pltpu.CompilerParams(dimension_semantics=("parallel", "parallel", "arbitrary"),
                                             vmem_limit_bytes=VMEM_LIMIT),
        name=f"dilated_d{dil}",
    )(qv, kv, kv, vv, vv, bias_p)


def _out_kernel(dils, x_ref, a_ref, gb_ref, w_ref, e_ref, *refs):
    npat = len(dils)
    o_refs = refs[0:npat]
    lse_refs = refs[npat:2 * npat]
    out_ref = refs[2 * npat]
    oslab_ref, lslab_ref = refs[2 * npat + 1:2 * npat + 3]
    tm = x_ref.shape[0]

    def natural_lse(j, lse_ref, d):
        if d == 1:
            return lse_ref[...]
        for r in range(d):
            lslab_ref[j, pl.ds(r, tm // d, stride=d), :] = lse_ref[:, r * LANES:(r + 1) * LANES]
        return lslab_ref[j]

    def natural_o(o_ref, d):
        if d == 1:
            return o_ref[...]
        for r in range(d):
            for pr in range(D_B // LANES):
                lane0 = r * D_B + pr * LANES
                oslab_ref[pr, pl.ds(r, tm // d, stride=d), :] = o_ref[:, lane0:lane0 + LANES]
        return jnp.concatenate([oslab_ref[pr] for pr in range(D_B // LANES)], axis=1)

    lses = [natural_lse(j, lse_ref, d) for j, (lse_ref, d) in enumerate(zip(lse_refs, dils))]
    top = functools.reduce(jnp.maximum, lses)
    ws = [jnp.exp(l - top) for l in lses]
    tot = functools.reduce(lambda a, c: a + c, ws)
    e = e_ref[...]
    bmix = jnp.zeros((tm, D_B), F32)
    for wgt, o_ref, d in zip(ws, o_refs, dils):
        wn = wgt / tot
        hi = wn.astype(BF16)
        lo = (wn - hi.astype(F32)).astype(BF16)
        bmix = bmix + (_dot(hi, e) + _dot(lo, e)) * natural_o(o_ref, d)
    bg = (bmix * gb_ref[...].astype(F32)).astype(BF16)
    out_ref[...] = x_ref[...] + _dot(a_ref[...], w_ref[0:D_A, :]) + _dot(bg, w_ref[D_A:D_A + D_B, :])


def _out_proj(x2, a_g, gb, w_out, expand, os_, lses, dils, tm):
    n, d = x2.shape
    row = lambda i: (i, 0)
    const = lambda i: (0, 0)
    one_buf = dict(pipeline_mode=pl.Buffered(1))
    in_specs = ([pl.BlockSpec((tm, d), row), pl.BlockSpec((tm, D_A), row), pl.BlockSpec((tm, D_B), row),
                 pl.BlockSpec((D_A + D_B, d), const, **one_buf),
                 pl.BlockSpec((LANES, D_B), const, **one_buf)]
                + [pl.BlockSpec((tm // dl, dl * D_B), row) for dl in dils]
                + [pl.BlockSpec((tm // dl, dl * LANES), row) for dl in dils])
    return pl.pallas_call(
        functools.partial(_out_kernel, dils),
        grid=(n // tm,),
        in_specs=in_specs,
        out_specs=pl.BlockSpec((tm, d), row),
        out_shape=jax.ShapeDtypeStruct((n, d), F32),
        scratch_shapes=[pltpu.VMEM((D_B // LANES, tm, LANES), F32), pltpu.VMEM((len(dils), tm, LANES), F32)],
        compiler_params=pltpu.CompilerParams(dimension_semantics=("parallel",),
                                             vmem_limit_bytes=VMEM_LIMIT),
        name="merge_outproj",
    )(x2, a_g, gb, w_out, expand, *os_, *lses)


def _pack_w_in(w):
    qa, ka, va, za = (w[:, j * D_A:(j + 1) * D_A] for j in range(4))
    o = 4 * D_A
    qi = w[:, o:o + D_QI]
    ki = w[:, o + D_QI:o + D_QI + IDX_DIM]
    wi = w[:, o + D_QI + IDX_DIM:o + D_QI + IDX_DIM + N_IDX_HEADS]
    wb = w[:, o + D_QI + IDX_DIM + N_IDX_HEADS:]
    wn = jnp.concatenate([ka, za, ki, ki, wb], axis=1).astype(BF16)
    wi = jnp.pad(wi, ((0, 0), (0, WI_ROWS - N_IDX_HEADS)))
    wt = jnp.concatenate([qa, va, qi, wi], axis=1).T.astype(BF16)
    return wn, wt


def _bias_lookup(table, bucket):
    onehot = jax.nn.one_hot(bucket, N_BUCKETS, dtype=F32)
    return jnp.einsum('...n,nh->...h', onehot, table.astype(F32), precision=lax.Precision.HIGHEST)


def _dsa_bias_tiles(bias_a, tq, tk):
    k = jnp.arange(tk)[:, None]
    q = jnp.arange(tq)[None, :]
    far = bias_a[N_BUCKETS - 1].astype(F32)
    tiles = [_bias_lookup(bias_a, _rel_bucket(q - k + tk * (1 - v))) - far for v in range(1 + tq // tk)]
    return (jnp.transpose(jnp.stack(tiles), (0, 3, 1, 2)) * LOG2E).astype(BF16)


def _dil_bias_tiles(bias_b, window, dil):
    steps = window // dil
    a = jnp.arange(BLOCK)[:, None]
    c = jnp.arange(2 * BLOCK)[None, :]
    diff = a - c + BLOCK
    band = (diff >= 0) & (diff <= steps)
    bias = jnp.transpose(_bias_lookup(bias_b, _rel_bucket(diff * dil)), (2, 0, 1))
    normal = jnp.where(band[None], bias, NEG_BIG)
    first = jnp.where((band & (c >= BLOCK))[None], bias, NEG_BIG)
    return jnp.stack([normal, first])


def kernel(x, norm_gain, w_in, w_out, rel_bias, q_norm_a, k_norm_a, q_norm_b, k_norm_b):
    b, s, d = x.shape
    n = b * s
    depth = norm_gain.shape[0]
    tq = min(DSA_TQ, s)
    tk = min(DSA_TK, s)
    tm = min(512, s)
    bias_a = rel_bias[:, :N_HEADS_A]
    bias_b = rel_bias[:, N_HEADS_A:]
    eye_blocks = (jnp.arange(D_A)[:, None] // HEAD_DIM == jnp.arange(D_A)[None, :] // HEAD_DIM).astype(BF16)
    expand = (jnp.arange(LANES)[:, None] == jnp.arange(D_B)[None, :] // HEAD_DIM).astype(BF16)
    assert tk >= MAX_DISTANCE and tq % tk == 0
    dsa_bias = _dsa_bias_tiles(bias_a, tq, tk)
    dil_bias = [_dil_bias_tiles(bias_b, wnd, dil) for wnd, dil in DIL_PATTERNS]
    dils = tuple(dil for _, dil in DIL_PATTERNS)

    for layer in range(depth):
        gn = jnp.stack([jnp.tile(g[layer], N_HEADS_A) for g in (q_norm_a, k_norm_a, q_norm_b, k_norm_b)])
        wn, wt = _pack_w_in(w_in[layer])
        npat = len(dils)
        outs = _inproj(x, norm_gain[layer][None, :], wn, wt, eye_blocks, gn, gn[0][:, None], tm, tk, dils)
        qat, ka, vat, ga, qit, ki, wit, gb = outs[:8]
        qviews, kviews, vviews = (outs[8 + j * npat:8 + (j + 1) * npat] for j in range(3))
        a_g = _dsa(qat, ka, vat, qit, ki, wit, ga, dsa_bias, tq, tk)

        os_, lses = [], []
        for j, ((wnd, dil), bias_p) in enumerate(zip(DIL_PATTERNS, dil_bias)):
            assert wnd // dil == BLOCK and s % (dil * BLOCK) == 0 and tm % (dil * 2 * SUBLANES) == 0
            n_sub = s // dil
            qb_blocks = min(4, n_sub // BLOCK)
            o, lse = _dilated_pattern(qviews[j], kviews[j], vviews[j], bias_p, dil, qb_blocks)
            os_.append(o.reshape(n // dil, dil * D_B))
            lses.append(lse.reshape(n // dil, dil * LANES))
        x = _out_proj(x.reshape(n, d), a_g.reshape(n, D_A), gb.reshape(n, D_B), w_out[layer].astype(BF16),
                      expand, os_, lses, dils, tm).reshape(b, s, d)
    return x
```

```python
import functools
import math

import jax
import jax.numpy as jnp
from jax import lax
from jax.experimental import pallas as pl
from jax.experimental.pallas import tpu as pltpu

HEAD_DIM = 64
N_HEADS_A = 8
N_HEADS_B = 8
D_A = N_HEADS_A * HEAD_DIM
D_B = N_HEADS_B * HEAD_DIM
N_IDX_HEADS = 4
IDX_DIM = 64
D_QI = N_IDX_HEADS * IDX_DIM
TOPK_MAX = 256
DIL_PATTERNS = ((128, 1), (512, 4), (2048, 16))
BLOCK = 128
N_BUCKETS = 32
MAX_DISTANCE = 128
EPS = 1e-6

LANES = 128
SUBLANES = 8
WI_ROWS = 16
NEG_BIG = -(2.0 ** 100)
LOG2E = math.log2(math.e)
DSA_TQ = 256
DSA_TK = 256
FAR_UNROLL = 4
VMEM_LIMIT = 56 * 1024 * 1024

F32 = jnp.float32
BF16 = jnp.bfloat16
NT_DIMS = (((1,), (1,)), ((), ()))


def _dot(a, b):
    return jnp.dot(a, b, preferred_element_type=F32)


def _dot_nt(a, b):
    return lax.dot_general(a, b, NT_DIMS, preferred_element_type=F32)


def _rel_bucket(dist):
    max_exact = N_BUCKETS // 2
    d = jnp.maximum(dist, 0)
    df = jnp.maximum(d, 1).astype(F32)
    large = max_exact + (jnp.log(df / max_exact) / math.log(MAX_DISTANCE / max_exact)
                         * (N_BUCKETS - max_exact)).astype(jnp.int32)
    large = jnp.minimum(large, N_BUCKETS - 1)
    return jnp.where(d < max_exact, d, large)


def _inproj_kernel(tk, dils, x_ref, g_ref, wn_ref, wt_ref, bd_ref, gn_ref, gcol_ref,
                   qat_ref, ka_ref, vat_ref, ga_ref, qit_ref, ki_ref, wit_ref, gb_ref, *rest):
    n_b = 3 * len(dils)
    b_refs = [rest[j * len(dils):(j + 1) * len(dils)] for j in range(3)]
    slab_refs = rest[n_b:]
    x = x_ref[0]
    tm = x.shape[0]
    r = lax.rsqrt(jnp.mean(x * x, axis=-1, keepdims=True) + EPS)
    xn = (x * r * g_ref[...]).astype(BF16)
    bd = bd_ref[...]

    def head_norm(q, gain, scale):
        ss = _dot((q * q).astype(BF16), bd)
        return q * lax.rsqrt(ss * (1.0 / HEAD_DIM) + EPS) * (gain * scale)

    def silu(z):
        return z * jax.nn.sigmoid(z)

    c0 = 2 * D_A + LANES
    p = _dot(xn, wn_ref[:, c0:c0 + 4 * D_B])
    gb_ref[0] = silu(p[:, 3 * D_B:4 * D_B]).astype(BF16)

    def emit_views(t, val, refs):
        assert dils[0] == 1
        refs[0][0] = val.astype(BF16)
        if len(dils) == 1:
            return
        npair = D_B // LANES
        for pr in range(npair):
            slab_refs[0][t, pr, 0] = val[:, pr * LANES:(pr + 1) * LANES]
        for lvl in range(1, len(dils)):
            d, d_prev = dils[lvl], dils[lvl - 1]
            f = d // d_prev
            for r in range(d):
                for pr in range(npair):
                    blk = slab_refs[lvl - 1][t, pr, r % d_prev, pl.ds(r // d_prev, tm // d, stride=f), :]
                    lane0 = r * D_B + pr * LANES
                    refs[lvl][0, :, lane0:lane0 + LANES] = blk.astype(BF16)
                    if lvl + 1 < len(dils):
                        slab_refs[lvl][t, pr, r] = blk

    emit_views(0, head_norm(p[:, 0:D_B], gn_ref[2:3, :], HEAD_DIM ** -0.5 * LOG2E), b_refs[0])
    emit_views(1, head_norm(p[:, D_B:2 * D_B], gn_ref[3:4, :], 1.0), b_refs[1])
    emit_views(2, p[:, 2 * D_B:3 * D_B], b_refs[2])

    p = _dot(xn, wn_ref[:, 0:c0])
    ka_ref[0] = head_norm(p[:, 0:D_A], gn_ref[1:2, :], 1.0).astype(BF16)
    ga_ref[0] = silu(p[:, D_A:2 * D_A]).astype(BF16)
    ki_ref[0] = p[:, 2 * D_A:2 * D_A + LANES].astype(BF16)

    qt = _dot_nt(wt_ref[0:D_A, :], xn)
    ss = _dot(bd, (qt * qt).astype(BF16))
    qt = qt * lax.rsqrt(ss * (1.0 / HEAD_DIM) + EPS) * (gcol_ref[...] * (HEAD_DIM ** -0.5 * LOG2E))
    qat_ref[0] = qt.astype(BF16)
    vt = _dot_nt(wt_ref[D_A:2 * D_A, :], xn)
    for cc in range(tm // tk):
        vat_ref[0, cc] = vt[:, cc * tk:(cc + 1) * tk].astype(BF16)
    qit_ref[0] = _dot_nt(wt_ref[2 * D_A:2 * D_A + D_QI, :], xn).astype(BF16)
    wit_ref[0] = _dot_nt(wt_ref[2 * D_A + D_QI:2 * D_A + D_QI + WI_ROWS, :], xn)


def _inproj(x, g, wn, wt, bd, gn, gcol, tm, tk, dils):
    b, s, d = x.shape
    const = lambda bi, j: (0, 0)
    tok = lambda c: pl.BlockSpec((1, tm, c), lambda bi, j: (bi, j, 0))
    feat = lambda c: pl.BlockSpec((1, c, tm), lambda bi, j: (bi, 0, j))
    one_buf = dict(pipeline_mode=pl.Buffered(1))
    sds = jax.ShapeDtypeStruct
    view_shapes = tuple(sds((b, s // dl, dl * D_B), BF16) for dl in dils)
    view_specs = tuple(pl.BlockSpec((1, tm // dl, dl * D_B), lambda bi, j: (bi, j, 0)) for dl in dils)
    out_shapes = (sds((b, D_A, s), BF16), sds((b, s, D_A), BF16), sds((b, s // tk, D_A, tk), BF16),
                  sds((b, s, D_A), BF16), sds((b, D_QI, s), BF16), sds((b, s, LANES), BF16),
                  sds((b, WI_ROWS, s), F32), sds((b, s, D_B), BF16)) + view_shapes * 3
    out_specs = (feat(D_A), tok(D_A),
                 pl.BlockSpec((1, tm // tk, D_A, tk), lambda bi, j: (bi, j, 0, 0)),
                 tok(D_A), feat(D_QI), tok(LANES), feat(WI_ROWS), tok(D_B)) + view_specs * 3
    return pl.pallas_call(
        functools.partial(_inproj_kernel, tk, dils),
        grid=(b, s // tm),
        scratch_shapes=[pltpu.VMEM((3, D_B // LANES, dl, tm // dl, LANES), F32) for dl in dils[:-1]],
        in_specs=[tok(d),
                  pl.BlockSpec((1, d), const, **one_buf),
                  pl.BlockSpec(wn.shape, const, **one_buf),
                  pl.BlockSpec(wt.shape, const, **one_buf),
                  pl.BlockSpec((D_A, D_A), const, **one_buf),
                  pl.BlockSpec((4, D_A), const, **one_buf),
                  pl.BlockSpec((D_A, 1), const, **one_buf)],
        out_specs=out_specs,
        out_shape=out_shapes,
        compiler_params=pltpu.CompilerParams(dimension_semantics=("parallel", "parallel"),
                                             vmem_limit_bytes=VMEM_LIMIT),
        name="inproj",
    )(x, g, wn, wt, bd, gn, gcol)


def _ordered_bits_to_float(u):
    key = u ^ jnp.int32(-2 ** 31)
    bits = key ^ ((key >> 31) & jnp.int32(0x7FFFFFFF))
    return lax.bitcast_convert_type(bits, F32)


def _fold_rows(x, group):
    parts = [None] * 4
    for r in range(x.shape[0] // group):
        blk = x[r * group:(r + 1) * group, :]
        parts[r % 4] = blk if parts[r % 4] is None else parts[r % 4] + blk
    return (parts[0] + parts[1]) + (parts[2] + parts[3])


def _chunk_loop(n, body, init, unroll):
    def group(j, carry):
        for u in range(unroll):
            carry = body(unroll * j + u, carry)
        return carry
    carry = lax.fori_loop(0, n // unroll, group, init)
    return lax.fori_loop((n // unroll) * unroll, n, body, carry)


def _truncate_to_bf16(x):
    bits = lax.bitcast_convert_type(x, jnp.int32) & jnp.int32(-2 ** 16)
    return lax.bitcast_convert_type(bits, F32).astype(BF16)


def _dsa_kernel(topk, tq, tk, nkc, qat_ref, ka_ref, vat_ref, qit_ref, ki_ref, wit_ref, ga_ref, bias_ref,
                out_ref, sc_ref, sc16_ref, madd_ref, qz_ref, m_ref, l_ref, acc_ref):
    i = pl.program_id(1)
    ratio = tq // tk
    nchunks = (i + 1) * ratio
    idx_scale = (N_IDX_HEADS * IDX_DIM) ** -0.5
    top_half = lax.broadcasted_iota(jnp.int32, (LANES, tq), 0) < HEAD_DIM

    def pad_heads(ref, npairs):
        out = []
        for p in range(npairs):
            blk = ref[0, p * LANES:(p + 1) * LANES, :]
            zero = jnp.zeros_like(blk)
            out.append(jnp.where(top_half, blk, zero))
            out.append(jnp.where(top_half, zero, blk))
        return out

    qiz = pad_heads(qit_ref, N_IDX_HEADS // 2)
    w = wit_ref[0]
    krow = lax.broadcasted_iota(jnp.int32, (tk, tq), 0)
    qcol_g = i * tq + lax.broadcasted_iota(jnp.int32, (tk, tq), 1)

    def score_chunk(c, carry):
        kc = ki_ref[0, pl.ds(pl.multiple_of(c * tk, tk), tk), :]
        acc = jnp.zeros((tk, tq), F32)
        for h in range(N_IDX_HEADS):
            acc = acc + jnp.maximum(_dot(kc, qiz[h]), 0.0) * w[h:h + 1, :]
        s = acc * idx_scale + 0.0
        s = jnp.where(krow + c * tk <= qcol_g, s, -jnp.inf)
        sc_ref[c] = s
        sc16_ref[c] = _truncate_to_bf16(s)
        return carry

    _chunk_loop(nchunks, score_chunk, 0, 2)

    def count_pass(pred):
        def body(c, acc):
            return acc + _fold_rows(jnp.where(pred(sc_ref[c], c), 1.0, 0.0), SUBLANES)
        acc = _chunk_loop(nchunks, body, jnp.zeros((SUBLANES, tq), F32), 2)
        return jnp.sum(acc, axis=0, keepdims=True)

    def count_pass16(cand16):
        one, zero = jnp.ones((), BF16), jnp.zeros((), BF16)

        def body(c, acc):
            return acc + _fold_rows(jnp.where(sc16_ref[c] >= cand16, one, zero), 2 * SUBLANES)
        acc = _chunk_loop(nchunks, body, jnp.zeros((2 * SUBLANES, tq), BF16), 2)
        return jnp.sum(acc.astype(F32), axis=0, keepdims=True)

    def bit_step16(it, cur):
        cand = cur | lax.shift_left(jnp.int32(1), 31 - it)
        cnt = count_pass16(_truncate_to_bf16(_ordered_bits_to_float(cand)))
        return jnp.where(cnt >= float(topk), cand, cur)

    def bit_step(it, cur):
        cand = cur | lax.shift_left(jnp.int32(1), 31 - it)
        candf = _ordered_bits_to_float(cand)
        cnt = count_pass(lambda s, c: s >= candf)
        return jnp.where(cnt >= float(topk), cand, cur)

    assert nkc * tk <= 256 * 2 * SUBLANES
    cur = lax.fori_loop(0, 16, bit_step16, jnp.zeros((1, tq), jnp.int32))
    cur = lax.fori_loop(16, 32, bit_step, cur)
    short = (cur & jnp.int32(-2 ** 23)) == 0
    thr = jnp.where(short, -jnp.inf, _ordered_bits_to_float(cur))
    cnt_gt = count_pass(lambda s, c: s > thr)
    n_tie = jnp.where(short, 0.0, float(topk) - cnt_gt)
    below = (lax.broadcasted_iota(jnp.int32, (tk, tk), 1)
             < lax.broadcasted_iota(jnp.int32, (tk, tk), 0)).astype(BF16)
    ones_row = jnp.ones((SUBLANES, tk), BF16)

    def mask_chunk(c, seen):
        s = sc_ref[c]
        tie = s == thr
        tie01 = jnp.where(tie, 1.0, 0.0).astype(BF16)
        rank = _dot(below, tie01) + seen
        sel = (s > thr) | (tie & (rank < n_tie))
        madd_ref[c] = jnp.where(sel, 0.0, NEG_BIG).astype(BF16)
        return seen + _dot(ones_row, tie01)[0:1, :]

    _chunk_loop(nchunks, mask_chunk, jnp.zeros((1, tq), F32), 2)

    for h, qz in enumerate(pad_heads(qat_ref, N_HEADS_A // 2)):
        qz_ref[h] = qz
    m_ref[...] = jnp.full(m_ref.shape, NEG_BIG, F32)
    l_ref[...] = jnp.zeros(l_ref.shape, F32)
    acc_ref[...] = jnp.zeros(acc_ref.shape, F32)

    def attn_chunk(c, variant):
        k0 = pl.multiple_of(c * tk, tk)
        for h in range(N_HEADS_A):
            p = h // 2
            kp = ka_ref[0, pl.ds(k0, tk), p * LANES:(p + 1) * LANES]
            logits = _dot(kp, qz_ref[h]).astype(BF16) + madd_ref[c]
            if variant is not None:
                logits = logits + bias_ref[variant, h]
            hrow = slice(h, h + 1)
            m_old = m_ref[hrow, :]
            m_new = jnp.maximum(m_old, jnp.max(logits, axis=0, keepdims=True).astype(F32))
            alpha = jnp.exp2(m_old - m_new)
            pr = jnp.exp2(logits - m_new.astype(BF16))
            l_ref[hrow, :] = alpha * l_ref[hrow, :] + _dot(ones_row, pr)[0:1, :]
            m_ref[hrow, :] = m_new
            rows = slice(h * HEAD_DIM, (h + 1) * HEAD_DIM)
            acc_ref[rows, :] = alpha * acc_ref[rows, :] + _dot(vat_ref[0, c, rows, :], pr)

    first_near = i * ratio - 1
    n_far = jnp.maximum(first_near, 0)

    def far_group(j, carry):
        for u in range(FAR_UNROLL):
            attn_chunk(FAR_UNROLL * j + u, None)
        return carry

    def far_body(c, carry):
        attn_chunk(c, None)
        return carry

    n_grouped = (n_far // FAR_UNROLL) * FAR_UNROLL
    lax.fori_loop(0, n_far // FAR_UNROLL, far_group, 0)
    lax.fori_loop(n_grouped, n_far, far_body, 0)

    @pl.when(i > 0)
    def _():
        for v in range(ratio + 1):
            attn_chunk(first_near + v, v)

    @pl.when(i == 0)
    def _():
        for v in range(1, ratio + 1):
            attn_chunk(v - 1, v)

    outs = []
    for h in range(N_HEADS_A):
        rows = slice(h * HEAD_DIM, (h + 1) * HEAD_DIM)
        outs.append(acc_ref[rows, :] / l_ref[h:h + 1, :])
    o = jnp.concatenate(outs, axis=0).T
    out_ref[0] = (o * ga_ref[0].astype(F32)).astype(BF16)


def _dsa(qat, ka, vat, qit, ki, wit, ga, bias_a, tq, tk):
    b, s, _ = ka.shape
    topk = min(TOPK_MAX, s // 4)
    nq = s // tq
    nkc = s // tk
    qfeat = lambda c: pl.BlockSpec((1, c, tq), lambda bi, i: (bi, 0, i))
    qtok = lambda c: pl.BlockSpec((1, tq, c), lambda bi, i: (bi, i, 0))
    full = lambda c: pl.BlockSpec((1, s, c), lambda bi, i: (bi, 0, 0))
    return pl.pallas_call(
        functools.partial(_dsa_kernel, topk, tq, tk, nkc),
        grid=(b, nq),
        in_specs=[qfeat(D_A), full(D_A),
                  pl.BlockSpec((1, nkc, D_A, tk), lambda bi, i: (bi, 0, 0, 0)),
                  qfeat(D_QI), full(LANES), qfeat(WI_ROWS), qtok(D_A),
                  pl.BlockSpec(bias_a.shape, lambda bi, i: (0, 0, 0, 0), pipeline_mode=pl.Buffered(1))],
        out_specs=qtok(D_A),
        out_shape=jax.ShapeDtypeStruct((b, s, D_A), BF16),
        scratch_shapes=[pltpu.VMEM((nkc, tk, tq), F32),
                        pltpu.VMEM((nkc, tk, tq), BF16),
                        pltpu.VMEM((nkc, tk, tq), BF16),
                        pltpu.VMEM((N_HEADS_A, LANES, tq), BF16),
                        pltpu.VMEM((N_HEADS_A, tq), F32),
                        pltpu.VMEM((N_HEADS_A, tq), F32),
                        pltpu.VMEM((D_A, tq), F32)],
        compiler_params=pltpu.CompilerParams(dimension_semantics=("parallel", "arbitrary"),
                                             vmem_limit_bytes=VMEM_LIMIT),
        name="dsa",
    )(qat, ka, vat, qit, ki, wit, ga, bias_a)


def _dil_kernel(qb_blocks, q_ref, kc_ref, kp_ref, vc_ref, vp_ref, bias_ref, o_ref, lse_ref):
    j = pl.program_id(2)
    lane = lax.broadcasted_iota(jnp.int32, (BLOCK, LANES), 1)
    low_half = lane < HEAD_DIM
    ones_cols = jnp.ones((2 * BLOCK, LANES), BF16)
    for blk in range(qb_blocks):
        rows = slice(blk * BLOCK, (blk + 1) * BLOCK)
        if blk == 0:
            variant = jnp.where(j == 0, 1, 0)
        else:
            variant = 0
        lse_tile = jnp.zeros((BLOCK, LANES), F32)
        for p in range(N_HEADS_B // 2):
            lanes = slice(p * LANES, (p + 1) * LANES)
            if blk == 0:
                kband = jnp.concatenate([kp_ref[0, :, lanes], kc_ref[0, 0:BLOCK, lanes]], axis=0)
                vband = jnp.concatenate([vp_ref[0, :, lanes], vc_ref[0, 0:BLOCK, lanes]], axis=0)
            else:
                band = slice((blk - 1) * BLOCK, (blk + 1) * BLOCK)
                kband = kc_ref[0, band, lanes]
                vband = vc_ref[0, band, lanes]
            vones = jnp.concatenate([vband, ones_cols], axis=1)
            qp = q_ref[0, rows, lanes]
            zero = jnp.zeros_like(qp)
            outs = []
            for e in range(2):
                h = 2 * p + e
                qz = jnp.where(low_half, qp, zero) if e == 0 else jnp.where(low_half, zero, qp)
                logits = _dot_nt(qz, kband).astype(BF16) + bias_ref[variant, h]
                m = jnp.max(logits, axis=1, keepdims=True)
                pr = jnp.exp2(logits - m)
                nd = _dot(pr, vones)
                den = nd[:, LANES:]
                outs.append(nd[:, :LANES] / den)
                lse_tile = lse_tile + jnp.where(lane == h, m.astype(F32) + jnp.log2(den), 0.0)
            o_ref[0, rows, lanes] = jnp.where(low_half, outs[0], outs[1]).astype(o_ref.dtype)
        lse_ref[0, rows, :] = lse_tile


def _dilated_pattern(qv, kv, vv, bias_p, dil, qb_blocks):
    b, n_sub, _ = qv.shape
    tu = qb_blocks * BLOCK
    nj = n_sub // tu
    cur = pl.BlockSpec((1, tu, D_B), lambda bi, r, j: (bi, j, r))
    prev = pl.BlockSpec((1, BLOCK, D_B), lambda bi, r, j: (bi, jnp.maximum(j * qb_blocks - 1, 0), r))
    return pl.pallas_call(
        functools.partial(_dil_kernel, qb_blocks),
        grid=(b, dil, nj),
        in_specs=[cur, cur, prev, cur, prev,
                  pl.BlockSpec((2, N_HEADS_B, BLOCK, 2 * BLOCK), lambda bi, r, j: (0, 0, 0, 0),
                               pipeline_mode=pl.Buffered(1))],
        out_specs=(pl.BlockSpec((1, tu, D_B), lambda bi, r, j: (bi, j, r)),
                   pl.BlockSpec((1, tu, LANES), lambda bi, r, j: (bi, j, r))),
        out_shape=(jax.ShapeDtypeStruct((b, n_sub, dil * D_B), BF16),
                   jax.ShapeDtypeStruct((b, n_sub, dil * LANES), F32)),
        compiler_params=pltpu.CompilerParams(dimension_semantics=("parallel", "parallel", "arbitrary"),
                                             vmem_limit_bytes=VMEM_LIMIT),
        name=f"dilated_d{dil}",
    )(qv, kv, kv, vv, vv, bias_p)


def _out_kernel(dils, x_ref, a_ref, gb_ref, w_ref, e_ref, *refs):
    npat = len(dils)
    o_refs = refs[0:npat]
    lse_refs = refs[npat:2 * npat]
    out_ref = refs[2 * npat]
    oslab_ref, lslab_ref = refs[2 * npat + 1:2 * npat + 3]
    tm = x_ref.shape[0]

    def natural_lse(j, lse_ref, d):
        if d == 1:
            return lse_ref[...]
        for r in range(d):
            lslab_ref[j, pl.ds(r, tm // d, stride=d), :] = lse_ref[:, r * LANES:(r + 1) * LANES]
        return lslab_ref[j]

    def natural_o(o_ref, d):
        if d == 1:
            return o_ref[...].astype(F32)
        for r in range(d):
            for pr in range(D_B // LANES):
                lane0 = r * D_B + pr * LANES
                oslab_ref[pr, pl.ds(r, tm // d, stride=d), :] = o_ref[:, lane0:lane0 + LANES].astype(F32)
        return jnp.concatenate([oslab_ref[pr] for pr in range(D_B // LANES)], axis=1)

    lses = [natural_lse(j, lse_ref, d) for j, (lse_ref, d) in enumerate(zip(lse_refs, dils))]
    top = functools.reduce(jnp.maximum, lses)
    ws = [jnp.exp2(l - top) for l in lses]
    tot = functools.reduce(lambda a, c: a + c, ws)
    e = e_ref[...]
    bmix = jnp.zeros((tm, D_B), F32)
    for wgt, o_ref, d in zip(ws, o_refs, dils):
        wn = wgt / tot
        hi = wn.astype(BF16)
        lo = (wn - hi.astype(F32)).astype(BF16)
        bmix = bmix + (_dot(hi, e) + _dot(lo, e)) * natural_o(o_ref, d)
    bg = (bmix * gb_ref[...].astype(F32)).astype(BF16)
    out_ref[...] = x_ref[...] + _dot(a_ref[...], w_ref[0:D_A, :]) + _dot(bg, w_ref[D_A:D_A + D_B, :])


def _out_proj(x2, a_g, gb, w_out, expand, os_, lses, dils, tm):
    n, d = x2.shape
    row = lambda i: (i, 0)
    const = lambda i: (0, 0)
    one_buf = dict(pipeline_mode=pl.Buffered(1))
    in_specs = ([pl.BlockSpec((tm, d), row), pl.BlockSpec((tm, D_A), row), pl.BlockSpec((tm, D_B), row),
                 pl.BlockSpec((D_A + D_B, d), const, **one_buf),
                 pl.BlockSpec((LANES, D_B), const, **one_buf)]
                + [pl.BlockSpec((tm // dl, dl * D_B), row) for dl in dils]
                + [pl.BlockSpec((tm // dl, dl * LANES), row) for dl in dils])
    return pl.pallas_call(
        functools.partial(_out_kernel, dils),
        grid=(n // tm,),
        in_specs=in_specs,
        out_specs=pl.BlockSpec((tm, d), row),
        out_shape=jax.ShapeDtypeStruct((n, d), F32),
        scratch_shapes=[pltpu.VMEM((D_B // LANES, tm, LANES), F32), pltpu.VMEM((len(dils), tm, LANES), F32)],
        compiler_params=pltpu.CompilerParams(dimension_semantics=("parallel",),
                                             vmem_limit_bytes=VMEM_LIMIT),
        name="merge_outproj",
    )(x2, a_g, gb, w_out, expand, *os_, *lses)


def _pack_w_in(w):
    qa, ka, va, za = (w[:, j * D_A:(j + 1) * D_A] for j in range(4))
    o = 4 * D_A
    qi = w[:, o:o + D_QI]
    ki = w[:, o + D_QI:o + D_QI + IDX_DIM]
    wi = w[:, o + D_QI + IDX_DIM:o + D_QI + IDX_DIM + N_IDX_HEADS]
    wb = w[:, o + D_QI + IDX_DIM + N_IDX_HEADS:]
    wn = jnp.concatenate([ka, za, ki, ki, wb], axis=1).astype(BF16)
    wi = jnp.pad(wi, ((0, 0), (0, WI_ROWS - N_IDX_HEADS)))
    wt = jnp.concatenate([qa, va, qi, wi], axis=1).T.astype(BF16)
    return wn, wt


def _bias_lookup(table, bucket):
    onehot = jax.nn.one_hot(bucket, N_BUCKETS, dtype=F32)
    return jnp.einsum('...n,nh->...h', onehot, table.astype(F32), precision=lax.Precision.HIGHEST)


def _dsa_bias_tiles(bias_a, tq, tk):
    k = jnp.arange(tk)[:, None]
    q = jnp.arange(tq)[None, :]
    far = bias_a[N_BUCKETS - 1].astype(F32)
    tiles = [_bias_lookup(bias_a, _rel_bucket(q - k + tk * (1 - v))) - far for v in range(1 + tq // tk)]
    return (jnp.transpose(jnp.stack(tiles), (0, 3, 1, 2)) * LOG2E).astype(BF16)


def _dil_bias_tiles(bias_b, window, dil):
    steps = window // dil
    a = jnp.arange(BLOCK)[:, None]
    c = jnp.arange(2 * BLOCK)[None, :]
    diff = a - c + BLOCK
    band = (diff >= 0) & (diff <= steps)
    bias = jnp.transpose(_bias_lookup(bias_b, _rel_bucket(diff * dil)), (2, 0, 1)) * LOG2E
    normal = jnp.where(band[None], bias, NEG_BIG)
    first = jnp.where((band & (c >= BLOCK))[None], bias, NEG_BIG)
    return jnp.stack([normal, first]).astype(BF16)


def kernel(x, norm_gain, w_in, w_out, rel_bias, q_norm_a, k_norm_a, q_norm_b, k_norm_b):
    b, s, d = x.shape
    n = b * s
    depth = norm_gain.shape[0]
    tq = min(DSA_TQ, s)
    tk = min(DSA_TK, s)
    tm = min(512, s)
    bias_a = rel_bias[:, :N_HEADS_A]
    bias_b = rel_bias[:, N_HEADS_A:]
    eye_blocks = (jnp.arange(D_A)[:, None] // HEAD_DIM == jnp.arange(D_A)[None, :] // HEAD_DIM).astype(BF16)
    expand = (jnp.arange(LANES)[:, None] == jnp.arange(D_B)[None, :] // HEAD_DIM).astype(BF16)
    assert tk >= MAX_DISTANCE and tq % tk == 0
    dsa_bias = _dsa_bias_tiles(bias_a, tq, tk)
    dil_bias = [_dil_bias_tiles(bias_b, wnd, dil) for wnd, dil in DIL_PATTERNS]
    dils = tuple(dil for _, dil in DIL_PATTERNS)

    for layer in range(depth):
        gn = jnp.stack([jnp.tile(g[layer], N_HEADS_A) for g in (q_norm_a, k_norm_a, q_norm_b, k_norm_b)])
        wn, wt = _pack_w_in(w_in[layer])
        npat = len(dils)
        outs = _inproj(x, norm_gain[layer][None, :], wn, wt, eye_blocks, gn, gn[0][:, None], tm, tk, dils)
        qat, ka, vat, ga, qit, ki, wit, gb = outs[:8]
        qviews, kviews, vviews = (outs[8 + j * npat:8 + (j + 1) * npat] for j in range(3))
        a_g = _dsa(qat, ka, vat, qit, ki, wit, ga, dsa_bias, tq, tk)

        os_, lses = [], []
        for j, ((wnd, dil), bias_p) in enumerate(zip(DIL_PATTERNS, dil_bias)):
            assert wnd // dil == BLOCK and s % (dil * BLOCK) == 0 and tm % (dil * 2 * SUBLANES) == 0
            n_sub = s // dil
            qb_blocks = min(4, n_sub // BLOCK)
            o, lse = _dilated_pattern(qviews[j], kviews[j], vviews[j], bias_p, dil, qb_blocks)
            os_.append(o.reshape(n // dil, dil * D_B))
            lses.append(lse.reshape(n // dil, dil * LANES))
        x = _out_proj(x.reshape(n, d), a_g.reshape(n, D_A), gb.reshape(n, D_B), w_out[layer].astype(BF16),
                      expand, os_, lses, dils, tm).reshape(b, s, d)
    return x
```

```python
import functools
import math

import jax
import jax.numpy as jnp
from jax import lax
from jax.experimental import pallas as pl
from jax.experimental.pallas import tpu as pltpu

HEAD_DIM = 64
N_HEADS_A = 8
N_HEADS_B = 8
D_A = N_HEADS_A * HEAD_DIM
D_B = N_HEADS_B * HEAD_DIM
N_IDX_HEADS = 4
IDX_DIM = 64
D_QI = N_IDX_HEADS * IDX_DIM
TOPK_MAX = 256
DIL_PATTERNS = ((128, 1), (512, 4), (2048, 16))
BLOCK = 128
N_BUCKETS = 32
MAX_DISTANCE = 128
EPS = 1e-6

LANES = 128
SUBLANES = 8
WI_ROWS = 16
NEG_BIG = -(2.0 ** 100)
LOG2E = math.log2(math.e)
DSA_TQ = 256
DSA_TK = 256
FAR_UNROLL = 4
LOW_BITS_CHECK_AT = 26
VMEM_LIMIT = 56 * 1024 * 1024

F32 = jnp.float32
BF16 = jnp.bfloat16
NT_DIMS = (((1,), (1,)), ((), ()))


def _dot(a, b):
    return jnp.dot(a, b, preferred_element_type=F32)


def _dot_nt(a, b):
    return lax.dot_general(a, b, NT_DIMS, preferred_element_type=F32)


def _rel_bucket(dist):
    max_exact = N_BUCKETS // 2
    d = jnp.maximum(dist, 0)
    df = jnp.maximum(d, 1).astype(F32)
    large = max_exact + (jnp.log(df / max_exact) / math.log(MAX_DISTANCE / max_exact)
                         * (N_BUCKETS - max_exact)).astype(jnp.int32)
    large = jnp.minimum(large, N_BUCKETS - 1)
    return jnp.where(d < max_exact, d, large)


def _inproj_kernel(tk, dils, x_ref, g_ref, wn_ref, wt_ref, bd_ref, gn_ref, gcol_ref,
                   qat_ref, ka_ref, vat_ref, ga_ref, qit_ref, ki_ref, wit_ref, gb_ref, *rest):
    n_b = 3 * len(dils)
    b_refs = [rest[j * len(dils):(j + 1) * len(dils)] for j in range(3)]
    slab_refs = rest[n_b:]
    x = x_ref[0]
    tm = x.shape[0]
    r = lax.rsqrt(jnp.mean(x * x, axis=-1, keepdims=True) + EPS)
    xn = (x * r * g_ref[...]).astype(BF16)
    bd = bd_ref[...]

    def head_norm(q, gain, scale):
        ss = _dot((q * q).astype(BF16), bd)
        return q * lax.rsqrt(ss * (1.0 / HEAD_DIM) + EPS) * (gain * scale)

    def silu(z):
        return z * jax.nn.sigmoid(z)

    c0 = 2 * D_A + LANES
    p = _dot(xn, wn_ref[:, c0:c0 + 4 * D_B])
    gb_ref[0] = silu(p[:, 3 * D_B:4 * D_B]).astype(BF16)

    def emit_views(t, val, refs):
        assert dils[0] == 1
        refs[0][0] = val.astype(BF16)
        if len(dils) == 1:
            return
        npair = D_B // LANES
        for pr in range(npair):
            slab_refs[0][t, pr, 0] = val[:, pr * LANES:(pr + 1) * LANES]
        for lvl in range(1, len(dils)):
            d, d_prev = dils[lvl], dils[lvl - 1]
            f = d // d_prev
            for r in range(d):
                for pr in range(npair):
                    blk = slab_refs[lvl - 1][t, pr, r % d_prev, pl.ds(r // d_prev, tm // d, stride=f), :]
                    lane0 = r * D_B + pr * LANES
                    refs[lvl][0, :, lane0:lane0 + LANES] = blk.astype(BF16)
                    if lvl + 1 < len(dils):
                        slab_refs[lvl][t, pr, r] = blk

    emit_views(0, head_norm(p[:, 0:D_B], gn_ref[2:3, :], HEAD_DIM ** -0.5 * LOG2E), b_refs[0])
    emit_views(1, head_norm(p[:, D_B:2 * D_B], gn_ref[3:4, :], 1.0), b_refs[1])
    emit_views(2, p[:, 2 * D_B:3 * D_B], b_refs[2])

    p = _dot(xn, wn_ref[:, 0:c0])
    ka_ref[0] = head_norm(p[:, 0:D_A], gn_ref[1:2, :], 1.0).astype(BF16)
    ga_ref[0] = silu(p[:, D_A:2 * D_A]).astype(BF16)
    ki_ref[0] = p[:, 2 * D_A:2 * D_A + LANES].astype(BF16)

    qt = _dot_nt(wt_ref[0:D_A, :], xn)
    ss = _dot(bd, (qt * qt).astype(BF16))
    qt = qt * lax.rsqrt(ss * (1.0 / HEAD_DIM) + EPS) * (gcol_ref[...] * (HEAD_DIM ** -0.5 * LOG2E))
    qat_ref[0] = qt.astype(BF16)
    vt = _dot_nt(wt_ref[D_A:2 * D_A, :], xn)
    for cc in range(tm // tk):
        vat_ref[0, cc] = vt[:, cc * tk:(cc + 1) * tk].astype(BF16)
    qit_ref[0] = _dot_nt(wt_ref[2 * D_A:2 * D_A + D_QI, :], xn).astype(BF16)
    wit_ref[0] = _dot_nt(wt_ref[2 * D_A + D_QI:2 * D_A + D_QI + WI_ROWS, :], xn)


def _inproj(x, g, wn, wt, bd, gn, gcol, tm, tk, dils):
    b, s, d = x.shape
    const = lambda bi, j: (0, 0)
    tok = lambda c: pl.BlockSpec((1, tm, c), lambda bi, j: (bi, j, 0))
    feat = lambda c: pl.BlockSpec((1, c, tm), lambda bi, j: (bi, 0, j))
    one_buf = dict(pipeline_mode=pl.Buffered(1))
    sds = jax.ShapeDtypeStruct
    view_shapes = tuple(sds((b, s // dl, dl * D_B), BF16) for dl in dils)
    view_specs = tuple(pl.BlockSpec((1, tm // dl, dl * D_B), lambda bi, j: (bi, j, 0)) for dl in dils)
    out_shapes = (sds((b, D_A, s), BF16), sds((b, s, D_A), BF16), sds((b, s // tk, D_A, tk), BF16),
                  sds((b, s, D_A), BF16), sds((b, D_QI, s), BF16), sds((b, s, LANES), BF16),
                  sds((b, WI_ROWS, s), F32), sds((b, s, D_B), BF16)) + view_shapes * 3
    out_specs = (feat(D_A), tok(D_A),
                 pl.BlockSpec((1, tm // tk, D_A, tk), lambda bi, j: (bi, j, 0, 0)),
                 tok(D_A), feat(D_QI), tok(LANES), feat(WI_ROWS), tok(D_B)) + view_specs * 3
    return pl.pallas_call(
        functools.partial(_inproj_kernel, tk, dils),
        grid=(b, s // tm),
        scratch_shapes=[pltpu.VMEM((3, D_B // LANES, dl, tm // dl, LANES), F32) for dl in dils[:-1]],
        in_specs=[tok(d),
                  pl.BlockSpec((1, d), const, **one_buf),
                  pl.BlockSpec(wn.shape, const, **one_buf),
                  pl.BlockSpec(wt.shape, const, **one_buf),
                  pl.BlockSpec((D_A, D_A), const, **one_buf),
                  pl.BlockSpec((4, D_A), const, **one_buf),
                  pl.BlockSpec((D_A, 1), const, **one_buf)],
        out_specs=out_specs,
        out_shape=out_shapes,
        compiler_params=pltpu.CompilerParams(dimension_semantics=("parallel", "parallel"),
                                             vmem_limit_bytes=VMEM_LIMIT),
        name="inproj",
    )(x, g, wn, wt, bd, gn, gcol)


def _ordered_bits_to_float(u):
    key = u ^ jnp.int32(-2 ** 31)
    bits = key ^ ((key >> 31) & jnp.int32(0x7FFFFFFF))
    return lax.bitcast_convert_type(bits, F32)


def _fold_rows(x, group):
    parts = [None] * 4
    for r in range(x.shape[0] // group):
        blk = x[r * group:(r + 1) * group, :]
        parts[r % 4] = blk if parts[r % 4] is None else parts[r % 4] + blk
    return (parts[0] + parts[1]) + (parts[2] + parts[3])


def _chunk_loop(n, body, init, unroll):
    def group(j, carry):
        for u in range(unroll):
            carry = body(unroll * j + u, carry)
        return carry
    carry = lax.fori_loop(0, n // unroll, group, init)
    return lax.fori_loop((n // unroll) * unroll, n, body, carry)


def _truncate_to_bf16(x):
    bits = lax.bitcast_convert_type(x, jnp.int32) & jnp.int32(-2 ** 16)
    return lax.bitcast_convert_type(bits, F32).astype(BF16)


def _dsa_kernel(topk, tq, tk, nkc, qat_ref, ka_ref, vat_ref, qit_ref, ki_ref, wit_ref, ga_ref, bias_ref,
                out_ref, sc_ref, sc16_ref, madd_ref, qz_ref, m_ref, l_ref, acc_ref, cur_ref):
    i = pl.program_id(1)
    ratio = tq // tk
    nchunks = (i + 1) * ratio
    idx_scale = (N_IDX_HEADS * IDX_DIM) ** -0.5
    top_half = lax.broadcasted_iota(jnp.int32, (LANES, tq), 0) < HEAD_DIM

    def pad_heads(ref, npairs):
        out = []
        for p in range(npairs):
            blk = ref[0, p * LANES:(p + 1) * LANES, :]
            zero = jnp.zeros_like(blk)
            out.append(jnp.where(top_half, blk, zero))
            out.append(jnp.where(top_half, zero, blk))
        return out

    qiz = pad_heads(qit_ref, N_IDX_HEADS // 2)
    w = wit_ref[0]
    krow = lax.broadcasted_iota(jnp.int32, (tk, tq), 0)
    qcol_g = i * tq + lax.broadcasted_iota(jnp.int32, (tk, tq), 1)

    def score_chunk(c, carry):
        kc = ki_ref[0, pl.ds(pl.multiple_of(c * tk, tk), tk), :]
        acc = jnp.zeros((tk, tq), F32)
        for h in range(N_IDX_HEADS):
            acc = acc + jnp.maximum(_dot(kc, qiz[h]), 0.0) * w[h:h + 1, :]
        s = acc * idx_scale + 0.0
        s = jnp.where(krow + c * tk <= qcol_g, s, -jnp.inf)
        sc_ref[c] = s
        sc16_ref[c] = _truncate_to_bf16(s)
        return carry

    _chunk_loop(nchunks, score_chunk, 0, 2)

    def count_pass(pred):
        def body(c, acc):
            return acc + _fold_rows(jnp.where(pred(sc_ref[c], c), 1.0, 0.0), SUBLANES)
        acc = _chunk_loop(nchunks, body, jnp.zeros((SUBLANES, tq), F32), 2)
        return jnp.sum(acc, axis=0, keepdims=True)

    def count_pass16(cand16):
        one, zero = jnp.ones((), BF16), jnp.zeros((), BF16)

        def body(c, acc):
            return acc + _fold_rows(jnp.where(sc16_ref[c] >= cand16, one, zero), 2 * SUBLANES)
        acc = _chunk_loop(nchunks, body, jnp.zeros((2 * SUBLANES, tq), BF16), 2)
        return jnp.sum(acc.astype(F32), axis=0, keepdims=True)

    def bit_step16(it, state):
        cur, c_lo = state
        cand = cur | lax.shift_left(jnp.int32(1), 31 - it)
        cnt = count_pass16(_truncate_to_bf16(_ordered_bits_to_float(cand)))
        take = cnt >= float(topk)
        return jnp.where(take, cand, cur), jnp.where(take, cnt, c_lo)

    assert nkc * tk <= 256 * 2 * SUBLANES
    n_keys = (nchunks * tk).astype(F32)
    cur, c_lo = lax.fori_loop(0, 16, bit_step16,
                              (jnp.zeros((1, tq), jnp.int32), jnp.zeros((1, tq), F32) + n_keys))

    q_pos = i * tq + lax.broadcasted_iota(jnp.int32, (1, tq), 1)
    few_keys = q_pos < topk
    zero_thr = (cur == jnp.int32(-2 ** 31)) & (count_pass(lambda s, c: s > 0.0) < float(topk))
    settled_early = few_keys | zero_thr

    def bit_step(it, state):
        cur, c_lo = state
        cand = cur | lax.shift_left(jnp.int32(1), 31 - it)
        candf = _ordered_bits_to_float(cand)
        cnt = count_pass(lambda s, c: s >= candf)
        take = cnt >= float(topk)
        return jnp.where(take, cand, cur), jnp.where(take, cnt, c_lo)

    cur, c_lo = lax.fori_loop(16, LOW_BITS_CHECK_AT, bit_step, (cur, c_lo))
    cur_ref[...] = cur

    @pl.when(jnp.max(jnp.where(settled_early | (c_lo == float(topk)), 0, 1)) > 0)
    def _():
        cur_ref[...] = lax.fori_loop(LOW_BITS_CHECK_AT, 32, bit_step, (cur, c_lo))[0]

    cur = cur_ref[...]
    short = (cur & jnp.int32(-2 ** 23)) == 0
    thr = jnp.where(short, -jnp.inf, _ordered_bits_to_float(cur))
    cnt_gt = count_pass(lambda s, c: s > thr)
    n_tie = jnp.where(short, 0.0, float(topk) - cnt_gt)
    below = (lax.broadcasted_iota(jnp.int32, (tk, tk), 1)
             < lax.broadcasted_iota(jnp.int32, (tk, tk), 0)).astype(BF16)
    ones_row = jnp.ones((SUBLANES, tk), BF16)

    def mask_chunk(c, seen):
        s = sc_ref[c]
        tie = s == thr
        tie01 = jnp.where(tie, 1.0, 0.0).astype(BF16)
        rank = _dot(below, tie01) + seen
        sel = (s > thr) | (tie & (rank < n_tie))
        madd_ref[c] = jnp.where(sel, 0.0, NEG_BIG).astype(BF16)
        return seen + _dot(ones_row, tie01)[0:1, :]

    _chunk_loop(nchunks, mask_chunk, jnp.zeros((1, tq), F32), 2)

    for h, qz in enumerate(pad_heads(qat_ref, N_HEADS_A // 2)):
        qz_ref[h] = qz
    m_ref[...] = jnp.full(m_ref.shape, NEG_BIG, F32)
    l_ref[...] = jnp.zeros(l_ref.shape, F32)
    acc_ref[...] = jnp.zeros(acc_ref.shape, F32)

    def attn_chunk(c, variant):
        k0 = pl.multiple_of(c * tk, tk)
        for h in range(N_HEADS_A):
            p = h // 2
            kp = ka_ref[0, pl.ds(k0, tk), p * LANES:(p + 1) * LANES]
            logits = _dot(kp, qz_ref[h]).astype(BF16) + madd_ref[c]
            if variant is not None:
                logits = logits + bias_ref[variant, h]
            hrow = slice(h, h + 1)
            m_old = m_ref[hrow, :]
            m_new = jnp.maximum(m_old, jnp.max(logits, axis=0, keepdims=True).astype(F32))
            alpha = jnp.exp2(m_old - m_new)
            pr = jnp.exp2(logits - m_new.astype(BF16))
            l_ref[hrow, :] = alpha * l_ref[hrow, :] + _dot(ones_row, pr)[0:1, :]
            m_ref[hrow, :] = m_new
            rows = slice(h * HEAD_DIM, (h + 1) * HEAD_DIM)
            acc_ref[rows, :] = alpha * acc_ref[rows, :] + _dot(vat_ref[0, c, rows, :], pr)

    first_near = i * ratio - 1
    n_far = jnp.maximum(first_near, 0)

    def far_group(j, carry):
        for u in range(FAR_UNROLL):
            attn_chunk(FAR_UNROLL * j + u, None)
        return carry

    def far_body(c, carry):
        attn_chunk(c, None)
        return carry

    n_grouped = (n_far // FAR_UNROLL) * FAR_UNROLL
    lax.fori_loop(0, n_far // FAR_UNROLL, far_group, 0)
    lax.fori_loop(n_grouped, n_far, far_body, 0)

    @pl.when(i > 0)
    def _():
        for v in range(ratio + 1):
            attn_chunk(first_near + v, v)

    @pl.when(i == 0)
    def _():
        for v in range(1, ratio + 1):
            attn_chunk(v - 1, v)

    outs = []
    for h in range(N_HEADS_A):
        rows = slice(h * HEAD_DIM, (h + 1) * HEAD_DIM)
        outs.append(acc_ref[rows, :] / l_ref[h:h + 1, :])
    o = jnp.concatenate(outs, axis=0).T
    out_ref[0] = (o * ga_ref[0].astype(F32)).astype(BF16)


def _dsa(qat, ka, vat, qit, ki, wit, ga, bias_a, tq, tk):
    b, s, _ = ka.shape
    topk = min(TOPK_MAX, s // 4)
    nq = s // tq
    nkc = s // tk
    qfeat = lambda c: pl.BlockSpec((1, c, tq), lambda bi, i: (bi, 0, i))
    qtok = lambda c: pl.BlockSpec((1, tq, c), lambda bi, i: (bi, i, 0))
    full = lambda c: pl.BlockSpec((1, s, c), lambda bi, i: (bi, 0, 0))
    return pl.pallas_call(
        functools.partial(_dsa_kernel, topk, tq, tk, nkc),
        grid=(b, nq),
        in_specs=[qfeat(D_A), full(D_A),
                  pl.BlockSpec((1, nkc, D_A, tk), lambda bi, i: (bi, 0, 0, 0)),
                  qfeat(D_QI), full(LANES), qfeat(WI_ROWS), qtok(D_A),
                  pl.BlockSpec(bias_a.shape, lambda bi, i: (0, 0, 0, 0), pipeline_mode=pl.Buffered(1))],
        out_specs=qtok(D_A),
        out_shape=jax.ShapeDtypeStruct((b, s, D_A), BF16),
        scratch_shapes=[pltpu.VMEM((nkc, tk, tq), F32),
                        pltpu.VMEM((nkc, tk, tq), BF16),
                        pltpu.VMEM((nkc, tk, tq), BF16),
                        pltpu.VMEM((N_HEADS_A, LANES, tq), BF16),
                        pltpu.VMEM((N_HEADS_A, tq), F32),
                        pltpu.VMEM((N_HEADS_A, tq), F32),
                        pltpu.VMEM((D_A, tq), F32),
                        pltpu.VMEM((1, tq), jnp.int32)],
        compiler_params=pltpu.CompilerParams(dimension_semantics=("parallel", "arbitrary"),
                                             vmem_limit_bytes=VMEM_LIMIT),
        name="dsa",
    )(qat, ka, vat, qit, ki, wit, ga, bias_a)


def _dil_kernel(qb_blocks, q_ref, kc_ref, kp_ref, vc_ref, vp_ref, bias_ref, o_ref, lse_ref):
    j = pl.program_id(2)
    lane = lax.broadcasted_iota(jnp.int32, (BLOCK, LANES), 1)
    low_half = lane < HEAD_DIM
    ones_cols = jnp.ones((2 * BLOCK, LANES), BF16)
    for blk in range(qb_blocks):
        rows = slice(blk * BLOCK, (blk + 1) * BLOCK)
        if blk == 0:
            variant = jnp.where(j == 0, 1, 0)
        else:
            variant = 0
        lse_tile = jnp.zeros((BLOCK, LANES), F32)
        for p in range(N_HEADS_B // 2):
            lanes = slice(p * LANES, (p + 1) * LANES)
            if blk == 0:
                kband = jnp.concatenate([kp_ref[0, :, lanes], kc_ref[0, 0:BLOCK, lanes]], axis=0)
                vband = jnp.concatenate([vp_ref[0, :, lanes], vc_ref[0, 0:BLOCK, lanes]], axis=0)
            else:
                band = slice((blk - 1) * BLOCK, (blk + 1) * BLOCK)
                kband = kc_ref[0, band, lanes]
                vband = vc_ref[0, band, lanes]
            vones = jnp.concatenate([vband, ones_cols], axis=1)
            qp = q_ref[0, rows, lanes]
            zero = jnp.zeros_like(qp)
            outs = []
            for e in range(2):
                h = 2 * p + e
                qz = jnp.where(low_half, qp, zero) if e == 0 else jnp.where(low_half, zero, qp)
                logits = _dot_nt(qz, kband).astype(BF16) + bias_ref[variant, h]
                m = jnp.max(logits, axis=1, keepdims=True)
                pr = jnp.exp2(logits - m)
                nd = _dot(pr, vones)
                den = nd[:, LANES:]
                outs.append(nd[:, :LANES] / den)
                lse_tile = lse_tile + jnp.where(lane == h, m.astype(F32) + jnp.log2(den), 0.0)
            o_ref[0, rows, lanes] = jnp.where(low_half, outs[0], outs[1]).astype(o_ref.dtype)
        lse_ref[0, rows, :] = lse_tile


def _dilated_pattern(qv, kv, vv, bias_p, dil, qb_blocks):
    b, n_sub, _ = qv.shape
    tu = qb_blocks * BLOCK
    nj = n_sub // tu
    cur = pl.BlockSpec((1, tu, D_B), lambda bi, r, j: (bi, j, r))
    prev = pl.BlockSpec((1, BLOCK, D_B), lambda bi, r, j: (bi, jnp.maximum(j * qb_blocks - 1, 0), r))
    return pl.pallas_call(
        functools.partial(_dil_kernel, qb_blocks),
        grid=(b, dil, nj),
        in_specs=[cur, cur, prev, cur, prev,
                  pl.BlockSpec((2, N_HEADS_B, BLOCK, 2 * BLOCK), lambda bi, r, j: (0, 0, 0, 0),
                               pipeline_mode=pl.Buffered(1))],
        out_specs=(pl.BlockSpec((1, tu, D_B), lambda bi, r, j: (bi, j, r)),
                   pl.BlockSpec((1, tu, LANES), lambda bi, r, j: (bi, j, r))),
        out_shape=(jax.ShapeDtypeStruct((b, n_sub, dil * D_B), BF16),
                   jax.ShapeDtypeStruct((b, n_sub, dil * LANES), F32)),
        compiler_params=pltpu.CompilerParams(dimension_semantics=("parallel", "parallel", "arbitrary"),
                                             vmem_limit_bytes=VMEM_LIMIT),
        name=f"dilated_d{dil}",
    )(qv, kv, kv, vv, vv, bias_p)


def _out_kernel(dils, x_ref, a_ref, gb_ref, w_ref, e_ref, *refs):
    npat = len(dils)
    o_refs = refs[0:npat]
    lse_refs = refs[npat:2 * npat]
    out_ref = refs[2 * npat]
    oslab_ref, lslab_ref = refs[2 * npat + 1:2 * npat + 3]
    tm = x_ref.shape[0]

    def natural_lse(j, lse_ref, d):
        if d == 1:
            return lse_ref[...]
        for r in range(d):
            lslab_ref[j, pl.ds(r, tm // d, stride=d), :] = lse_ref[:, r * LANES:(r + 1) * LANES]
        return lslab_ref[j]

    def natural_o(o_ref, d):
        if d == 1:
            return o_ref[...].astype(F32)
        for r in range(d):
            for pr in range(D_B // LANES):
                lane0 = r * D_B + pr * LANES
                oslab_ref[pr, pl.ds(r, tm // d, stride=d), :] = o_ref[:, lane0:lane0 + LANES].astype(F32)
        return jnp.concatenate([oslab_ref[pr] for pr in range(D_B // LANES)], axis=1)

    lses = [natural_lse(j, lse_ref, d) for j, (lse_ref, d) in enumerate(zip(lse_refs, dils))]
    top = functools.reduce(jnp.maximum, lses)
    ws = [jnp.exp2(l - top) for l in lses]
    tot = functools.reduce(lambda a, c: a + c, ws)
    e = e_ref[...]
    bmix = jnp.zeros((tm, D_B), F32)
    for wgt, o_ref, d in zip(ws, o_refs, dils):
        wn = wgt / tot
        hi = wn.astype(BF16)
        lo = (wn - hi.astype(F32)).astype(BF16)
        bmix = bmix + (_dot(hi, e) + _dot(lo, e)) * natural_o(o_ref, d)
    bg = (bmix * gb_ref[...].astype(F32)).astype(BF16)
    out_ref[...] = x_ref[...] + _dot(a_ref[...], w_ref[0:D_A, :]) + _dot(bg, w_ref[D_A:D_A + D_B, :])


def _out_proj(x2, a_g, gb, w_out, expand, os_, lses, dils, tm):
    n, d = x2.shape
    row = lambda i: (i, 0)
    const = lambda i: (0, 0)
    one_buf = dict(pipeline_mode=pl.Buffered(1))
    in_specs = ([pl.BlockSpec((tm, d), row), pl.BlockSpec((tm, D_A), row), pl.BlockSpec((tm, D_B), row),
                 pl.BlockSpec((D_A + D_B, d), const, **one_buf),
                 pl.BlockSpec((LANES, D_B), const, **one_buf)]
                + [pl.BlockSpec((tm // dl, dl * D_B), row) for dl in dils]
                + [pl.BlockSpec((tm // dl, dl * LANES), row) for dl in dils])
    return pl.pallas_call(
        functools.partial(_out_kernel, dils),
        grid=(n // tm,),
        in_specs=in_specs,
        out_specs=pl.BlockSpec((tm, d), row),
        out_shape=jax.ShapeDtypeStruct((n, d), F32),
        scratch_shapes=[pltpu.VMEM((D_B // LANES, tm, LANES), F32), pltpu.VMEM((len(dils), tm, LANES), F32)],
        compiler_params=pltpu.CompilerParams(dimension_semantics=("parallel",),
                                             vmem_limit_bytes=VMEM_LIMIT),
        name="merge_outproj",
    )(x2, a_g, gb, w_out, expand, *os_, *lses)


def _pack_w_in(w):
    qa, ka, va, za = (w[:, j * D_A:(j + 1) * D_A] for j in range(4))
    o = 4 * D_A
    qi = w[:, o:o + D_QI]
    ki = w[:, o + D_QI:o + D_QI + IDX_DIM]
    wi = w[:, o + D_QI + IDX_DIM:o + D_QI + IDX_DIM + N_IDX_HEADS]
    wb = w[:, o + D_QI + IDX_DIM + N_IDX_HEADS:]
    wn = jnp.concatenate([ka, za, ki, ki, wb], axis=1).astype(BF16)
    wi = jnp.pad(wi, ((0, 0), (0, WI_ROWS - N_IDX_HEADS)))
    wt = jnp.concatenate([qa, va, qi, wi], axis=1).T.astype(BF16)
    return wn, wt


def _bias_lookup(table, bucket):
    onehot = jax.nn.one_hot(bucket, N_BUCKETS, dtype=F32)
    return jnp.einsum('...n,nh->...h', onehot, table.astype(F32), precision=lax.Precision.HIGHEST)


def _dsa_bias_tiles(bias_a, tq, tk):
    k = jnp.arange(tk)[:, None]
    q = jnp.arange(tq)[None, :]
    far = bias_a[N_BUCKETS - 1].astype(F32)
    tiles = [_bias_lookup(bias_a, _rel_bucket(q - k + tk * (1 - v))) - far for v in range(1 + tq // tk)]
    return (jnp.transpose(jnp.stack(tiles), (0, 3, 1, 2)) * LOG2E).astype(BF16)


def _dil_bias_tiles(bias_b, window, dil):
    steps = window // dil
    a = jnp.arange(BLOCK)[:, None]
    c = jnp.arange(2 * BLOCK)[None, :]
    diff = a - c + BLOCK
    band = (diff >= 0) & (diff <= steps)
    bias = jnp.transpose(_bias_lookup(bias_b, _rel_bucket(diff * dil)), (2, 0, 1)) * LOG2E
    normal = jnp.where(band[None], bias, NEG_BIG)
    first = jnp.where((band & (c >= BLOCK))[None], bias, NEG_BIG)
    return jnp.stack([normal, first]).astype(BF16)


def kernel(x, norm_gain, w_in, w_out, rel_bias, q_norm_a, k_norm_a, q_norm_b, k_norm_b):
    b, s, d = x.shape
    n = b * s
    depth = norm_gain.shape[0]
    tq = min(DSA_TQ, s)
    tk = min(DSA_TK, s)
    tm = min(512, s)
    bias_a = rel_bias[:, :N_HEADS_A]
    bias_b = rel_bias[:, N_HEADS_A:]
    eye_blocks = (jnp.arange(D_A)[:, None] // HEAD_DIM == jnp.arange(D_A)[None, :] // HEAD_DIM).astype(BF16)
    expand = (jnp.arange(LANES)[:, None] == jnp.arange(D_B)[None, :] // HEAD_DIM).astype(BF16)
    assert tk >= MAX_DISTANCE and tq % tk == 0
    dsa_bias = _dsa_bias_tiles(bias_a, tq, tk)
    dil_bias = [_dil_bias_tiles(bias_b, wnd, dil) for wnd, dil in DIL_PATTERNS]
    dils = tuple(dil for _, dil in DIL_PATTERNS)

    for layer in range(depth):
        gn = jnp.stack([jnp.tile(g[layer], N_HEADS_A) for g in (q_norm_a, k_norm_a, q_norm_b, k_norm_b)])
        wn, wt = _pack_w_in(w_in[layer])
        npat = len(dils)
        outs = _inproj(x, norm_gain[layer][None, :], wn, wt, eye_blocks, gn, gn[0][:, None], tm, tk, dils)
        qat, ka, vat, ga, qit, ki, wit, gb = outs[:8]
        qviews, kviews, vviews = (outs[8 + j * npat:8 + (j + 1) * npat] for j in range(3))
        a_g = _dsa(qat, ka, vat, qit, ki, wit, ga, dsa_bias, tq, tk)

        os_, lses = [], []
        for j, ((wnd, dil), bias_p) in enumerate(zip(DIL_PATTERNS, dil_bias)):
            assert wnd // dil == BLOCK and s % (dil * BLOCK) == 0 and tm % (dil * 2 * SUBLANES) == 0
            n_sub = s // dil
            qb_blocks = min(4, n_sub // BLOCK)
            o, lse = _dilated_pattern(qviews[j], kviews[j], vviews[j], bias_p, dil, qb_blocks)
            os_.append(o.reshape(n // dil, dil * D_B))
            lses.append(lse.reshape(n // dil, dil * LANES))
        x = _out_proj(x.reshape(n, d), a_g.reshape(n, D_A), gb.reshape(n, D_B), w_out[layer].astype(BF16),
                      expand, os_, lses, dils, tm).reshape(b, s, d)
    return x
```

```python
import functools
import math

import jax
import jax.numpy as jnp
import numpy as np
from jax import lax
from jax.experimental import pallas as pl
from jax.experimental.pallas import tpu as pltpu

HEAD_DIM = 64
N_HEADS_A = 8
N_HEADS_B = 8
D_A = N_HEADS_A * HEAD_DIM
D_B = N_HEADS_B * HEAD_DIM
N_IDX_HEADS = 4
IDX_DIM = 64
D_QI = N_IDX_HEADS * IDX_DIM
TOPK_MAX = 256
DIL_PATTERNS = ((128, 1), (512, 4), (2048, 16))
BLOCK = 128
N_BUCKETS = 32
MAX_DISTANCE = 128
EPS = 1e-6

LANES = 128
SUBLANES = 8
WI_ROWS = 16
NEG_BIG = -(2.0 ** 100)
LOG2E = math.log2(math.e)
DSA_TQ = 256
DSA_TK = 256
FAR_UNROLL = 4
SCORE_UNROLL = 2
COUNT_UNROLL = 4
MASK_UNROLL = 4
VMEM_LIMIT = 56 * 1024 * 1024
INPROJ_VMEM_LIMIT = VMEM_LIMIT
INPROJ_SLAB_SETS = (3, 3)
INPROJ_TM = 512
OUT_TM = 512

F32 = jnp.float32
BF16 = jnp.bfloat16
NT_DIMS = (((1,), (1,)), ((), ()))


def _dot(a, b):
    return jnp.dot(a, b, preferred_element_type=F32)


def _dot_nt(a, b):
    return lax.dot_general(a, b, NT_DIMS, preferred_element_type=F32)


def _rel_bucket(dist):
    max_exact = N_BUCKETS // 2
    d = np.maximum(dist, 0)
    df = np.maximum(d, 1).astype(np.float32)
    large = max_exact + (np.log(df / np.float32(max_exact)) / np.float32(math.log(MAX_DISTANCE / max_exact))
                         * np.float32(N_BUCKETS - max_exact)).astype(np.int32)
    large = np.minimum(large, N_BUCKETS - 1)
    return np.where(d < max_exact, d, large).astype(np.int32)


def _inproj_kernel(tk, dils, x_ref, g_ref, wn_ref, wt_ref, bd_ref, gn_ref, gcol_ref,
                   qat_ref, ka_ref, vat_ref, ga_ref, qit_ref, ki_ref, wit_ref, gb_ref, *rest):
    n_b = 3 * len(dils)
    b_refs = [rest[j * len(dils):(j + 1) * len(dils)] for j in range(3)]
    slab_refs = rest[n_b:]
    x = x_ref[0]
    tm = x.shape[0]
    r = lax.rsqrt(jnp.mean(x * x, axis=-1, keepdims=True) + EPS)
    xn = (x * r * g_ref[...]).astype(BF16)
    bd = bd_ref[...]

    def head_norm(q, gain, scale):
        ss = _dot((q * q).astype(BF16), bd)
        return q * lax.rsqrt(ss * (1.0 / HEAD_DIM) + EPS) * (gain * scale)

    def silu(z):
        return z * jax.nn.sigmoid(z)

    c0 = 2 * D_A + LANES
    p = _dot(xn, wn_ref[:, c0:c0 + 4 * D_B])
    gb_ref[0] = silu(p[:, 3 * D_B:4 * D_B]).astype(BF16)

    def emit_views(t, val, refs):
        assert dils[0] == 1
        refs[0][0] = val.astype(BF16)
        if len(dils) == 1:
            return
        npair = D_B // LANES
        slot = [t % ref.shape[0] for ref in slab_refs]
        for pr in range(npair):
            slab_refs[0][slot[0], pr, 0] = val[:, pr * LANES:(pr + 1) * LANES]
        for lvl in range(1, len(dils)):
            d, d_prev = dils[lvl], dils[lvl - 1]
            f = d // d_prev
            for r in range(d):
                for pr in range(npair):
                    blk = slab_refs[lvl - 1][slot[lvl - 1], pr, r % d_prev, pl.ds(r // d_prev, tm // d, stride=f), :]
                    lane0 = r * D_B + pr * LANES
                    refs[lvl][0, :, lane0:lane0 + LANES] = blk.astype(BF16)
                    if lvl + 1 < len(dils):
                        slab_refs[lvl][slot[lvl], pr, r] = blk

    emit_views(0, head_norm(p[:, 0:D_B], gn_ref[2:3, :], HEAD_DIM ** -0.5 * LOG2E), b_refs[0])
    emit_views(1, head_norm(p[:, D_B:2 * D_B], gn_ref[3:4, :], 1.0), b_refs[1])
    emit_views(2, p[:, 2 * D_B:3 * D_B], b_refs[2])

    p = _dot(xn, wn_ref[:, 0:c0])
    ka_ref[0] = head_norm(p[:, 0:D_A], gn_ref[1:2, :], 1.0).astype(BF16)
    ga_ref[0] = silu(p[:, D_A:2 * D_A]).astype(BF16)
    ki_ref[0] = p[:, 2 * D_A:2 * D_A + LANES].astype(BF16)

    qt = _dot_nt(wt_ref[0:D_A, :], xn)
    ss = _dot(bd, (qt * qt).astype(BF16))
    qt = qt * lax.rsqrt(ss * (1.0 / HEAD_DIM) + EPS) * (gcol_ref[...] * (HEAD_DIM ** -0.5 * LOG2E))
    qat_ref[0] = qt.astype(BF16)
    vt = _dot_nt(wt_ref[D_A:2 * D_A, :], xn)
    for cc in range(tm // tk):
        vat_ref[0, cc] = vt[:, cc * tk:(cc + 1) * tk].astype(BF16)
    qit_ref[0] = _dot_nt(wt_ref[2 * D_A:2 * D_A + D_QI, :], xn).astype(BF16)
    wit_ref[0] = _dot_nt(wt_ref[2 * D_A + D_QI:2 * D_A + D_QI + WI_ROWS, :], xn)


def _inproj(x, g, wn, wt, bd, gn, gcol, tm, tk, dils):
    b, s, d = x.shape
    const = lambda bi, j: (0, 0)
    tok = lambda c: pl.BlockSpec((1, tm, c), lambda bi, j: (bi, j, 0))
    feat = lambda c: pl.BlockSpec((1, c, tm), lambda bi, j: (bi, 0, j))
    one_buf = dict(pipeline_mode=pl.Buffered(1))
    sds = jax.ShapeDtypeStruct
    view_shapes = tuple(sds((b, s // dl, dl * D_B), BF16) for dl in dils)
    view_specs = tuple(pl.BlockSpec((1, tm // dl, dl * D_B), lambda bi, j: (bi, j, 0)) for dl in dils)
    out_shapes = (sds((b, D_A, s), BF16), sds((b, s, D_A), BF16), sds((b, s // tk, D_A, tk), BF16),
                  sds((b, s, D_A), BF16), sds((b, D_QI, s), BF16), sds((b, s, LANES), BF16),
                  sds((b, WI_ROWS, s), F32), sds((b, s, D_B), BF16)) + view_shapes * 3
    out_specs = (feat(D_A), tok(D_A),
                 pl.BlockSpec((1, tm // tk, D_A, tk), lambda bi, j: (bi, j, 0, 0)),
                 tok(D_A), feat(D_QI), tok(LANES), feat(WI_ROWS), tok(D_B)) + view_specs * 3
    return pl.pallas_call(
        functools.partial(_inproj_kernel, tk, dils),
        grid=(b, s // tm),
        scratch_shapes=[pltpu.VMEM((sets, D_B // LANES, dl, tm // dl, LANES), F32)
                        for sets, dl in zip(INPROJ_SLAB_SETS, dils[:-1])],
        in_specs=[tok(d),
                  pl.BlockSpec((1, d), const, **one_buf),
                  pl.BlockSpec(wn.shape, const, **one_buf),
                  pl.BlockSpec(wt.shape, const, **one_buf),
                  pl.BlockSpec((D_A, D_A), const, **one_buf),
                  pl.BlockSpec((4, D_A), const, **one_buf),
                  pl.BlockSpec((D_A, 1), const, **one_buf)],
        out_specs=out_specs,
        out_shape=out_shapes,
        compiler_params=pltpu.CompilerParams(dimension_semantics=("parallel", "parallel"),
                                             vmem_limit_bytes=INPROJ_VMEM_LIMIT),
        name="inproj",
    )(x, g, wn, wt, bd, gn, gcol)


def _ordered_bits_to_float(u):
    key = u ^ jnp.int32(-2 ** 31)
    bits = key ^ ((key >> 31) & jnp.int32(0x7FFFFFFF))
    return lax.bitcast_convert_type(bits, F32)


def _fold_rows(x, group):
    parts = [None] * 4
    for r in range(x.shape[0] // group):
        blk = x[r * group:(r + 1) * group, :]
        parts[r % 4] = blk if parts[r % 4] is None else parts[r % 4] + blk
    return (parts[0] + parts[1]) + (parts[2] + parts[3])


def _chunk_loop(n, body, init, unroll):
    def group(j, carry):
        for u in range(unroll):
            carry = body(unroll * j + u, carry)
        return carry
    carry = lax.fori_loop(0, n // unroll, group, init)
    return lax.fori_loop((n // unroll) * unroll, n, body, carry)


def _prefix_bf16_bits(u):
    key = u ^ jnp.int32(-2 ** 31)
    return (key ^ ((key >> 31) & jnp.int32(0x7FFFFFFF))) & jnp.int32(-2 ** 16)


def _float_bits_to_ordered(bits):
    return (bits ^ ((bits >> 31) & jnp.int32(0x7FFFFFFF))) ^ jnp.int32(-2 ** 31)


def _dsa_kernel(topk, tq, tk, nkc, qat_ref, ka_ref, vat_ref, qit_ref, ki_ref, wit_ref, ga_ref, bias_ref,
                out_ref, sc_ref, sc16_ref, madd_ref, qz_ref, m_ref, l_ref, acc_ref):
    i = pl.program_id(1)
    ratio = tq // tk
    nchunks = (i + 1) * ratio
    idx_scale = (N_IDX_HEADS * IDX_DIM) ** -0.5
    top_half = lax.broadcasted_iota(jnp.int32, (LANES, tq), 0) < HEAD_DIM

    def pad_heads(ref, npairs):
        out = []
        for p in range(npairs):
            blk = ref[0, p * LANES:(p + 1) * LANES, :]
            zero = jnp.zeros_like(blk)
            out.append(jnp.where(top_half, blk, zero))
            out.append(jnp.where(top_half, zero, blk))
        return out

    qiz = pad_heads(qit_ref, N_IDX_HEADS // 2)
    w = wit_ref[0]
    krow = lax.broadcasted_iota(jnp.int32, (tk, tq), 0)
    qcol_g = i * tq + lax.broadcasted_iota(jnp.int32, (tk, tq), 1)

    def score_chunk(c, carry):
        kc = ki_ref[0, pl.ds(pl.multiple_of(c * tk, tk), tk), :]
        acc = jnp.zeros((tk, tq), F32)
        for h in range(N_IDX_HEADS):
            acc = acc + jnp.maximum(_dot(kc, qiz[h]), 0.0) * w[h:h + 1, :]
        s = acc * idx_scale + 0.0
        s = jnp.where(krow + c * tk <= qcol_g, s, -jnp.inf)
        sc_ref[c] = s
        sc16_ref[c] = s.astype(BF16)
        return carry

    _chunk_loop(nchunks, score_chunk, 0, SCORE_UNROLL)

    def count_pass(pred):
        def body(c, acc):
            return acc + _fold_rows(jnp.where(pred(sc_ref[c], c), 1.0, 0.0), SUBLANES)
        acc = _chunk_loop(nchunks, body, jnp.zeros((SUBLANES, tq), F32), COUNT_UNROLL)
        return jnp.sum(acc, axis=0, keepdims=True)

    def count_pass16(cand16):
        one, zero = jnp.ones((), BF16), jnp.zeros((), BF16)

        def body(c, acc):
            return acc + _fold_rows(jnp.where(sc16_ref[c] >= cand16, one, zero), 2 * SUBLANES)
        acc = _chunk_loop(nchunks, body, jnp.zeros((2 * SUBLANES, tq), BF16), COUNT_UNROLL)
        return jnp.sum(acc.astype(F32), axis=0, keepdims=True)

    def bit_step16(it, cur):
        cand = cur | lax.shift_left(jnp.int32(1), 31 - it)
        cand16 = lax.bitcast_convert_type(_prefix_bf16_bits(cand), F32).astype(BF16)
        return jnp.where(count_pass16(cand16) >= float(topk), cand, cur)

    assert nkc * tk <= 256 * 2 * SUBLANES
    cur16 = lax.fori_loop(0, 16, bit_step16, jnp.zeros((1, tq), jnp.int32))
    short = (cur16 & jnp.int32(-2 ** 23)) == 0

    key_v = _float_bits_to_ordered(_prefix_bf16_bits(cur16))
    window = (key_v - (2 ** 15 + 1), key_v + 2 ** 16)

    def bisect_step(it, bounds):
        lo, hi = bounds
        mid = lo + ((hi - lo) >> 1)
        midf = _ordered_bits_to_float(mid)
        take = count_pass(lambda s, c: s >= midf) >= float(topk)
        return jnp.where(take, mid, lo), jnp.where(take, hi, mid)

    n_bisect = (2 ** 16 + 2 ** 15 + 1).bit_length()
    cur, _ = lax.fori_loop(0, n_bisect, bisect_step, window)
    thr = jnp.where(short, -jnp.inf, _ordered_bits_to_float(cur))
    cnt_gt = count_pass(lambda s, c: s > thr)
    n_tie = jnp.where(short, 0.0, float(topk) - cnt_gt)
    below = (lax.broadcasted_iota(jnp.int32, (tk, tk), 1)
             < lax.broadcasted_iota(jnp.int32, (tk, tk), 0)).astype(BF16)
    ones_row = jnp.ones((SUBLANES, tk), BF16)

    def mask_chunk(c, seen):
        s = sc_ref[c]
        tie = s == thr
        tie01 = jnp.where(tie, 1.0, 0.0).astype(BF16)
        rank = _dot(below, tie01) + seen
        sel = (s > thr) | (tie & (rank < n_tie))
        madd_ref[c] = jnp.where(sel, 0.0, NEG_BIG).astype(BF16)
        return seen + _dot(ones_row, tie01)[0:1, :]

    _chunk_loop(nchunks, mask_chunk, jnp.zeros((1, tq), F32), MASK_UNROLL)

    for h, qz in enumerate(pad_heads(qat_ref, N_HEADS_A // 2)):
        qz_ref[h] = qz
    m_ref[...] = jnp.full(m_ref.shape, NEG_BIG, F32)
    l_ref[...] = jnp.zeros(l_ref.shape, F32)
    acc_ref[...] = jnp.zeros(acc_ref.shape, F32)

    def attn_chunk(c, variant):
        k0 = pl.multiple_of(c * tk, tk)
        for h in range(N_HEADS_A):
            p = h // 2
            kp = ka_ref[0, pl.ds(k0, tk), p * LANES:(p + 1) * LANES]
            logits = _dot(kp, qz_ref[h]).astype(BF16) + madd_ref[c]
            if variant is not None:
                logits = logits + bias_ref[variant, h]
            hrow = slice(h, h + 1)
            m_old = m_ref[hrow, :]
            m_new = jnp.maximum(m_old, jnp.max(logits, axis=0, keepdims=True).astype(F32))
            alpha = jnp.exp2(m_old - m_new)
            pr = jnp.exp2(logits - m_new.astype(BF16))
            l_ref[hrow, :] = alpha * l_ref[hrow, :] + _dot(ones_row, pr)[0:1, :]
            m_ref[hrow, :] = m_new
            rows = slice(h * HEAD_DIM, (h + 1) * HEAD_DIM)
            acc_ref[rows, :] = alpha * acc_ref[rows, :] + _dot(vat_ref[0, c, rows, :], pr)

    first_near = i * ratio - 1
    n_far = jnp.maximum(first_near, 0)

    def far_group(j, carry):
        for u in range(FAR_UNROLL):
            attn_chunk(FAR_UNROLL * j + u, None)
        return carry

    def far_body(c, carry):
        attn_chunk(c, None)
        return carry

    n_grouped = (n_far // FAR_UNROLL) * FAR_UNROLL
    lax.fori_loop(0, n_far // FAR_UNROLL, far_group, 0)
    lax.fori_loop(n_grouped, n_far, far_body, 0)

    @pl.when(i > 0)
    def _():
        for v in range(ratio + 1):
            attn_chunk(first_near + v, v)

    @pl.when(i == 0)
    def _():
        for v in range(1, ratio + 1):
            attn_chunk(v - 1, v)

    outs = []
    for h in range(N_HEADS_A):
        rows = slice(h * HEAD_DIM, (h + 1) * HEAD_DIM)
        outs.append(acc_ref[rows, :] / l_ref[h:h + 1, :])
    o = jnp.concatenate(outs, axis=0).T
    out_ref[0] = (o * ga_ref[0].astype(F32)).astype(BF16)


def _dsa(qat, ka, vat, qit, ki, wit, ga, bias_a, tq, tk):
    b, s, _ = ka.shape
    topk = min(TOPK_MAX, s // 4)
    nq = s // tq
    nkc = s // tk
    qfeat = lambda c: pl.BlockSpec((1, c, tq), lambda bi, i: (bi, 0, i))
    qtok = lambda c: pl.BlockSpec((1, tq, c), lambda bi, i: (bi, i, 0))
    full = lambda c: pl.BlockSpec((1, s, c), lambda bi, i: (bi, 0, 0))
    return pl.pallas_call(
        functools.partial(_dsa_kernel, topk, tq, tk, nkc),
        grid=(b, nq),
        in_specs=[qfeat(D_A), full(D_A),
                  pl.BlockSpec((1, nkc, D_A, tk), lambda bi, i: (bi, 0, 0, 0)),
                  qfeat(D_QI), full(LANES), qfeat(WI_ROWS), qtok(D_A),
                  pl.BlockSpec(bias_a.shape, lambda bi, i: (0, 0, 0, 0), pipeline_mode=pl.Buffered(1))],
        out_specs=qtok(D_A),
        out_shape=jax.ShapeDtypeStruct((b, s, D_A), BF16),
        scratch_shapes=[pltpu.VMEM((nkc, tk, tq), F32),
                        pltpu.VMEM((nkc, tk, tq), BF16),
                        pltpu.VMEM((nkc, tk, tq), BF16),
                        pltpu.VMEM((N_HEADS_A, LANES, tq), BF16),
                        pltpu.VMEM((N_HEADS_A, tq), F32),
                        pltpu.VMEM((N_HEADS_A, tq), F32),
                        pltpu.VMEM((D_A, tq), F32)],
        compiler_params=pltpu.CompilerParams(dimension_semantics=("parallel", "arbitrary"),
                                             vmem_limit_bytes=VMEM_LIMIT),
        name="dsa",
    )(qat, ka, vat, qit, ki, wit, ga, bias_a)


def _dil_kernel(qb_blocks, q_ref, kc_ref, kp_ref, vc_ref, vp_ref, bias_ref, o_ref, lse_ref):
    j = pl.program_id(2)
    lane = lax.broadcasted_iota(jnp.int32, (BLOCK, LANES), 1)
    low_half = lane < HEAD_DIM
    ones_cols = jnp.ones((2 * BLOCK, LANES), BF16)
    for blk in range(qb_blocks):
        rows = slice(blk * BLOCK, (blk + 1) * BLOCK)
        if blk == 0:
            variant = jnp.where(j == 0, 1, 0)
        else:
            variant = 0
        lse_tile = jnp.zeros((BLOCK, LANES), F32)
        for p in range(N_HEADS_B // 2):
            lanes = slice(p * LANES, (p + 1) * LANES)
            if blk == 0:
                kband = jnp.concatenate([kp_ref[0, :, lanes], kc_ref[0, 0:BLOCK, lanes]], axis=0)
                vband = jnp.concatenate([vp_ref[0, :, lanes], vc_ref[0, 0:BLOCK, lanes]], axis=0)
            else:
                band = slice((blk - 1) * BLOCK, (blk + 1) * BLOCK)
                kband = kc_ref[0, band, lanes]
                vband = vc_ref[0, band, lanes]
            vones = jnp.concatenate([vband, ones_cols], axis=1)
            qp = q_ref[0, rows, lanes]
            zero = jnp.zeros_like(qp)
            outs = []
            for e in range(2):
                h = 2 * p + e
                qz = jnp.where(low_half, qp, zero) if e == 0 else jnp.where(low_half, zero, qp)
                logits = _dot_nt(qz, kband).astype(BF16) + bias_ref[variant, h]
                m = jnp.max(logits, axis=1, keepdims=True)
                pr = jnp.exp2(logits - m)
                nd = _dot(pr, vones)
                den = nd[:, LANES:]
                outs.append(nd[:, :LANES] / den)
                lse_tile = lse_tile + jnp.where(lane == h, m.astype(F32) + jnp.log2(den), 0.0)
            o_ref[0, rows, lanes] = jnp.where(low_half, outs[0], outs[1]).astype(o_ref.dtype)
        lse_ref[0, rows, :] = lse_tile


def _dilated_pattern(qv, kv, vv, bias_p, dil, qb_blocks):
    b, n_sub, _ = qv.shape
    tu = qb_blocks * BLOCK
    nj = n_sub // tu
    cur = pl.BlockSpec((1, tu, D_B), lambda bi, r, j: (bi, j, r))
    prev = pl.BlockSpec((1, BLOCK, D_B), lambda bi, r, j: (bi, jnp.maximum(j * qb_blocks - 1, 0), r))
    return pl.pallas_call(
        functools.partial(_dil_kernel, qb_blocks),
        grid=(b, dil, nj),
        in_specs=[cur, cur, prev, cur, prev,
                  pl.BlockSpec((2, N_HEADS_B, BLOCK, 2 * BLOCK), lambda bi, r, j: (0, 0, 0, 0),
                               pipeline_mode=pl.Buffered(1))],
        out_specs=(pl.BlockSpec((1, tu, D_B), lambda bi, r, j: (bi, j, r)),
                   pl.BlockSpec((1, tu, LANES), lambda bi, r, j: (bi, j, r))),
        out_shape=(jax.ShapeDtypeStruct((b, n_sub, dil * D_B), BF16),
                   jax.ShapeDtypeStruct((b, n_sub, dil * LANES), F32)),
        compiler_params=pltpu.CompilerParams(dimension_semantics=("parallel", "parallel", "arbitrary"),
                                             vmem_limit_bytes=VMEM_LIMIT),
        name=f"dilated_d{dil}",
    )(qv, kv, kv, vv, vv, bias_p)


def _out_kernel(dils, x_ref, a_ref, gb_ref, w_ref, e_ref, *refs):
    npat = len(dils)
    o_refs = refs[0:npat]
    lse_refs = refs[npat:2 * npat]
    out_ref = refs[2 * npat]
    oslab_ref, lslab_ref = refs[2 * npat + 1:2 * npat + 3]
    tm = x_ref.shape[0]

    def natural_lse(j, lse_ref, d):
        if d == 1:
            return lse_ref[...]
        for r in range(d):
            lslab_ref[j, pl.ds(r, tm // d, stride=d), :] = lse_ref[:, r * LANES:(r + 1) * LANES]
        return lslab_ref[j]

    def natural_o(o_ref, d):
        if d == 1:
            return o_ref[...].astype(F32)
        for r in range(d):
            for pr in range(D_B // LANES):
                lane0 = r * D_B + pr * LANES
                oslab_ref[pr, pl.ds(r, tm // d, stride=d), :] = o_ref[:, lane0:lane0 + LANES].astype(F32)
        return jnp.concatenate([oslab_ref[pr] for pr in range(D_B // LANES)], axis=1)

    lses = [natural_lse(j, lse_ref, d) for j, (lse_ref, d) in enumerate(zip(lse_refs, dils))]
    top = functools.reduce(jnp.maximum, lses)
    ws = [jnp.exp2(l - top) for l in lses]
    tot = functools.reduce(lambda a, c: a + c, ws)
    e = e_ref[...]
    bmix = jnp.zeros((tm, D_B), F32)
    for wgt, o_ref, d in zip(ws, o_refs, dils):
        wn = wgt / tot
        hi = wn.astype(BF16)
        lo = (wn - hi.astype(F32)).astype(BF16)
        bmix = bmix + (_dot(hi, e) + _dot(lo, e)) * natural_o(o_ref, d)
    bg = (bmix * gb_ref[...].astype(F32)).astype(BF16)
    out_ref[...] = x_ref[...] + _dot(a_ref[...], w_ref[0:D_A, :]) + _dot(bg, w_ref[D_A:D_A + D_B, :])


def _out_proj(x2, a_g, gb, w_out, expand, os_, lses, dils, tm):
    n, d = x2.shape
    row = lambda i: (i, 0)
    const = lambda i: (0, 0)
    one_buf = dict(pipeline_mode=pl.Buffered(1))
    in_specs = ([pl.BlockSpec((tm, d), row), pl.BlockSpec((tm, D_A), row), pl.BlockSpec((tm, D_B), row),
                 pl.BlockSpec((D_A + D_B, d), const, **one_buf),
                 pl.BlockSpec((LANES, D_B), const, **one_buf)]
                + [pl.BlockSpec((tm // dl, dl * D_B), row) for dl in dils]
                + [pl.BlockSpec((tm // dl, dl * LANES), row) for dl in dils])
    return pl.pallas_call(
        functools.partial(_out_kernel, dils),
        grid=(n // tm,),
        in_specs=in_specs,
        out_specs=pl.BlockSpec((tm, d), row),
        out_shape=jax.ShapeDtypeStruct((n, d), F32),
        scratch_shapes=[pltpu.VMEM((D_B // LANES, tm, LANES), F32), pltpu.VMEM((len(dils), tm, LANES), F32)],
        compiler_params=pltpu.CompilerParams(dimension_semantics=("parallel",),
                                             vmem_limit_bytes=VMEM_LIMIT),
        name="merge_outproj",
    )(x2, a_g, gb, w_out, expand, *os_, *lses)


def _pack_w_in(w):
    qa, ka, va, za = (w[:, j * D_A:(j + 1) * D_A] for j in range(4))
    o = 4 * D_A
    qi = w[:, o:o + D_QI]
    ki = w[:, o + D_QI:o + D_QI + IDX_DIM]
    wi = w[:, o + D_QI + IDX_DIM:o + D_QI + IDX_DIM + N_IDX_HEADS]
    wb = w[:, o + D_QI + IDX_DIM + N_IDX_HEADS:]
    wn = jnp.concatenate([ka, za, ki, ki, wb], axis=1).astype(BF16)
    wi = jnp.pad(wi, ((0, 0), (0, WI_ROWS - N_IDX_HEADS)))
    wt = jnp.concatenate([qa, va, qi, wi], axis=1).T.astype(BF16)
    return wn, wt


def _bias_lookup(table, bucket):
    onehot = (bucket[..., None] == np.arange(N_BUCKETS)).astype(np.float32)
    return jnp.einsum('...n,nh->...h', onehot, table.astype(F32), precision=lax.Precision.HIGHEST)


def _dsa_bias_tiles(bias_a, tq, tk):
    k = np.arange(tk)[:, None]
    q = np.arange(tq)[None, :]
    far = bias_a[N_BUCKETS - 1].astype(F32)
    tiles = [_bias_lookup(bias_a, _rel_bucket(q - k + tk * (1 - v))) - far for v in range(1 + tq // tk)]
    return (jnp.transpose(jnp.stack(tiles), (0, 3, 1, 2)) * LOG2E).astype(BF16)


def _dil_bias_tiles(bias_b, window, dil):
    steps = window // dil
    a = np.arange(BLOCK)[:, None]
    c = np.arange(2 * BLOCK)[None, :]
    diff = a - c + BLOCK
    band = (diff >= 0) & (diff <= steps)
    bias = jnp.transpose(_bias_lookup(bias_b, _rel_bucket(diff * dil)), (2, 0, 1)) * LOG2E
    normal = jnp.where(band[None], bias, NEG_BIG)
    first = jnp.where((band & (c >= BLOCK))[None], bias, NEG_BIG)
    return jnp.stack([normal, first]).astype(BF16)


def kernel(x, norm_gain, w_in, w_out, rel_bias, q_norm_a, k_norm_a, q_norm_b, k_norm_b):
    b, s, d = x.shape
    n = b * s
    depth = norm_gain.shape[0]
    tq = min(DSA_TQ, s)
    tk = min(DSA_TK, s)
    tm_in = min(INPROJ_TM, s)
    tm_out = min(OUT_TM, s)
    bias_a = rel_bias[:, :N_HEADS_A]
    bias_b = rel_bias[:, N_HEADS_A:]
    eye_blocks = (jnp.arange(D_A)[:, None] // HEAD_DIM == jnp.arange(D_A)[None, :] // HEAD_DIM).astype(BF16)
    expand = (jnp.arange(LANES)[:, None] == jnp.arange(D_B)[None, :] // HEAD_DIM).astype(BF16)
    assert tk >= MAX_DISTANCE and tq % tk == 0
    dsa_bias = _dsa_bias_tiles(bias_a, tq, tk)
    dil_bias = [_dil_bias_tiles(bias_b, wnd, dil) for wnd, dil in DIL_PATTERNS]
    dils = tuple(dil for _, dil in DIL_PATTERNS)

    for layer in range(depth):
        gn = jnp.stack([jnp.tile(g[layer], N_HEADS_A) for g in (q_norm_a, k_norm_a, q_norm_b, k_norm_b)])
        wn, wt = _pack_w_in(w_in[layer])
        npat = len(dils)
        outs = _inproj(x, norm_gain[layer][None, :], wn, wt, eye_blocks, gn, gn[0][:, None], tm_in, tk, dils)
        qat, ka, vat, ga, qit, ki, wit, gb = outs[:8]
        qviews, kviews, vviews = (outs[8 + j * npat:8 + (j + 1) * npat] for j in range(3))
        a_g = _dsa(qat, ka, vat, qit, ki, wit, ga, dsa_bias, tq, tk)

        os_, lses = [], []
        for j, ((wnd, dil), bias_p) in enumerate(zip(DIL_PATTERNS, dil_bias)):
            assert wnd // dil == BLOCK and s % (dil * BLOCK) == 0 and tm_in % (dil * 2 * SUBLANES) == 0 and tm_out % (dil * SUBLANES) == 0
            n_sub = s // dil
            qb_blocks = min(4, n_sub // BLOCK)
            o, lse = _dilated_pattern(qviews[j], kviews[j], vviews[j], bias_p, dil, qb_blocks)
            os_.append(o.reshape(n // dil, dil * D_B))
            lses.append(lse.reshape(n // dil, dil * LANES))
        x = _out_proj(x.reshape(n, d), a_g.reshape(n, D_A), gb.reshape(n, D_B), w_out[layer].astype(BF16),
                      expand, os_, lses, dils, tm_out).reshape(b, s, d)
    return x
```

```python
import functools
import math

import jax
import jax.numpy as jnp
import numpy as np
from jax import lax
from jax.experimental import pallas as pl
from jax.experimental.pallas import tpu as pltpu

HEAD_DIM = 64
N_HEADS_A = 8
N_HEADS_B = 8
D_A = N_HEADS_A * HEAD_DIM
D_B = N_HEADS_B * HEAD_DIM
N_IDX_HEADS = 4
IDX_DIM = 64
D_QI = N_IDX_HEADS * IDX_DIM
TOPK_MAX = 256
DIL_PATTERNS = ((128, 1), (512, 4), (2048, 16))
BLOCK = 128
N_BUCKETS = 32
MAX_DISTANCE = 128
EPS = 1e-6

LANES = 128
SUBLANES = 8
WI_ROWS = 16
NEG_BIG = -(2.0 ** 100)
LOG2E = math.log2(math.e)
DSA_TQ = 256
DSA_TK = 256
FAR_UNROLL = 4
SCORE_UNROLL = 2
COUNT_UNROLL = 4
MASK_UNROLL = 4
VMEM_LIMIT = 56 * 1024 * 1024
INPROJ_VMEM_LIMIT = VMEM_LIMIT
INPROJ_SLAB_SETS = (3, 3)
INPROJ_TM = 512
OUT_TM = 512
DIL_BLOCKS_PER_STEP = 8

F32 = jnp.float32
BF16 = jnp.bfloat16
NT_DIMS = (((1,), (1,)), ((), ()))


def _dot(a, b):
    return jnp.dot(a, b, preferred_element_type=F32)


def _dot_nt(a, b):
    return lax.dot_general(a, b, NT_DIMS, preferred_element_type=F32)


def _rel_bucket(dist):
    max_exact = N_BUCKETS // 2
    d = np.maximum(dist, 0)
    df = np.maximum(d, 1).astype(np.float32)
    large = max_exact + (np.log(df / np.float32(max_exact)) / np.float32(math.log(MAX_DISTANCE / max_exact))
                         * np.float32(N_BUCKETS - max_exact)).astype(np.int32)
    large = np.minimum(large, N_BUCKETS - 1)
    return np.where(d < max_exact, d, large).astype(np.int32)


def _inproj_kernel(tk, dils, x_ref, g_ref, wn_ref, wt_ref, bd_ref, gn_ref, gcol_ref,
                   qat_ref, ka_ref, vat_ref, ga_ref, qit_ref, ki_ref, wit_ref, gb_ref, *rest):
    n_b = 3 * len(dils)
    b_refs = [rest[j * len(dils):(j + 1) * len(dils)] for j in range(3)]
    slab_refs = rest[n_b:]
    x = x_ref[0]
    tm = x.shape[0]
    r = lax.rsqrt(jnp.mean(x * x, axis=-1, keepdims=True) + EPS)
    xn = (x * r * g_ref[...]).astype(BF16)
    bd = bd_ref[...]

    def head_norm(q, gain, scale):
        ss = _dot((q * q).astype(BF16), bd)
        return q * lax.rsqrt(ss * (1.0 / HEAD_DIM) + EPS) * (gain * scale)

    def silu(z):
        return z * jax.nn.sigmoid(z)

    c0 = 2 * D_A + LANES
    p = _dot(xn, wn_ref[:, c0:c0 + 4 * D_B])
    gb_ref[0] = silu(p[:, 3 * D_B:4 * D_B]).astype(BF16)

    def emit_views(t, val, refs):
        assert dils[0] == 1
        refs[0][0] = val.astype(BF16)
        if len(dils) == 1:
            return
        npair = D_B // LANES
        slot = [t % ref.shape[0] for ref in slab_refs]
        for pr in range(npair):
            slab_refs[0][slot[0], pr, 0] = val[:, pr * LANES:(pr + 1) * LANES]
        for lvl in range(1, len(dils)):
            d, d_prev = dils[lvl], dils[lvl - 1]
            f = d // d_prev
            for r in range(d):
                for pr in range(npair):
                    blk = slab_refs[lvl - 1][slot[lvl - 1], pr, r % d_prev, pl.ds(r // d_prev, tm // d, stride=f), :]
                    lane0 = r * D_B + pr * LANES
                    refs[lvl][0, :, lane0:lane0 + LANES] = blk.astype(BF16)
                    if lvl + 1 < len(dils):
                        slab_refs[lvl][slot[lvl], pr, r] = blk

    emit_views(0, head_norm(p[:, 0:D_B], gn_ref[2:3, :], HEAD_DIM ** -0.5 * LOG2E), b_refs[0])
    emit_views(1, head_norm(p[:, D_B:2 * D_B], gn_ref[3:4, :], 1.0), b_refs[1])
    emit_views(2, p[:, 2 * D_B:3 * D_B], b_refs[2])

    p = _dot(xn, wn_ref[:, 0:c0])
    ka_ref[0] = head_norm(p[:, 0:D_A], gn_ref[1:2, :], 1.0).astype(BF16)
    ga_ref[0] = silu(p[:, D_A:2 * D_A]).astype(BF16)
    ki_ref[0] = p[:, 2 * D_A:2 * D_A + LANES].astype(BF16)

    qt = _dot_nt(wt_ref[0:D_A, :], xn)
    ss = _dot(bd, (qt * qt).astype(BF16))
    qt = qt * lax.rsqrt(ss * (1.0 / HEAD_DIM) + EPS) * (gcol_ref[...] * (HEAD_DIM ** -0.5 * LOG2E))
    qat_ref[0] = qt.astype(BF16)
    vt = _dot_nt(wt_ref[D_A:2 * D_A, :], xn)
    for cc in range(tm // tk):
        vat_ref[0, cc] = vt[:, cc * tk:(cc + 1) * tk].astype(BF16)
    qit_ref[0] = _dot_nt(wt_ref[2 * D_A:2 * D_A + D_QI, :], xn).astype(BF16)
    wit_ref[0] = _dot_nt(wt_ref[2 * D_A + D_QI:2 * D_A + D_QI + WI_ROWS, :], xn)


def _inproj(x, g, wn, wt, bd, gn, gcol, tm, tk, dils):
    b, s, d = x.shape
    const = lambda bi, j: (0, 0)
    tok = lambda c: pl.BlockSpec((1, tm, c), lambda bi, j: (bi, j, 0))
    feat = lambda c: pl.BlockSpec((1, c, tm), lambda bi, j: (bi, 0, j))
    one_buf = dict(pipeline_mode=pl.Buffered(1))
    sds = jax.ShapeDtypeStruct
    view_shapes = tuple(sds((b, s // dl, dl * D_B), BF16) for dl in dils)
    view_specs = tuple(pl.BlockSpec((1, tm // dl, dl * D_B), lambda bi, j: (bi, j, 0)) for dl in dils)
    out_shapes = (sds((b, D_A, s), BF16), sds((b, s, D_A), BF16), sds((b, s // tk, D_A, tk), BF16),
                  sds((b, s, D_A), BF16), sds((b, D_QI, s), BF16), sds((b, s, LANES), BF16),
                  sds((b, WI_ROWS, s), F32), sds((b, s, D_B), BF16)) + view_shapes * 3
    out_specs = (feat(D_A), tok(D_A),
                 pl.BlockSpec((1, tm // tk, D_A, tk), lambda bi, j: (bi, j, 0, 0)),
                 tok(D_A), feat(D_QI), tok(LANES), feat(WI_ROWS), tok(D_B)) + view_specs * 3
    return pl.pallas_call(
        functools.partial(_inproj_kernel, tk, dils),
        grid=(b, s // tm),
        scratch_shapes=[pltpu.VMEM((sets, D_B // LANES, dl, tm // dl, LANES), F32)
                        for sets, dl in zip(INPROJ_SLAB_SETS, dils[:-1])],
        in_specs=[tok(d),
                  pl.BlockSpec((1, d), const, **one_buf),
                  pl.BlockSpec(wn.shape, const, **one_buf),
                  pl.BlockSpec(wt.shape, const, **one_buf),
                  pl.BlockSpec((D_A, D_A), const, **one_buf),
                  pl.BlockSpec((4, D_A), const, **one_buf),
                  pl.BlockSpec((D_A, 1), const, **one_buf)],
        out_specs=out_specs,
        out_shape=out_shapes,
        compiler_params=pltpu.CompilerParams(dimension_semantics=("parallel", "parallel"),
                                             vmem_limit_bytes=INPROJ_VMEM_LIMIT),
        name="inproj",
    )(x, g, wn, wt, bd, gn, gcol)


def _ordered_bits_to_float(u):
    key = u ^ jnp.int32(-2 ** 31)
    bits = key ^ ((key >> 31) & jnp.int32(0x7FFFFFFF))
    return lax.bitcast_convert_type(bits, F32)


def _fold_rows(x, group):
    parts = [None] * 4
    for r in range(x.shape[0] // group):
        blk = x[r * group:(r + 1) * group, :]
        parts[r % 4] = blk if parts[r % 4] is None else parts[r % 4] + blk
    return (parts[0] + parts[1]) + (parts[2] + parts[3])


def _chunk_loop(n, body, init, unroll):
    def group(j, carry):
        for u in range(unroll):
            carry = body(unroll * j + u, carry)
        return carry
    carry = lax.fori_loop(0, n // unroll, group, init)
    return lax.fori_loop((n // unroll) * unroll, n, body, carry)


def _prefix_bf16_bits(u):
    key = u ^ jnp.int32(-2 ** 31)
    return (key ^ ((key >> 31) & jnp.int32(0x7FFFFFFF))) & jnp.int32(-2 ** 16)


def _float_bits_to_ordered(bits):
    return (bits ^ ((bits >> 31) & jnp.int32(0x7FFFFFFF))) ^ jnp.int32(-2 ** 31)


def _dsa_kernel(topk, tq, tk, nkc, qat_ref, ka_ref, vat_ref, qit_ref, ki_ref, wit_ref, ga_ref, bias_ref,
                out_ref, sc_ref, sc16_ref, madd_ref, qz_ref, m_ref, l_ref, acc_ref):
    i = pl.program_id(1)
    ratio = tq // tk
    nchunks = (i + 1) * ratio
    idx_scale = (N_IDX_HEADS * IDX_DIM) ** -0.5
    top_half = lax.broadcasted_iota(jnp.int32, (LANES, tq), 0) < HEAD_DIM

    def pad_heads(ref, npairs):
        out = []
        for p in range(npairs):
            blk = ref[0, p * LANES:(p + 1) * LANES, :]
            zero = jnp.zeros_like(blk)
            out.append(jnp.where(top_half, blk, zero))
            out.append(jnp.where(top_half, zero, blk))
        return out

    qiz = pad_heads(qit_ref, N_IDX_HEADS // 2)
    w = wit_ref[0]
    krow = lax.broadcasted_iota(jnp.int32, (tk, tq), 0)
    qcol_g = i * tq + lax.broadcasted_iota(jnp.int32, (tk, tq), 1)

    def score_chunk(c, carry):
        kc = ki_ref[0, pl.ds(pl.multiple_of(c * tk, tk), tk), :]
        acc = jnp.zeros((tk, tq), F32)
        for h in range(N_IDX_HEADS):
            acc = acc + jnp.maximum(_dot(kc, qiz[h]), 0.0) * w[h:h + 1, :]
        s = acc * idx_scale + 0.0
        s = jnp.where(krow + c * tk <= qcol_g, s, -jnp.inf)
        sc_ref[c] = s
        sc16_ref[c] = s.astype(BF16)
        return carry

    _chunk_loop(nchunks, score_chunk, 0, SCORE_UNROLL)

    def count_pass(pred):
        def body(c, acc):
            return acc + _fold_rows(jnp.where(pred(sc_ref[c], c), 1.0, 0.0), SUBLANES)
        acc = _chunk_loop(nchunks, body, jnp.zeros((SUBLANES, tq), F32), COUNT_UNROLL)
        return jnp.sum(acc, axis=0, keepdims=True)

    def count_pass16(cand16):
        one, zero = jnp.ones((), BF16), jnp.zeros((), BF16)

        def body(c, acc):
            return acc + _fold_rows(jnp.where(sc16_ref[c] >= cand16, one, zero), 2 * SUBLANES)
        acc = _chunk_loop(nchunks, body, jnp.zeros((2 * SUBLANES, tq), BF16), COUNT_UNROLL)
        return jnp.sum(acc.astype(F32), axis=0, keepdims=True)

    def bit_step16(it, cur):
        cand = cur | lax.shift_left(jnp.int32(1), 31 - it)
        cand16 = lax.bitcast_convert_type(_prefix_bf16_bits(cand), F32).astype(BF16)
        return jnp.where(count_pass16(cand16) >= float(topk), cand, cur)

    assert nkc * tk <= 256 * 2 * SUBLANES
    cur16 = lax.fori_loop(0, 16, bit_step16, jnp.zeros((1, tq), jnp.int32))
    short = (cur16 & jnp.int32(-2 ** 23)) == 0

    key_v = _float_bits_to_ordered(_prefix_bf16_bits(cur16))
    window = (key_v - (2 ** 15 + 1), key_v + 2 ** 16)

    def bisect_step(it, bounds):
        lo, hi = bounds
        mid = lo + ((hi - lo) >> 1)
        midf = _ordered_bits_to_float(mid)
        take = count_pass(lambda s, c: s >= midf) >= float(topk)
        return jnp.where(take, mid, lo), jnp.where(take, hi, mid)

    n_bisect = (2 ** 16 + 2 ** 15 + 1).bit_length()
    cur, _ = lax.fori_loop(0, n_bisect, bisect_step, window)
    thr = jnp.where(short, -jnp.inf, _ordered_bits_to_float(cur))
    cnt_gt = count_pass(lambda s, c: s > thr)
    n_tie = jnp.where(short, 0.0, float(topk) - cnt_gt)
    below = (lax.broadcasted_iota(jnp.int32, (tk, tk), 1)
             < lax.broadcasted_iota(jnp.int32, (tk, tk), 0)).astype(BF16)
    ones_row = jnp.ones((SUBLANES, tk), BF16)

    def mask_chunk(c, seen):
        s = sc_ref[c]
        tie = s == thr
        tie01 = jnp.where(tie, 1.0, 0.0).astype(BF16)
        rank = _dot(below, tie01) + seen
        sel = (s > thr) | (tie & (rank < n_tie))
        madd_ref[c] = jnp.where(sel, 0.0, NEG_BIG).astype(BF16)
        return seen + _dot(ones_row, tie01)[0:1, :]

    _chunk_loop(nchunks, mask_chunk, jnp.zeros((1, tq), F32), MASK_UNROLL)

    for h, qz in enumerate(pad_heads(qat_ref, N_HEADS_A // 2)):
        qz_ref[h] = qz
    m_ref[...] = jnp.full(m_ref.shape, NEG_BIG, F32)
    l_ref[...] = jnp.zeros(l_ref.shape, F32)
    acc_ref[...] = jnp.zeros(acc_ref.shape, F32)

    def attn_chunk(c, variant):
        k0 = pl.multiple_of(c * tk, tk)
        for h in range(N_HEADS_A):
            p = h // 2
            kp = ka_ref[0, pl.ds(k0, tk), p * LANES:(p + 1) * LANES]
            logits = _dot(kp, qz_ref[h]).astype(BF16) + madd_ref[c]
            if variant is not None:
                logits = logits + bias_ref[variant, h]
            hrow = slice(h, h + 1)
            m_old = m_ref[hrow, :]
            m_new = jnp.maximum(m_old, jnp.max(logits, axis=0, keepdims=True).astype(F32))
            alpha = jnp.exp2(m_old - m_new)
            pr = jnp.exp2(logits - m_new.astype(BF16))
            l_ref[hrow, :] = alpha * l_ref[hrow, :] + _dot(ones_row, pr)[0:1, :]
            m_ref[hrow, :] = m_new
            rows = slice(h * HEAD_DIM, (h + 1) * HEAD_DIM)
            acc_ref[rows, :] = alpha * acc_ref[rows, :] + _dot(vat_ref[0, c, rows, :], pr)

    first_near = i * ratio - 1
    n_far = jnp.maximum(first_near, 0)

    def far_group(j, carry):
        for u in range(FAR_UNROLL):
            attn_chunk(FAR_UNROLL * j + u, None)
        return carry

    def far_body(c, carry):
        attn_chunk(c, None)
        return carry

    n_grouped = (n_far // FAR_UNROLL) * FAR_UNROLL
    lax.fori_loop(0, n_far // FAR_UNROLL, far_group, 0)

    @pl.when(n_far - n_grouped >= 2)
    def _():
        attn_chunk(n_grouped, None)
        attn_chunk(n_grouped + 1, None)

    lax.fori_loop(n_grouped + 2 * ((n_far - n_grouped) // 2), n_far, far_body, 0)

    @pl.when(i > 0)
    def _():
        for v in range(ratio + 1):
            attn_chunk(first_near + v, v)

    @pl.when(i == 0)
    def _():
        for v in range(1, ratio + 1):
            attn_chunk(v - 1, v)

    outs = []
    for h in range(N_HEADS_A):
        rows = slice(h * HEAD_DIM, (h + 1) * HEAD_DIM)
        outs.append(acc_ref[rows, :] / l_ref[h:h + 1, :])
    o = jnp.concatenate(outs, axis=0).T
    out_ref[0] = (o * ga_ref[0].astype(F32)).astype(BF16)


def _dsa(qat, ka, vat, qit, ki, wit, ga, bias_a, tq, tk):
    b, s, _ = ka.shape
    topk = min(TOPK_MAX, s // 4)
    nq = s // tq
    nkc = s // tk
    qfeat = lambda c: pl.BlockSpec((1, c, tq), lambda bi, i: (bi, 0, i))
    qtok = lambda c: pl.BlockSpec((1, tq, c), lambda bi, i: (bi, i, 0))
    full = lambda c: pl.BlockSpec((1, s, c), lambda bi, i: (bi, 0, 0))
    return pl.pallas_call(
        functools.partial(_dsa_kernel, topk, tq, tk, nkc),
        grid=(b, nq),
        in_specs=[qfeat(D_A), full(D_A),
                  pl.BlockSpec((1, nkc, D_A, tk), lambda bi, i: (bi, 0, 0, 0)),
                  qfeat(D_QI), full(LANES), qfeat(WI_ROWS), qtok(D_A),
                  pl.BlockSpec(bias_a.shape, lambda bi, i: (0, 0, 0, 0), pipeline_mode=pl.Buffered(1))],
        out_specs=qtok(D_A),
        out_shape=jax.ShapeDtypeStruct((b, s, D_A), BF16),
        scratch_shapes=[pltpu.VMEM((nkc, tk, tq), F32),
                        pltpu.VMEM((nkc, tk, tq), BF16),
                        pltpu.VMEM((nkc, tk, tq), BF16),
                        pltpu.VMEM((N_HEADS_A, LANES, tq), BF16),
                        pltpu.VMEM((N_HEADS_A, tq), F32),
                        pltpu.VMEM((N_HEADS_A, tq), F32),
                        pltpu.VMEM((D_A, tq), F32)],
        compiler_params=pltpu.CompilerParams(dimension_semantics=("parallel", "arbitrary"),
                                             vmem_limit_bytes=VMEM_LIMIT),
        name="dsa",
    )(qat, ka, vat, qit, ki, wit, ga, bias_a)


def _dil_kernel(qb_blocks, nres, q_ref, kc_ref, kp_ref, vc_ref, vp_ref, bias_ref, o_ref, lse_ref):
    j = pl.program_id(2)
    lane = lax.broadcasted_iota(jnp.int32, (BLOCK, LANES), 1)
    low_half = lane < HEAD_DIM
    ones_cols = jnp.ones((2 * BLOCK, LANES), BF16)
    for rr, blk in [(rr, blk) for rr in range(nres) for blk in range(qb_blocks)]:
        rows = slice(blk * BLOCK, (blk + 1) * BLOCK)
        if blk == 0:
            variant = jnp.where(j == 0, 1, 0)
        else:
            variant = 0
        lse_tile = jnp.zeros((BLOCK, LANES), F32)
        for p in range(N_HEADS_B // 2):
            lanes = slice(rr * D_B + p * LANES, rr * D_B + (p + 1) * LANES)
            if blk == 0:
                kband = jnp.concatenate([kp_ref[0, :, lanes], kc_ref[0, 0:BLOCK, lanes]], axis=0)
                vband = jnp.concatenate([vp_ref[0, :, lanes], vc_ref[0, 0:BLOCK, lanes]], axis=0)
            else:
                band = slice((blk - 1) * BLOCK, (blk + 1) * BLOCK)
                kband = kc_ref[0, band, lanes]
                vband = vc_ref[0, band, lanes]
            vones = jnp.concatenate([vband, ones_cols], axis=1)
            qp = q_ref[0, rows, lanes]
            zero = jnp.zeros_like(qp)
            outs = []
            for e in range(2):
                h = 2 * p + e
                qz = jnp.where(low_half, qp, zero) if e == 0 else jnp.where(low_half, zero, qp)
                logits = _dot_nt(qz, kband).astype(BF16) + bias_ref[variant, h]
                m = jnp.max(logits, axis=1, keepdims=True)
                pr = jnp.exp2(logits - m)
                nd = _dot(pr, vones)
                den = nd[:, LANES:]
                outs.append(nd[:, :LANES] / den)
                lse_tile = lse_tile + jnp.where(lane == h, m.astype(F32) + jnp.log2(den), 0.0)
            o_ref[0, rows, lanes] = jnp.where(low_half, outs[0], outs[1]).astype(o_ref.dtype)
        lse_ref[0, rows, rr * LANES:(rr + 1) * LANES] = lse_tile


def _dilated_pattern(qv, kv, vv, bias_p, dil, qb_blocks, nres):
    b, n_sub, _ = qv.shape
    tu = qb_blocks * BLOCK
    nj = n_sub // tu
    cur = pl.BlockSpec((1, tu, nres * D_B), lambda bi, r, j: (bi, j, r))
    prev = pl.BlockSpec((1, BLOCK, nres * D_B), lambda bi, r, j: (bi, jnp.maximum(j * qb_blocks - 1, 0), r))
    return pl.pallas_call(
        functools.partial(_dil_kernel, qb_blocks, nres),
        grid=(b, dil // nres, nj),
        in_specs=[cur, cur, prev, cur, prev,
                  pl.BlockSpec((2, N_HEADS_B, BLOCK, 2 * BLOCK), lambda bi, r, j: (0, 0, 0, 0),
                               pipeline_mode=pl.Buffered(1))],
        out_specs=(pl.BlockSpec((1, tu, nres * D_B), lambda bi, r, j: (bi, j, r)),
                   pl.BlockSpec((1, tu, nres * LANES), lambda bi, r, j: (bi, j, r))),
        out_shape=(jax.ShapeDtypeStruct((b, n_sub, dil * D_B), BF16),
                   jax.ShapeDtypeStruct((b, n_sub, dil * LANES), F32)),
        compiler_params=pltpu.CompilerParams(dimension_semantics=("parallel", "parallel", "arbitrary"),
                                             vmem_limit_bytes=VMEM_LIMIT),
        name=f"dilated_d{dil}",
    )(qv, kv, kv, vv, vv, bias_p)


def _out_kernel(dils, x_ref, a_ref, gb_ref, w_ref, e_ref, *refs):
    npat = len(dils)
    o_refs = refs[0:npat]
    lse_refs = refs[npat:2 * npat]
    out_ref = refs[2 * npat]
    oslab_ref, lslab_ref = refs[2 * npat + 1:2 * npat + 3]
    tm = x_ref.shape[0]

    def natural_lse(j, lse_ref, d):
        if d == 1:
            return lse_ref[...]
        for r in range(d):
            lslab_ref[j, pl.ds(r, tm // d, stride=d), :] = lse_ref[:, r * LANES:(r + 1) * LANES]
        return lslab_ref[j]

    def natural_o(o_ref, d):
        if d == 1:
            return o_ref[...].astype(F32)
        for r in range(d):
            for pr in range(D_B // LANES):
                lane0 = r * D_B + pr * LANES
                oslab_ref[pr, pl.ds(r, tm // d, stride=d), :] = o_ref[:, lane0:lane0 + LANES].astype(F32)
        return jnp.concatenate([oslab_ref[pr] for pr in range(D_B // LANES)], axis=1)

    lses = [natural_lse(j, lse_ref, d) for j, (lse_ref, d) in enumerate(zip(lse_refs, dils))]
    top = functools.reduce(jnp.maximum, lses)
    ws = [jnp.exp2(l - top) for l in lses]
    tot = functools.reduce(lambda a, c: a + c, ws)
    e = e_ref[...]
    bmix = jnp.zeros((tm, D_B), F32)
    for wgt, o_ref, d in zip(ws, o_refs, dils):
        wn = wgt / tot
        hi = wn.astype(BF16)
        lo = (wn - hi.astype(F32)).astype(BF16)
        bmix = bmix + (_dot(hi, e) + _dot(lo, e)) * natural_o(o_ref, d)
    bg = (bmix * gb_ref[...].astype(F32)).astype(BF16)
    out_ref[...] = x_ref[...] + _dot(a_ref[...], w_ref[0:D_A, :]) + _dot(bg, w_ref[D_A:D_A + D_B, :])


def _out_proj(x2, a_g, gb, w_out, expand, os_, lses, dils, tm):
    n, d = x2.shape
    row = lambda i: (i, 0)
    const = lambda i: (0, 0)
    one_buf = dict(pipeline_mode=pl.Buffered(1))
    in_specs = ([pl.BlockSpec((tm, d), row), pl.BlockSpec((tm, D_A), row), pl.BlockSpec((tm, D_B), row),
                 pl.BlockSpec((D_A + D_B, d), const, **one_buf),
                 pl.BlockSpec((LANES, D_B), const, **one_buf)]
                + [pl.BlockSpec((tm // dl, dl * D_B), row) for dl in dils]
                + [pl.BlockSpec((tm // dl, dl * LANES), row) for dl in dils])
    return pl.pallas_call(
        functools.partial(_out_kernel, dils),
        grid=(n // tm,),
        in_specs=in_specs,
        out_specs=pl.BlockSpec((tm, d), row),
        out_shape=jax.ShapeDtypeStruct((n, d), F32),
        scratch_shapes=[pltpu.VMEM((D_B // LANES, tm, LANES), F32), pltpu.VMEM((len(dils), tm, LANES), F32)],
        compiler_params=pltpu.CompilerParams(dimension_semantics=("parallel",),
                                             vmem_limit_bytes=VMEM_LIMIT),
        name="merge_outproj",
    )(x2, a_g, gb, w_out, expand, *os_, *lses)


def _pack_w_in(w):
    qa, ka, va, za = (w[:, j * D_A:(j + 1) * D_A] for j in range(4))
    o = 4 * D_A
    qi = w[:, o:o + D_QI]
    ki = w[:, o + D_QI:o + D_QI + IDX_DIM]
    wi = w[:, o + D_QI + IDX_DIM:o + D_QI + IDX_DIM + N_IDX_HEADS]
    wb = w[:, o + D_QI + IDX_DIM + N_IDX_HEADS:]
    wn = jnp.concatenate([ka, za, ki, ki, wb], axis=1).astype(BF16)
    wi = jnp.pad(wi, ((0, 0), (0, WI_ROWS - N_IDX_HEADS)))
    wt = jnp.concatenate([qa, va, qi, wi], axis=1).T.astype(BF16)
    return wn, wt


def _bias_lookup(table, bucket):
    onehot = (bucket[..., None] == np.arange(N_BUCKETS)).astype(np.float32)
    return jnp.einsum('...n,nh->...h', onehot, table.astype(F32), precision=lax.Precision.HIGHEST)


def _dsa_bias_tiles(bias_a, tq, tk):
    k = np.arange(tk)[:, None]
    q = np.arange(tq)[None, :]
    far = bias_a[N_BUCKETS - 1].astype(F32)
    tiles = [_bias_lookup(bias_a, _rel_bucket(q - k + tk * (1 - v))) - far for v in range(1 + tq // tk)]
    return (jnp.transpose(jnp.stack(tiles), (0, 3, 1, 2)) * LOG2E).astype(BF16)


def _dil_bias_tiles(bias_b, window, dil):
    steps = window // dil
    a = np.arange(BLOCK)[:, None]
    c = np.arange(2 * BLOCK)[None, :]
    diff = a - c + BLOCK
    band = (diff >= 0) & (diff <= steps)
    bias = jnp.transpose(_bias_lookup(bias_b, _rel_bucket(diff * dil)), (2, 0, 1)) * LOG2E
    normal = jnp.where(band[None], bias, NEG_BIG)
    first = jnp.where((band & (c >= BLOCK))[None], bias, NEG_BIG)
    return jnp.stack([normal, first]).astype(BF16)


def kernel(x, norm_gain, w_in, w_out, rel_bias, q_norm_a, k_norm_a, q_norm_b, k_norm_b):
    b, s, d = x.shape
    n = b * s
    depth = norm_gain.shape[0]
    tq = min(DSA_TQ, s)
    tk = min(DSA_TK, s)
    tm_in = min(INPROJ_TM, s)
    tm_out = min(OUT_TM, s)
    bias_a = rel_bias[:, :N_HEADS_A]
    bias_b = rel_bias[:, N_HEADS_A:]
    eye_blocks = (jnp.arange(D_A)[:, None] // HEAD_DIM == jnp.arange(D_A)[None, :] // HEAD_DIM).astype(BF16)
    expand = (jnp.arange(LANES)[:, None] == jnp.arange(D_B)[None, :] // HEAD_DIM).astype(BF16)
    assert tk >= MAX_DISTANCE and tq % tk == 0
    dsa_bias = _dsa_bias_tiles(bias_a, tq, tk)
    dil_bias = [_dil_bias_tiles(bias_b, wnd, dil) for wnd, dil in DIL_PATTERNS]
    dils = tuple(dil for _, dil in DIL_PATTERNS)

    for layer in range(depth):
        gn = jnp.stack([jnp.tile(g[layer], N_HEADS_A) for g in (q_norm_a, k_norm_a, q_norm_b, k_norm_b)])
        wn, wt = _pack_w_in(w_in[layer])
        npat = len(dils)
        outs = _inproj(x, norm_gain[layer][None, :], wn, wt, eye_blocks, gn, gn[0][:, None], tm_in, tk, dils)
        qat, ka, vat, ga, qit, ki, wit, gb = outs[:8]
        qviews, kviews, vviews = (outs[8 + j * npat:8 + (j + 1) * npat] for j in range(3))
        a_g = _dsa(qat, ka, vat, qit, ki, wit, ga, dsa_bias, tq, tk)

        os_, lses = [], []
        for j, ((wnd, dil), bias_p) in enumerate(zip(DIL_PATTERNS, dil_bias)):
            assert wnd // dil == BLOCK and s % (dil * BLOCK) == 0 and tm_in % (dil * 2 * SUBLANES) == 0 and tm_out % (dil * SUBLANES) == 0
            n_sub = s // dil
            qb_blocks = min(DIL_BLOCKS_PER_STEP, n_sub // BLOCK)
            nres = max(1, min(dil, DIL_BLOCKS_PER_STEP // qb_blocks))
            o, lse = _dilated_pattern(qviews[j], kviews[j], vviews[j], bias_p, dil, qb_blocks, nres)
            os_.append(o.reshape(n // dil, dil * D_B))
            lses.append(lse.reshape(n // dil, dil * LANES))
        x = _out_proj(x.reshape(n, d), a_g.reshape(n, D_A), gb.reshape(n, D_B), w_out[layer].astype(BF16),
                      expand, os_, lses, dils, tm_out).reshape(b, s, d)
    return x
```

```python
import functools
import math

import jax
import jax.numpy as jnp
import numpy as np
from jax import lax
from jax.experimental import pallas as pl
from jax.experimental.pallas import tpu as pltpu

HEAD_DIM = 64
N_HEADS_A = 8
N_HEADS_B = 8
D_A = N_HEADS_A * HEAD_DIM
D_B = N_HEADS_B * HEAD_DIM
N_IDX_HEADS = 4
IDX_DIM = 64
D_QI = N_IDX_HEADS * IDX_DIM
TOPK_MAX = 256
DIL_PATTERNS = ((128, 1), (512, 4), (2048, 16))
BLOCK = 128
N_BUCKETS = 32
MAX_DISTANCE = 128
EPS = 1e-6

LANES = 128
SUBLANES = 8
WI_ROWS = 16
NEG_BIG = -(2.0 ** 100)
LOG2E = math.log2(math.e)
DSA_TQ = 256
DSA_TK = 256
FAR_UNROLL = 4
SCORE_UNROLL = 4
COUNT_UNROLL = 4
MASK_UNROLL = 4
VMEM_LIMIT = 56 * 1024 * 1024
INPROJ_VMEM_LIMIT = VMEM_LIMIT
INPROJ_SLAB_SETS = (3, 3)
INPROJ_TM = 512
OUT_TM = 512
DIL_BLOCKS_PER_STEP = 8

F32 = jnp.float32
BF16 = jnp.bfloat16
NT_DIMS = (((1,), (1,)), ((), ()))


def _dot(a, b):
    return jnp.dot(a, b, preferred_element_type=F32)


def _dot_nt(a, b):
    return lax.dot_general(a, b, NT_DIMS, preferred_element_type=F32)


def _rel_bucket(dist):
    max_exact = N_BUCKETS // 2
    d = np.maximum(dist, 0)
    df = np.maximum(d, 1).astype(np.float32)
    large = max_exact + (np.log(df / np.float32(max_exact)) / np.float32(math.log(MAX_DISTANCE / max_exact))
                         * np.float32(N_BUCKETS - max_exact)).astype(np.int32)
    large = np.minimum(large, N_BUCKETS - 1)
    return np.where(d < max_exact, d, large).astype(np.int32)


def _inproj_kernel(tk, dils, x_ref, g_ref, wn_ref, wt_ref, bd_ref, gn_ref, gcol_ref,
                   qat_ref, ka_ref, vat_ref, ga_ref, qit_ref, ki_ref, wit_ref, gb_ref, *rest):
    n_b = 3 * len(dils)
    b_refs = [rest[j * len(dils):(j + 1) * len(dils)] for j in range(3)]
    slab_refs = rest[n_b:]
    x = x_ref[0]
    tm = x.shape[0]
    r = lax.rsqrt(jnp.mean(x * x, axis=-1, keepdims=True) + EPS)
    xn = (x * r * g_ref[...]).astype(BF16)
    bd = bd_ref[...]

    def head_norm(q, gain, scale):
        ss = _dot((q * q).astype(BF16), bd)
        return q * lax.rsqrt(ss * (1.0 / HEAD_DIM) + EPS) * (gain * scale)

    def silu(z):
        return z * jax.nn.sigmoid(z)

    c0 = 2 * D_A + LANES
    p_all = _dot(xn, wn_ref[...])
    p = p_all[:, c0:c0 + 4 * D_B]
    gb_ref[0] = silu(p[:, 3 * D_B:4 * D_B]).astype(BF16)

    def emit_views(t, val, refs):
        assert dils[0] == 1
        refs[0][0] = val.astype(BF16)
        if len(dils) == 1:
            return
        npair = D_B // LANES
        slot = [t % ref.shape[0] for ref in slab_refs]
        for pr in range(npair):
            slab_refs[0][slot[0], pr, 0] = val[:, pr * LANES:(pr + 1) * LANES]
        for lvl in range(1, len(dils)):
            d, d_prev = dils[lvl], dils[lvl - 1]
            f = d // d_prev
            for r in range(d):
                for pr in range(npair):
                    blk = slab_refs[lvl - 1][slot[lvl - 1], pr, r % d_prev, pl.ds(r // d_prev, tm // d, stride=f), :]
                    lane0 = r * D_B + pr * LANES
                    refs[lvl][0, :, lane0:lane0 + LANES] = blk.astype(BF16)
                    if lvl + 1 < len(dils):
                        slab_refs[lvl][slot[lvl], pr, r] = blk

    emit_views(0, head_norm(p[:, 0:D_B], gn_ref[2:3, :], HEAD_DIM ** -0.5 * LOG2E), b_refs[0])
    emit_views(1, head_norm(p[:, D_B:2 * D_B], gn_ref[3:4, :], 1.0), b_refs[1])
    emit_views(2, p[:, 2 * D_B:3 * D_B], b_refs[2])

    p = p_all[:, 0:c0]
    ka_ref[0] = head_norm(p[:, 0:D_A], gn_ref[1:2, :], 1.0).astype(BF16)
    ga_ref[0] = silu(p[:, D_A:2 * D_A]).astype(BF16)
    ki_ref[0] = p[:, 2 * D_A:2 * D_A + LANES].astype(BF16)

    pt = _dot_nt(wt_ref[...], xn)
    qt = pt[0:D_A, :]
    ss = _dot(bd, (qt * qt).astype(BF16))
    qt = qt * lax.rsqrt(ss * (1.0 / HEAD_DIM) + EPS) * (gcol_ref[...] * (HEAD_DIM ** -0.5 * LOG2E))
    qat_ref[0] = qt.astype(BF16)
    for cc in range(tm // tk):
        vat_ref[0, cc] = pt[D_A:2 * D_A, cc * tk:(cc + 1) * tk].astype(BF16)
    qit_ref[0] = pt[2 * D_A:2 * D_A + D_QI, :].astype(BF16)
    wit_ref[0] = pt[2 * D_A + D_QI:2 * D_A + D_QI + WI_ROWS, :]


def _inproj(x, g, wn, wt, bd, gn, gcol, tm, tk, dils):
    b, s, d = x.shape
    const = lambda bi, j: (0, 0)
    tok = lambda c: pl.BlockSpec((1, tm, c), lambda bi, j: (bi, j, 0))
    feat = lambda c: pl.BlockSpec((1, c, tm), lambda bi, j: (bi, 0, j))
    one_buf = dict(pipeline_mode=pl.Buffered(1))
    sds = jax.ShapeDtypeStruct
    view_shapes = tuple(sds((b, s // dl, dl * D_B), BF16) for dl in dils)
    view_specs = tuple(pl.BlockSpec((1, tm // dl, dl * D_B), lambda bi, j: (bi, j, 0)) for dl in dils)
    out_shapes = (sds((b, D_A, s), BF16), sds((b, s, D_A), BF16), sds((b, s // tk, D_A, tk), BF16),
                  sds((b, s, D_A), BF16), sds((b, D_QI, s), BF16), sds((b, s, LANES), BF16),
                  sds((b, WI_ROWS, s), F32), sds((b, s, D_B), BF16)) + view_shapes * 3
    out_specs = (feat(D_A), tok(D_A),
                 pl.BlockSpec((1, tm // tk, D_A, tk), lambda bi, j: (bi, j, 0, 0)),
                 tok(D_A), feat(D_QI), tok(LANES), feat(WI_ROWS), tok(D_B)) + view_specs * 3
    return pl.pallas_call(
        functools.partial(_inproj_kernel, tk, dils),
        grid=(b, s // tm),
        scratch_shapes=[pltpu.VMEM((sets, D_B // LANES, dl, tm // dl, LANES), F32)
                        for sets, dl in zip(INPROJ_SLAB_SETS, dils[:-1])],
        in_specs=[tok(d),
                  pl.BlockSpec((1, d), const, **one_buf),
                  pl.BlockSpec(wn.shape, const, **one_buf),
                  pl.BlockSpec(wt.shape, const, **one_buf),
                  pl.BlockSpec((D_A, D_A), const, **one_buf),
                  pl.BlockSpec((4, D_A), const, **one_buf),
                  pl.BlockSpec((D_A, 1), const, **one_buf)],
        out_specs=out_specs,
        out_shape=out_shapes,
        compiler_params=pltpu.CompilerParams(dimension_semantics=("parallel", "parallel"),
                                             vmem_limit_bytes=INPROJ_VMEM_LIMIT),
        name="inproj",
    )(x, g, wn, wt, bd, gn, gcol)


def _ordered_bits_to_float(u):
    key = u ^ jnp.int32(-2 ** 31)
    bits = key ^ ((key >> 31) & jnp.int32(0x7FFFFFFF))
    return lax.bitcast_convert_type(bits, F32)


def _fold_rows(x, group):
    parts = [None] * 4
    for r in range(x.shape[0] // group):
        blk = x[r * group:(r + 1) * group, :]
        parts[r % 4] = blk if parts[r % 4] is None else parts[r % 4] + blk
    return (parts[0] + parts[1]) + (parts[2] + parts[3])


def _chunk_loop(n, body, init, unroll):
    def group(j, carry):
        for u in range(unroll):
            carry = body(unroll * j + u, carry)
        return carry
    carry = lax.fori_loop(0, n // unroll, group, init)
    return lax.fori_loop((n // unroll) * unroll, n, body, carry)


def _prefix_bf16_bits(u):
    key = u ^ jnp.int32(-2 ** 31)
    return (key ^ ((key >> 31) & jnp.int32(0x7FFFFFFF))) & jnp.int32(-2 ** 16)


def _float_bits_to_ordered(bits):
    return (bits ^ ((bits >> 31) & jnp.int32(0x7FFFFFFF))) ^ jnp.int32(-2 ** 31)


def _dsa_kernel(topk, tq, tk, nkc, qat_ref, ka_ref, vat_ref, qit_ref, ki_ref, wit_ref, ga_ref, bias_ref,
                out_ref, sc_ref, sc16_ref, madd_ref, qz_ref, m_ref, l_ref, acc_ref):
    i = pl.program_id(1)
    ratio = tq // tk
    nchunks = (i + 1) * ratio
    idx_scale = (N_IDX_HEADS * IDX_DIM) ** -0.5
    top_half = lax.broadcasted_iota(jnp.int32, (LANES, tq), 0) < HEAD_DIM

    def pad_heads(ref, npairs):
        out = []
        for p in range(npairs):
            blk = ref[0, p * LANES:(p + 1) * LANES, :]
            zero = jnp.zeros_like(blk)
            out.append(jnp.where(top_half, blk, zero))
            out.append(jnp.where(top_half, zero, blk))
        return out

    qiz = pad_heads(qit_ref, N_IDX_HEADS // 2)
    w = wit_ref[0]
    krow = lax.broadcasted_iota(jnp.int32, (tk, tq), 0)
    qcol_g = i * tq + lax.broadcasted_iota(jnp.int32, (tk, tq), 1)

    def score_chunk(c, carry):
        kc = ki_ref[0, pl.ds(pl.multiple_of(c * tk, tk), tk), :]
        acc = jnp.zeros((tk, tq), F32)
        for h in range(N_IDX_HEADS):
            acc = acc + jnp.maximum(_dot(kc, qiz[h]), 0.0) * w[h:h + 1, :]
        s = acc * idx_scale + 0.0
        s = jnp.where(krow + c * tk <= qcol_g, s, -jnp.inf)
        sc_ref[c] = s
        sc16_ref[c] = s.astype(BF16)
        return carry

    _chunk_loop(nchunks, score_chunk, 0, SCORE_UNROLL)

    def count_pass(pred):
        def body(c, acc):
            return acc + _fold_rows(jnp.where(pred(sc_ref[c], c), 1.0, 0.0), SUBLANES)
        acc = _chunk_loop(nchunks, body, jnp.zeros((SUBLANES, tq), F32), COUNT_UNROLL)
        return jnp.sum(acc, axis=0, keepdims=True)

    def count_pass16(cand16):
        one, zero = jnp.ones((), BF16), jnp.zeros((), BF16)

        def body(c, acc):
            return acc + _fold_rows(jnp.where(sc16_ref[c] >= cand16, one, zero), 2 * SUBLANES)
        acc = _chunk_loop(nchunks, body, jnp.zeros((2 * SUBLANES, tq), BF16), COUNT_UNROLL)
        return jnp.sum(acc.astype(F32), axis=0, keepdims=True)

    def bit_step16(it, cur):
        cand = cur | lax.shift_left(jnp.int32(1), 31 - it)
        cand16 = lax.bitcast_convert_type(_prefix_bf16_bits(cand), F32).astype(BF16)
        return jnp.where(count_pass16(cand16) >= float(topk), cand, cur)

    assert nkc * tk <= 256 * 2 * SUBLANES
    cur16 = lax.fori_loop(0, 16, bit_step16, jnp.zeros((1, tq), jnp.int32))
    short = (cur16 & jnp.int32(-2 ** 23)) == 0

    key_v = _float_bits_to_ordered(_prefix_bf16_bits(cur16))
    window = (key_v - (2 ** 15 + 1), key_v + 2 ** 16)

    def bisect_step(it, bounds):
        lo, hi = bounds
        mid = lo + ((hi - lo) >> 1)
        midf = _ordered_bits_to_float(mid)
        take = count_pass(lambda s, c: s >= midf) >= float(topk)
        return jnp.where(take, mid, lo), jnp.where(take, hi, mid)

    n_bisect = (2 ** 16 + 2 ** 15 + 1).bit_length()
    cur, _ = lax.fori_loop(0, n_bisect, bisect_step, window)
    thr = jnp.where(short, -jnp.inf, _ordered_bits_to_float(cur))
    cnt_gt = count_pass(lambda s, c: s > thr)
    n_tie = jnp.where(short, 0.0, float(topk) - cnt_gt)
    below = (lax.broadcasted_iota(jnp.int32, (tk, tk), 1)
             < lax.broadcasted_iota(jnp.int32, (tk, tk), 0)).astype(BF16)
    ones_row = jnp.ones((SUBLANES, tk), BF16)

    def mask_chunk(c, seen):
        s = sc_ref[c]
        tie = s == thr
        tie01 = jnp.where(tie, 1.0, 0.0).astype(BF16)
        rank = _dot(below, tie01) + seen
        sel = (s > thr) | (tie & (rank < n_tie))
        madd_ref[c] = jnp.where(sel, 0.0, NEG_BIG).astype(BF16)
        return seen + _dot(ones_row, tie01)[0:1, :]

    _chunk_loop(nchunks, mask_chunk, jnp.zeros((1, tq), F32), MASK_UNROLL)

    for h, qz in enumerate(pad_heads(qat_ref, N_HEADS_A // 2)):
        qz_ref[h] = qz
    m_ref[...] = jnp.full(m_ref.shape, NEG_BIG, F32)
    l_ref[...] = jnp.zeros(l_ref.shape, F32)
    acc_ref[...] = jnp.zeros(acc_ref.shape, F32)

    def attn_chunk(c, variant):
        k0 = pl.multiple_of(c * tk, tk)
        for h in range(N_HEADS_A):
            p = h // 2
            kp = ka_ref[0, pl.ds(k0, tk), p * LANES:(p + 1) * LANES]
            logits = _dot(kp, qz_ref[h]).astype(BF16) + madd_ref[c]
            if variant is not None:
                logits = logits + bias_ref[variant, h]
            hrow = slice(h, h + 1)
            m_old = m_ref[hrow, :]
            m_new = jnp.maximum(m_old, jnp.max(logits, axis=0, keepdims=True).astype(F32))
            alpha = jnp.exp2(m_old - m_new)
            pr = jnp.exp2(logits - m_new.astype(BF16))
            l_ref[hrow, :] = alpha * l_ref[hrow, :] + _dot(ones_row, pr)[0:1, :]
            m_ref[hrow, :] = m_new
            rows = slice(h * HEAD_DIM, (h + 1) * HEAD_DIM)
            acc_ref[rows, :] = alpha * acc_ref[rows, :] + _dot(vat_ref[0, c, rows, :], pr)

    first_near = i * ratio - 1
    n_far = jnp.maximum(first_near, 0)

    def far_group(j, carry):
        for u in range(FAR_UNROLL):
            attn_chunk(FAR_UNROLL * j + u, None)
        return carry

    def far_body(c, carry):
        attn_chunk(c, None)
        return carry

    n_grouped = (n_far // FAR_UNROLL) * FAR_UNROLL
    lax.fori_loop(0, n_far // FAR_UNROLL, far_group, 0)

    @pl.when(n_far - n_grouped >= 2)
    def _():
        attn_chunk(n_grouped, None)
        attn_chunk(n_grouped + 1, None)

    lax.fori_loop(n_grouped + 2 * ((n_far - n_grouped) // 2), n_far, far_body, 0)

    @pl.when(i > 0)
    def _():
        for v in range(ratio + 1):
            attn_chunk(first_near + v, v)

    @pl.when(i == 0)
    def _():
        for v in range(1, ratio + 1):
            attn_chunk(v - 1, v)

    outs = []
    for h in range(N_HEADS_A):
        rows = slice(h * HEAD_DIM, (h + 1) * HEAD_DIM)
        outs.append(acc_ref[rows, :] / l_ref[h:h + 1, :])
    o = jnp.concatenate(outs, axis=0).T
    out_ref[0] = (o * ga_ref[0].astype(F32)).astype(BF16)


def _dsa(qat, ka, vat, qit, ki, wit, ga, bias_a, tq, tk):
    b, s, _ = ka.shape
    topk = min(TOPK_MAX, s // 4)
    nq = s // tq
    nkc = s // tk
    qfeat = lambda c: pl.BlockSpec((1, c, tq), lambda bi, i: (bi, 0, i))
    qtok = lambda c: pl.BlockSpec((1, tq, c), lambda bi, i: (bi, i, 0))
    full = lambda c: pl.BlockSpec((1, s, c), lambda bi, i: (bi, 0, 0))
    return pl.pallas_call(
        functools.partial(_dsa_kernel, topk, tq, tk, nkc),
        grid=(b, nq),
        in_specs=[qfeat(D_A), full(D_A),
                  pl.BlockSpec((1, nkc, D_A, tk), lambda bi, i: (bi, 0, 0, 0)),
                  qfeat(D_QI), full(LANES), qfeat(WI_ROWS), qtok(D_A),
                  pl.BlockSpec(bias_a.shape, lambda bi, i: (0, 0, 0, 0), pipeline_mode=pl.Buffered(1))],
        out_specs=qtok(D_A),
        out_shape=jax.ShapeDtypeStruct((b, s, D_A), BF16),
        scratch_shapes=[pltpu.VMEM((nkc, tk, tq), F32),
                        pltpu.VMEM((nkc, tk, tq), BF16),
                        pltpu.VMEM((nkc, tk, tq), BF16),
                        pltpu.VMEM((N_HEADS_A, LANES, tq), BF16),
                        pltpu.VMEM((N_HEADS_A, tq), F32),
                        pltpu.VMEM((N_HEADS_A, tq), F32),
                        pltpu.VMEM((D_A, tq), F32)],
        compiler_params=pltpu.CompilerParams(dimension_semantics=("parallel", "arbitrary"),
                                             vmem_limit_bytes=VMEM_LIMIT),
        name="dsa",
    )(qat, ka, vat, qit, ki, wit, ga, bias_a)


def _dil_kernel(qb_blocks, nres, q_ref, kc_ref, kp_ref, vc_ref, vp_ref, bias_ref, o_ref, lse_ref):
    j = pl.program_id(2)
    lane = lax.broadcasted_iota(jnp.int32, (BLOCK, LANES), 1)
    low_half = lane < HEAD_DIM
    ones_cols = jnp.ones((2 * BLOCK, LANES), BF16)
    for rr, blk in [(rr, blk) for rr in range(nres) for blk in range(qb_blocks)]:
        rows = slice(blk * BLOCK, (blk + 1) * BLOCK)
        if blk == 0:
            variant = jnp.where(j == 0, 1, 0)
        else:
            variant = 0
        lse_tile = jnp.zeros((BLOCK, LANES), F32)
        for p in range(N_HEADS_B // 2):
            lanes = slice(rr * D_B + p * LANES, rr * D_B + (p + 1) * LANES)
            if blk == 0:
                kband = jnp.concatenate([kp_ref[0, :, lanes], kc_ref[0, 0:BLOCK, lanes]], axis=0)
                vband = jnp.concatenate([vp_ref[0, :, lanes], vc_ref[0, 0:BLOCK, lanes]], axis=0)
            else:
                band = slice((blk - 1) * BLOCK, (blk + 1) * BLOCK)
                kband = kc_ref[0, band, lanes]
                vband = vc_ref[0, band, lanes]
            vones = jnp.concatenate([vband, ones_cols], axis=1)
            qp = q_ref[0, rows, lanes]
            zero = jnp.zeros_like(qp)
            outs = []
            for e in range(2):
                h = 2 * p + e
                qz = jnp.where(low_half, qp, zero) if e == 0 else jnp.where(low_half, zero, qp)
                logits = _dot_nt(qz, kband).astype(BF16) + bias_ref[variant, h]
                m = jnp.max(logits, axis=1, keepdims=True)
                pr = jnp.exp2(logits - m)
                nd = _dot(pr, vones)
                den = nd[:, LANES:]
                outs.append(nd[:, :LANES] / den)
                lse_tile = lse_tile + jnp.where(lane == h, m.astype(F32) + jnp.log2(den), 0.0)
            o_ref[0, rows, lanes] = jnp.where(low_half, outs[0], outs[1]).astype(o_ref.dtype)
        lse_ref[0, rows, rr * LANES:(rr + 1) * LANES] = lse_tile


def _dilated_pattern(qv, kv, vv, bias_p, dil, qb_blocks, nres):
    b, n_sub, _ = qv.shape
    tu = qb_blocks * BLOCK
    nj = n_sub // tu
    cur = pl.BlockSpec((1, tu, nres * D_B), lambda bi, r, j: (bi, j, r))
    prev = pl.BlockSpec((1, BLOCK, nres * D_B), lambda bi, r, j: (bi, jnp.maximum(j * qb_blocks - 1, 0), r))
    return pl.pallas_call(
        functools.partial(_dil_kernel, qb_blocks, nres),
        grid=(b, dil // nres, nj),
        in_specs=[cur, cur, prev, cur, prev,
                  pl.BlockSpec((2, N_HEADS_B, BLOCK, 2 * BLOCK), lambda bi, r, j: (0, 0, 0, 0),
                               pipeline_mode=pl.Buffered(1))],
        out_specs=(pl.BlockSpec((1, tu, nres * D_B), lambda bi, r, j: (bi, j, r)),
                   pl.BlockSpec((1, tu, nres * LANES), lambda bi, r, j: (bi, j, r))),
        out_shape=(jax.ShapeDtypeStruct((b, n_sub, dil * D_B), BF16),
                   jax.ShapeDtypeStruct((b, n_sub, dil * LANES), F32)),
        compiler_params=pltpu.CompilerParams(dimension_semantics=("parallel", "parallel", "arbitrary"),
                                             vmem_limit_bytes=VMEM_LIMIT),
        name=f"dilated_d{dil}",
    )(qv, kv, kv, vv, vv, bias_p)


def _out_kernel(dils, x_ref, a_ref, gb_ref, w_ref, e_ref, *refs):
    npat = len(dils)
    o_refs = refs[0:npat]
    lse_refs = refs[npat:2 * npat]
    out_ref = refs[2 * npat]
    oslab_ref, lslab_ref = refs[2 * npat + 1:2 * npat + 3]
    tm = x_ref.shape[0]

    def natural_lse(j, lse_ref, d):
        if d == 1:
            return lse_ref[...]
        for r in range(d):
            lslab_ref[j, pl.ds(r, tm // d, stride=d), :] = lse_ref[:, r * LANES:(r + 1) * LANES]
        return lslab_ref[j]

    def natural_o(o_ref, d):
        if d == 1:
            return o_ref[...].astype(F32)
        for r in range(d):
            for pr in range(D_B // LANES):
                lane0 = r * D_B + pr * LANES
                oslab_ref[pr, pl.ds(r, tm // d, stride=d), :] = o_ref[:, lane0:lane0 + LANES].astype(F32)
        return jnp.concatenate([oslab_ref[pr] for pr in range(D_B // LANES)], axis=1)

    lses = [natural_lse(j, lse_ref, d) for j, (lse_ref, d) in enumerate(zip(lse_refs, dils))]
    top = functools.reduce(jnp.maximum, lses)
    ws = [jnp.exp2(l - top) for l in lses]
    tot = functools.reduce(lambda a, c: a + c, ws)
    e = e_ref[...]
    bmix = jnp.zeros((tm, D_B), F32)
    for wgt, o_ref, d in zip(ws, o_refs, dils):
        wn = wgt / tot
        hi = wn.astype(BF16)
        lo = (wn - hi.astype(F32)).astype(BF16)
        bmix = bmix + (_dot(hi, e) + _dot(lo, e)) * natural_o(o_ref, d)
    bg = (bmix * gb_ref[...].astype(F32)).astype(BF16)
    out_ref[...] = x_ref[...] + _dot(a_ref[...], w_ref[0:D_A, :]) + _dot(bg, w_ref[D_A:D_A + D_B, :])


def _out_proj(x2, a_g, gb, w_out, expand, os_, lses, dils, tm):
    n, d = x2.shape
    row = lambda i: (i, 0)
    const = lambda i: (0, 0)
    one_buf = dict(pipeline_mode=pl.Buffered(1))
    in_specs = ([pl.BlockSpec((tm, d), row), pl.BlockSpec((tm, D_A), row), pl.BlockSpec((tm, D_B), row),
                 pl.BlockSpec((D_A + D_B, d), const, **one_buf),
                 pl.BlockSpec((LANES, D_B), const, **one_buf)]
                + [pl.BlockSpec((tm // dl, dl * D_B), row) for dl in dils]
                + [pl.BlockSpec((tm // dl, dl * LANES), row) for dl in dils])
    return pl.pallas_call(
        functools.partial(_out_kernel, dils),
        grid=(n // tm,),
        in_specs=in_specs,
        out_specs=pl.BlockSpec((tm, d), row),
        out_shape=jax.ShapeDtypeStruct((n, d), F32),
        scratch_shapes=[pltpu.VMEM((D_B // LANES, tm, LANES), F32), pltpu.VMEM((len(dils), tm, LANES), F32)],
        compiler_params=pltpu.CompilerParams(dimension_semantics=("parallel",),
                                             vmem_limit_bytes=VMEM_LIMIT),
        name="merge_outproj",
    )(x2, a_g, gb, w_out, expand, *os_, *lses)


def _pack_w_in(w):
    qa, ka, va, za = (w[:, j * D_A:(j + 1) * D_A] for j in range(4))
    o = 4 * D_A
    qi = w[:, o:o + D_QI]
    ki = w[:, o + D_QI:o + D_QI + IDX_DIM]
    wi = w[:, o + D_QI + IDX_DIM:o + D_QI + IDX_DIM + N_IDX_HEADS]
    wb = w[:, o + D_QI + IDX_DIM + N_IDX_HEADS:]
    wn = jnp.concatenate([ka, za, ki, ki, wb], axis=1).astype(BF16)
    wi = jnp.pad(wi, ((0, 0), (0, WI_ROWS - N_IDX_HEADS)))
    wt = jnp.concatenate([qa, va, qi, wi], axis=1).T.astype(BF16)
    return wn, wt


def _bias_lookup(table, bucket):
    onehot = (bucket[..., None] == np.arange(N_BUCKETS)).astype(np.float32)
    return jnp.einsum('...n,nh->...h', onehot, table.astype(F32), precision=lax.Precision.HIGHEST)


def _dsa_bias_tiles(bias_a, tq, tk):
    k = np.arange(tk)[:, None]
    q = np.arange(tq)[None, :]
    far = bias_a[N_BUCKETS - 1].astype(F32)
    tiles = [_bias_lookup(bias_a, _rel_bucket(q - k + tk * (1 - v))) - far for v in range(1 + tq // tk)]
    return (jnp.transpose(jnp.stack(tiles), (0, 3, 1, 2)) * LOG2E).astype(BF16)


def _dil_bias_tiles(bias_b, window, dil):
    steps = window // dil
    a = np.arange(BLOCK)[:, None]
    c = np.arange(2 * BLOCK)[None, :]
    diff = a - c + BLOCK
    band = (diff >= 0) & (diff <= steps)
    bias = jnp.transpose(_bias_lookup(bias_b, _rel_bucket(diff * dil)), (2, 0, 1)) * LOG2E
    normal = jnp.where(band[None], bias, NEG_BIG)
    first = jnp.where((band & (c >= BLOCK))[None], bias, NEG_BIG)
    return jnp.stack([normal, first]).astype(BF16)


def kernel(x, norm_gain, w_in, w_out, rel_bias, q_norm_a, k_norm_a, q_norm_b, k_norm_b):
    b, s, d = x.shape
    n = b * s
    depth = norm_gain.shape[0]
    tq = min(DSA_TQ, s)
    tk = min(DSA_TK, s)
    tm_in = min(INPROJ_TM, s)
    tm_out = min(OUT_TM, s)
    bias_a = rel_bias[:, :N_HEADS_A]
    bias_b = rel_bias[:, N_HEADS_A:]
    eye_blocks = (jnp.arange(D_A)[:, None] // HEAD_DIM == jnp.arange(D_A)[None, :] // HEAD_DIM).astype(BF16)
    expand = (jnp.arange(LANES)[:, None] == jnp.arange(D_B)[None, :] // HEAD_DIM).astype(BF16)
    assert tk >= MAX_DISTANCE and tq % tk == 0
    dsa_bias = _dsa_bias_tiles(bias_a, tq, tk)
    dil_bias = [_dil_bias_tiles(bias_b, wnd, dil) for wnd, dil in DIL_PATTERNS]
    dils = tuple(dil for _, dil in DIL_PATTERNS)

    for layer in range(depth):
        gn = jnp.stack([jnp.tile(g[layer], N_HEADS_A) for g in (q_norm_a, k_norm_a, q_norm_b, k_norm_b)])
        wn, wt = _pack_w_in(w_in[layer])
        npat = len(dils)
        outs = _inproj(x, norm_gain[layer][None, :], wn, wt, eye_blocks, gn, gn[0][:, None], tm_in, tk, dils)
        qat, ka, vat, ga, qit, ki, wit, gb = outs[:8]
        qviews, kviews, vviews = (outs[8 + j * npat:8 + (j + 1) * npat] for j in range(3))
        a_g = _dsa(qat, ka, vat, qit, ki, wit, ga, dsa_bias, tq, tk)

        os_, lses = [], []
        for j, ((wnd, dil), bias_p) in enumerate(zip(DIL_PATTERNS, dil_bias)):
            assert wnd // dil == BLOCK and s % (dil * BLOCK) == 0 and tm_in % (dil * 2 * SUBLANES) == 0 and tm_out % (dil * SUBLANES) == 0
            n_sub = s // dil
            qb_blocks = min(DIL_BLOCKS_PER_STEP, n_sub // BLOCK)
            nres = max(1, min(dil, DIL_BLOCKS_PER_STEP // qb_blocks))
            o, lse = _dilated_pattern(qviews[j], kviews[j], vviews[j], bias_p, dil, qb_blocks, nres)
            os_.append(o.reshape(n // dil, dil * D_B))
            lses.append(lse.reshape(n // dil, dil * LANES))
        x = _out_proj(x.reshape(n, d), a_g.reshape(n, D_A), gb.reshape(n, D_B), w_out[layer].astype(BF16),
                      expand, os_, lses, dils, tm_out).reshape(b, s, d)
    return x
```

```python
import functools
import math

import jax
import jax.numpy as jnp
import numpy as np
from jax import lax
from jax.experimental import pallas as pl
from jax.experimental.pallas import tpu as pltpu

HEAD_DIM = 64
N_HEADS_A = 8
N_HEADS_B = 8
D_A = N_HEADS_A * HEAD_DIM
D_B = N_HEADS_B * HEAD_DIM
N_IDX_HEADS = 4
IDX_DIM = 64
D_QI = N_IDX_HEADS * IDX_DIM
TOPK_MAX = 256
DIL_PATTERNS = ((128, 1), (512, 4), (2048, 16))
BLOCK = 128
N_BUCKETS = 32
MAX_DISTANCE = 128
EPS = 1e-6

LANES = 128
SUBLANES = 8
WI_ROWS = 16
NEG_BIG = -(2.0 ** 100)
LOG2E = math.log2(math.e)
DSA_TQ = 256
DSA_TK = 256
FAR_UNROLL = 4
SCORE_UNROLL = 4
COUNT_UNROLL = 4
MASK_UNROLL = 4
VMEM_LIMIT = 56 * 1024 * 1024
INPROJ_VMEM_LIMIT = VMEM_LIMIT
INPROJ_SLAB_SETS = (3, 3)
INPROJ_TM = 512
OUT_TM = 1024
DIL_BLOCKS_PER_STEP = 16

F32 = jnp.float32
BF16 = jnp.bfloat16
NT_DIMS = (((1,), (1,)), ((), ()))


def _dot(a, b):
    return jnp.dot(a, b, preferred_element_type=F32)


def _dot_nt(a, b):
    return lax.dot_general(a, b, NT_DIMS, preferred_element_type=F32)


def _rel_bucket(dist):
    max_exact = N_BUCKETS // 2
    d = np.maximum(dist, 0)
    df = np.maximum(d, 1).astype(np.float32)
    large = max_exact + (np.log(df / np.float32(max_exact)) / np.float32(math.log(MAX_DISTANCE / max_exact))
                         * np.float32(N_BUCKETS - max_exact)).astype(np.int32)
    large = np.minimum(large, N_BUCKETS - 1)
    return np.where(d < max_exact, d, large).astype(np.int32)


def _inproj_kernel(tk, dils, x_ref, g_ref, wn_ref, wt_ref, bd_ref, gn_ref, gcol_ref,
                   qat_ref, ka_ref, vat_ref, ga_ref, qit_ref, ki_ref, wit_ref, gb_ref, *rest):
    n_b = 3 * len(dils)
    b_refs = [rest[j * len(dils):(j + 1) * len(dils)] for j in range(3)]
    slab_refs = rest[n_b:]
    x = x_ref[0]
    tm = x.shape[0]
    r = lax.rsqrt(jnp.mean(x * x, axis=-1, keepdims=True) + EPS)
    xn = (x * r * g_ref[...]).astype(BF16)
    bd = bd_ref[...]

    def head_norm(q, gain, scale):
        ss = _dot((q * q).astype(BF16), bd)
        return q * lax.rsqrt(ss * (1.0 / HEAD_DIM) + EPS) * (gain * scale)

    def silu(z):
        return z * jax.nn.sigmoid(z)

    c0 = 2 * D_A + LANES
    p_all = _dot(xn, wn_ref[...])
    p = p_all[:, c0:c0 + 4 * D_B]
    gb_ref[0] = silu(p[:, 3 * D_B:4 * D_B]).astype(BF16)

    def emit_views(t, val, refs):
        assert dils[0] == 1
        refs[0][0] = val.astype(BF16)
        if len(dils) == 1:
            return
        npair = D_B // LANES
        slot = [t % ref.shape[0] for ref in slab_refs]
        for pr in range(npair):
            slab_refs[0][slot[0], pr, 0] = val[:, pr * LANES:(pr + 1) * LANES]
        for lvl in range(1, len(dils)):
            d, d_prev = dils[lvl], dils[lvl - 1]
            f = d // d_prev
            for r in range(d):
                for pr in range(npair):
                    blk = slab_refs[lvl - 1][slot[lvl - 1], pr, r % d_prev, pl.ds(r // d_prev, tm // d, stride=f), :]
                    lane0 = r * D_B + pr * LANES
                    refs[lvl][0, :, lane0:lane0 + LANES] = blk.astype(BF16)
                    if lvl + 1 < len(dils):
                        slab_refs[lvl][slot[lvl], pr, r] = blk

    emit_views(0, head_norm(p[:, 0:D_B], gn_ref[2:3, :], HEAD_DIM ** -0.5 * LOG2E), b_refs[0])
    emit_views(1, head_norm(p[:, D_B:2 * D_B], gn_ref[3:4, :], 1.0), b_refs[1])
    emit_views(2, p[:, 2 * D_B:3 * D_B], b_refs[2])

    p = p_all[:, 0:c0]
    ka_ref[0] = head_norm(p[:, 0:D_A], gn_ref[1:2, :], 1.0).astype(BF16)
    ga_ref[0] = silu(p[:, D_A:2 * D_A]).astype(BF16)
    ki_ref[0] = p[:, 2 * D_A:2 * D_A + LANES].astype(BF16)

    pt = _dot_nt(wt_ref[...], xn)
    qt = pt[0:D_A, :]
    ss = _dot(bd, (qt * qt).astype(BF16))
    qt = qt * lax.rsqrt(ss * (1.0 / HEAD_DIM) + EPS) * (gcol_ref[...] * (HEAD_DIM ** -0.5 * LOG2E))
    qat_ref[0] = qt.astype(BF16)
    for cc in range(tm // tk):
        vat_ref[0, cc] = pt[D_A:2 * D_A, cc * tk:(cc + 1) * tk].astype(BF16)
    qit_ref[0] = pt[2 * D_A:2 * D_A + D_QI, :].astype(BF16)
    wit_ref[0] = pt[2 * D_A + D_QI:2 * D_A + D_QI + WI_ROWS, :]


def _inproj(x, g, wn, wt, bd, gn, gcol, tm, tk, dils):
    b, s, d = x.shape
    const = lambda bi, j: (0, 0)
    tok = lambda c: pl.BlockSpec((1, tm, c), lambda bi, j: (bi, j, 0))
    feat = lambda c: pl.BlockSpec((1, c, tm), lambda bi, j: (bi, 0, j))
    one_buf = dict(pipeline_mode=pl.Buffered(1))
    sds = jax.ShapeDtypeStruct
    view_shapes = tuple(sds((b, s // dl, dl * D_B), BF16) for dl in dils)
    view_specs = tuple(pl.BlockSpec((1, tm // dl, dl * D_B), lambda bi, j: (bi, j, 0)) for dl in dils)
    out_shapes = (sds((b, D_A, s), BF16), sds((b, s, D_A), BF16), sds((b, s // tk, D_A, tk), BF16),
                  sds((b, s, D_A), BF16), sds((b, D_QI, s), BF16), sds((b, s, LANES), BF16),
                  sds((b, WI_ROWS, s), F32), sds((b, s, D_B), BF16)) + view_shapes * 3
    out_specs = (feat(D_A), tok(D_A),
                 pl.BlockSpec((1, tm // tk, D_A, tk), lambda bi, j: (bi, j, 0, 0)),
                 tok(D_A), feat(D_QI), tok(LANES), feat(WI_ROWS), tok(D_B)) + view_specs * 3
    return pl.pallas_call(
        functools.partial(_inproj_kernel, tk, dils),
        grid=(b, s // tm),
        scratch_shapes=[pltpu.VMEM((sets, D_B // LANES, dl, tm // dl, LANES), F32)
                        for sets, dl in zip(INPROJ_SLAB_SETS, dils[:-1])],
        in_specs=[tok(d),
                  pl.BlockSpec((1, d), const, **one_buf),
                  pl.BlockSpec(wn.shape, const, **one_buf),
                  pl.BlockSpec(wt.shape, const, **one_buf),
                  pl.BlockSpec((D_A, D_A), const, **one_buf),
                  pl.BlockSpec((4, D_A), const, **one_buf),
                  pl.BlockSpec((D_A, 1), const, **one_buf)],
        out_specs=out_specs,
        out_shape=out_shapes,
        compiler_params=pltpu.CompilerParams(dimension_semantics=("parallel", "parallel"),
                                             vmem_limit_bytes=INPROJ_VMEM_LIMIT),
        name="inproj",
    )(x, g, wn, wt, bd, gn, gcol)


def _ordered_bits_to_float(u):
    key = u ^ jnp.int32(-2 ** 31)
    bits = key ^ ((key >> 31) & jnp.int32(0x7FFFFFFF))
    return lax.bitcast_convert_type(bits, F32)


def _fold_rows(x, group):
    parts = [None] * 4
    for r in range(x.shape[0] // group):
        blk = x[r * group:(r + 1) * group, :]
        parts[r % 4] = blk if parts[r % 4] is None else parts[r % 4] + blk
    return (parts[0] + parts[1]) + (parts[2] + parts[3])


def _chunk_loop(n, body, init, unroll):
    def group(j, carry):
        for u in range(unroll):
            carry = body(unroll * j + u, carry)
        return carry
    carry = lax.fori_loop(0, n // unroll, group, init)
    return lax.fori_loop((n // unroll) * unroll, n, body, carry)


def _prefix_bf16_bits(u):
    key = u ^ jnp.int32(-2 ** 31)
    return (key ^ ((key >> 31) & jnp.int32(0x7FFFFFFF))) & jnp.int32(-2 ** 16)


def _float_bits_to_ordered(bits):
    return (bits ^ ((bits >> 31) & jnp.int32(0x7FFFFFFF))) ^ jnp.int32(-2 ** 31)


def _dsa_kernel(topk, tq, tk, nkc, qat_ref, ka_ref, vat_ref, qit_ref, ki_ref, wit_ref, ga_ref, bias_ref,
                out_ref, sc_ref, sc16_ref, madd_ref, qz_ref, m_ref, l_ref, acc_ref):
    i = pl.program_id(1)
    ratio = tq // tk
    nchunks = (i + 1) * ratio
    idx_scale = (N_IDX_HEADS * IDX_DIM) ** -0.5
    top_half = lax.broadcasted_iota(jnp.int32, (LANES, tq), 0) < HEAD_DIM

    def pad_heads(ref, npairs):
        out = []
        for p in range(npairs):
            blk = ref[0, p * LANES:(p + 1) * LANES, :]
            zero = jnp.zeros_like(blk)
            out.append(jnp.where(top_half, blk, zero))
            out.append(jnp.where(top_half, zero, blk))
        return out

    qiz = pad_heads(qit_ref, N_IDX_HEADS // 2)
    w = wit_ref[0]
    krow = lax.broadcasted_iota(jnp.int32, (tk, tq), 0)
    qcol_g = i * tq + lax.broadcasted_iota(jnp.int32, (tk, tq), 1)

    def score_chunk(c, carry):
        kc = ki_ref[0, pl.ds(pl.multiple_of(c * tk, tk), tk), :]
        acc = jnp.zeros((tk, tq), F32)
        for h in range(N_IDX_HEADS):
            acc = acc + jnp.maximum(_dot(kc, qiz[h]), 0.0) * w[h:h + 1, :]
        s = acc * idx_scale + 0.0
        s = jnp.where(krow + c * tk <= qcol_g, s, -jnp.inf)
        sc_ref[c] = s
        sc16_ref[c] = s.astype(BF16)
        return carry

    _chunk_loop(nchunks, score_chunk, 0, SCORE_UNROLL)

    def count_pass(pred):
        def body(c, acc):
            return acc + _fold_rows(jnp.where(pred(sc_ref[c], c), 1.0, 0.0), SUBLANES)
        acc = _chunk_loop(nchunks, body, jnp.zeros((SUBLANES, tq), F32), COUNT_UNROLL)
        return jnp.sum(acc, axis=0, keepdims=True)

    def count_pass16(cand16):
        one, zero = jnp.ones((), BF16), jnp.zeros((), BF16)

        def body(c, acc):
            return acc + _fold_rows(jnp.where(sc16_ref[c] >= cand16, one, zero), 2 * SUBLANES)
        acc = _chunk_loop(nchunks, body, jnp.zeros((2 * SUBLANES, tq), BF16), COUNT_UNROLL)
        return jnp.sum(acc.astype(F32), axis=0, keepdims=True)

    def bit_step16(it, cur):
        cand = cur | lax.shift_left(jnp.int32(1), 31 - it)
        cand16 = lax.bitcast_convert_type(_prefix_bf16_bits(cand), F32).astype(BF16)
        return jnp.where(count_pass16(cand16) >= float(topk), cand, cur)

    assert nkc * tk <= 256 * 2 * SUBLANES
    cur16 = lax.fori_loop(0, 16, bit_step16, jnp.zeros((1, tq), jnp.int32))
    short = (cur16 & jnp.int32(-2 ** 23)) == 0

    key_v = _float_bits_to_ordered(_prefix_bf16_bits(cur16))
    window = (key_v - (2 ** 15 + 1), key_v + 2 ** 16)

    def bisect_step(it, bounds):
        lo, hi = bounds
        mid = lo + ((hi - lo) >> 1)
        midf = _ordered_bits_to_float(mid)
        take = count_pass(lambda s, c: s >= midf) >= float(topk)
        return jnp.where(take, mid, lo), jnp.where(take, hi, mid)

    n_bisect = (2 ** 16 + 2 ** 15 + 1).bit_length()
    cur, _ = lax.fori_loop(0, n_bisect, bisect_step, window)
    thr = jnp.where(short, -jnp.inf, _ordered_bits_to_float(cur))
    cnt_gt = count_pass(lambda s, c: s > thr)
    n_tie = jnp.where(short, 0.0, float(topk) - cnt_gt)
    below = (lax.broadcasted_iota(jnp.int32, (tk, tk), 1)
             < lax.broadcasted_iota(jnp.int32, (tk, tk), 0)).astype(BF16)
    ones_row = jnp.ones((SUBLANES, tk), BF16)

    def mask_chunk(c, seen):
        s = sc_ref[c]
        tie = s == thr
        tie01 = jnp.where(tie, 1.0, 0.0).astype(BF16)
        rank = _dot(below, tie01) + seen
        sel = (s > thr) | (tie & (rank < n_tie))
        madd_ref[c] = jnp.where(sel, 0.0, NEG_BIG).astype(BF16)
        return seen + _dot(ones_row, tie01)[0:1, :]

    _chunk_loop(nchunks, mask_chunk, jnp.zeros((1, tq), F32), MASK_UNROLL)

    for h, qz in enumerate(pad_heads(qat_ref, N_HEADS_A // 2)):
        qz_ref[h] = qz
    m_ref[...] = jnp.full(m_ref.shape, NEG_BIG, F32)
    l_ref[...] = jnp.zeros(l_ref.shape, F32)
    acc_ref[...] = jnp.zeros(acc_ref.shape, F32)

    def attn_chunk(c, variant):
        k0 = pl.multiple_of(c * tk, tk)
        for h in range(N_HEADS_A):
            p = h // 2
            kp = ka_ref[0, pl.ds(k0, tk), p * LANES:(p + 1) * LANES]
            logits = _dot(kp, qz_ref[h]).astype(BF16) + madd_ref[c]
            if variant is not None:
                logits = logits + bias_ref[variant, h]
            hrow = slice(h, h + 1)
            m_old = m_ref[hrow, :]
            m_new = jnp.maximum(m_old, jnp.max(logits, axis=0, keepdims=True).astype(F32))
            alpha = jnp.exp2(m_old - m_new)
            pr = jnp.exp2(logits - m_new.astype(BF16))
            l_ref[hrow, :] = alpha * l_ref[hrow, :] + _dot(ones_row, pr)[0:1, :]
            m_ref[hrow, :] = m_new
            rows = slice(h * HEAD_DIM, (h + 1) * HEAD_DIM)
            acc_ref[rows, :] = alpha * acc_ref[rows, :] + _dot(vat_ref[0, c, rows, :], pr)

    first_near = i * ratio - 1
    n_far = jnp.maximum(first_near, 0)

    def far_group(j, carry):
        for u in range(FAR_UNROLL):
            attn_chunk(FAR_UNROLL * j + u, None)
        return carry

    def far_body(c, carry):
        attn_chunk(c, None)
        return carry

    n_grouped = (n_far // FAR_UNROLL) * FAR_UNROLL
    lax.fori_loop(0, n_far // FAR_UNROLL, far_group, 0)

    @pl.when(n_far - n_grouped >= 2)
    def _():
        attn_chunk(n_grouped, None)
        attn_chunk(n_grouped + 1, None)

    lax.fori_loop(n_grouped + 2 * ((n_far - n_grouped) // 2), n_far, far_body, 0)

    @pl.when(i > 0)
    def _():
        for v in range(ratio + 1):
            attn_chunk(first_near + v, v)

    @pl.when(i == 0)
    def _():
        for v in range(1, ratio + 1):
            attn_chunk(v - 1, v)

    outs = []
    for h in range(N_HEADS_A):
        rows = slice(h * HEAD_DIM, (h + 1) * HEAD_DIM)
        outs.append(acc_ref[rows, :] / l_ref[h:h + 1, :])
    o = jnp.concatenate(outs, axis=0).T
    out_ref[0] = (o * ga_ref[0].astype(F32)).astype(BF16)


def _dsa(qat, ka, vat, qit, ki, wit, ga, bias_a, tq, tk):
    b, s, _ = ka.shape
    topk = min(TOPK_MAX, s // 4)
    nq = s // tq
    nkc = s // tk
    qfeat = lambda c: pl.BlockSpec((1, c, tq), lambda bi, i: (bi, 0, i))
    qtok = lambda c: pl.BlockSpec((1, tq, c), lambda bi, i: (bi, i, 0))
    full = lambda c: pl.BlockSpec((1, s, c), lambda bi, i: (bi, 0, 0))
    return pl.pallas_call(
        functools.partial(_dsa_kernel, topk, tq, tk, nkc),
        grid=(b, nq),
        in_specs=[qfeat(D_A), full(D_A),
                  pl.BlockSpec((1, nkc, D_A, tk), lambda bi, i: (bi, 0, 0, 0)),
                  qfeat(D_QI), full(LANES), qfeat(WI_ROWS), qtok(D_A),
                  pl.BlockSpec(bias_a.shape, lambda bi, i: (0, 0, 0, 0), pipeline_mode=pl.Buffered(1))],
        out_specs=qtok(D_A),
        out_shape=jax.ShapeDtypeStruct((b, s, D_A), BF16),
        scratch_shapes=[pltpu.VMEM((nkc, tk, tq), F32),
                        pltpu.VMEM((nkc, tk, tq), BF16),
                        pltpu.VMEM((nkc, tk, tq), BF16),
                        pltpu.VMEM((N_HEADS_A, LANES, tq), BF16),
                        pltpu.VMEM((N_HEADS_A, tq), F32),
                        pltpu.VMEM((N_HEADS_A, tq), F32),
                        pltpu.VMEM((D_A, tq), F32)],
        compiler_params=pltpu.CompilerParams(dimension_semantics=("parallel", "arbitrary"),
                                             vmem_limit_bytes=VMEM_LIMIT),
        name="dsa",
    )(qat, ka, vat, qit, ki, wit, ga, bias_a)


def _dil_kernel(qb_blocks, nres, q_ref, kc_ref, kp_ref, vc_ref, vp_ref, bias_ref, o_ref, lse_ref):
    j = pl.program_id(2)
    lane = lax.broadcasted_iota(jnp.int32, (BLOCK, LANES), 1)
    low_half = lane < HEAD_DIM
    ones_cols = jnp.ones((2 * BLOCK, LANES), BF16)
    for rr, blk in [(rr, blk) for rr in range(nres) for blk in range(qb_blocks)]:
        rows = slice(blk * BLOCK, (blk + 1) * BLOCK)
        if blk == 0:
            variant = jnp.where(j == 0, 1, 0)
        else:
            variant = 0
        lse_tile = jnp.zeros((BLOCK, LANES), F32)
        for p in range(N_HEADS_B // 2):
            lanes = slice(rr * D_B + p * LANES, rr * D_B + (p + 1) * LANES)
            if blk == 0:
                kband = jnp.concatenate([kp_ref[0, :, lanes], kc_ref[0, 0:BLOCK, lanes]], axis=0)
                vband = jnp.concatenate([vp_ref[0, :, lanes], vc_ref[0, 0:BLOCK, lanes]], axis=0)
            else:
                band = slice((blk - 1) * BLOCK, (blk + 1) * BLOCK)
                kband = kc_ref[0, band, lanes]
                vband = vc_ref[0, band, lanes]
            vones = jnp.concatenate([vband, ones_cols], axis=1)
            qp = q_ref[0, rows, lanes]
            zero = jnp.zeros_like(qp)
            outs = []
            for e in range(2):
                h = 2 * p + e
                qz = jnp.where(low_half, qp, zero) if e == 0 else jnp.where(low_half, zero, qp)
                logits = _dot_nt(qz, kband).astype(BF16) + bias_ref[variant, h]
                m = jnp.max(logits, axis=1, keepdims=True)
                pr = jnp.exp2(logits - m)
                nd = _dot(pr, vones)
                den = nd[:, LANES:]
                outs.append(nd[:, :LANES] / den)
                lse_tile = lse_tile + jnp.where(lane == h, m.astype(F32) + jnp.log2(den), 0.0)
            o_ref[0, rows, lanes] = jnp.where(low_half, outs[0], outs[1]).astype(o_ref.dtype)
        lse_ref[0, rows, rr * LANES:(rr + 1) * LANES] = lse_tile


def _dilated_pattern(qv, kv, vv, bias_p, dil, qb_blocks, nres):
    b, n_sub, _ = qv.shape
    tu = qb_blocks * BLOCK
    nj = n_sub // tu
    cur = pl.BlockSpec((1, tu, nres * D_B), lambda bi, r, j: (bi, j, r))
    prev = pl.BlockSpec((1, BLOCK, nres * D_B), lambda bi, r, j: (bi, jnp.maximum(j * qb_blocks - 1, 0), r))
    return pl.pallas_call(
        functools.partial(_dil_kernel, qb_blocks, nres),
        grid=(b, dil // nres, nj),
        in_specs=[cur, cur, prev, cur, prev,
                  pl.BlockSpec((2, N_HEADS_B, BLOCK, 2 * BLOCK), lambda bi, r, j: (0, 0, 0, 0),
                               pipeline_mode=pl.Buffered(1))],
        out_specs=(pl.BlockSpec((1, tu, nres * D_B), lambda bi, r, j: (bi, j, r)),
                   pl.BlockSpec((1, tu, nres * LANES), lambda bi, r, j: (bi, j, r))),
        out_shape=(jax.ShapeDtypeStruct((b, n_sub, dil * D_B), BF16),
                   jax.ShapeDtypeStruct((b, n_sub, dil * LANES), F32)),
        compiler_params=pltpu.CompilerParams(dimension_semantics=("parallel", "parallel", "arbitrary"),
                                             vmem_limit_bytes=VMEM_LIMIT),
        name=f"dilated_d{dil}",
    )(qv, kv, kv, vv, vv, bias_p)


def _out_kernel(dils, x_ref, a_ref, gb_ref, w_ref, e_ref, *refs):
    npat = len(dils)
    o_refs = refs[0:npat]
    lse_refs = refs[npat:2 * npat]
    out_ref = refs[2 * npat]
    oslab_ref, lslab_ref = refs[2 * npat + 1:2 * npat + 3]
    tm = x_ref.shape[0]

    def natural_lse(j, lse_ref, d):
        if d == 1:
            return lse_ref[...]
        for r in range(d):
            lslab_ref[j, pl.ds(r, tm // d, stride=d), :] = lse_ref[:, r * LANES:(r + 1) * LANES]
        return lslab_ref[j]

    def natural_o(o_ref, d):
        if d == 1:
            return o_ref[...].astype(F32)
        for r in range(d):
            for pr in range(D_B // LANES):
                lane0 = r * D_B + pr * LANES
                oslab_ref[pr, pl.ds(r, tm // d, stride=d), :] = o_ref[:, lane0:lane0 + LANES].astype(F32)
        return jnp.concatenate([oslab_ref[pr] for pr in range(D_B // LANES)], axis=1)

    lses = [natural_lse(j, lse_ref, d) for j, (lse_ref, d) in enumerate(zip(lse_refs, dils))]
    top = functools.reduce(jnp.maximum, lses)
    ws = [jnp.exp2(l - top) for l in lses]
    tot = functools.reduce(lambda a, c: a + c, ws)
    e = e_ref[...]
    bmix = jnp.zeros((tm, D_B), F32)
    for wgt, o_ref, d in zip(ws, o_refs, dils):
        wn = wgt / tot
        hi = wn.astype(BF16)
        lo = (wn - hi.astype(F32)).astype(BF16)
        bmix = bmix + (_dot(hi, e) + _dot(lo, e)) * natural_o(o_ref, d)
    bg = (bmix * gb_ref[...].astype(F32)).astype(BF16)
    out_ref[...] = x_ref[...] + _dot(a_ref[...], w_ref[0:D_A, :]) + _dot(bg, w_ref[D_A:D_A + D_B, :])


def _out_proj(x2, a_g, gb, w_out, expand, os_, lses, dils, tm):
    n, d = x2.shape
    row = lambda i: (i, 0)
    const = lambda i: (0, 0)
    one_buf = dict(pipeline_mode=pl.Buffered(1))
    in_specs = ([pl.BlockSpec((tm, d), row), pl.BlockSpec((tm, D_A), row), pl.BlockSpec((tm, D_B), row),
                 pl.BlockSpec((D_A + D_B, d), const, **one_buf),
                 pl.BlockSpec((LANES, D_B), const, **one_buf)]
                + [pl.BlockSpec((tm // dl, dl * D_B), row) for dl in dils]
                + [pl.BlockSpec((tm // dl, dl * LANES), row) for dl in dils])
    return pl.pallas_call(
        functools.partial(_out_kernel, dils),
        grid=(n // tm,),
        in_specs=in_specs,
        out_specs=pl.BlockSpec((tm, d), row),
        out_shape=jax.ShapeDtypeStruct((n, d), F32),
        scratch_shapes=[pltpu.VMEM((D_B // LANES, tm, LANES), F32), pltpu.VMEM((len(dils), tm, LANES), F32)],
        compiler_params=pltpu.CompilerParams(dimension_semantics=("parallel",),
                                             vmem_limit_bytes=VMEM_LIMIT),
        name="merge_outproj",
    )(x2, a_g, gb, w_out, expand, *os_, *lses)


def _pack_w_in(w):
    qa, ka, va, za = (w[:, j * D_A:(j + 1) * D_A] for j in range(4))
    o = 4 * D_A
    qi = w[:, o:o + D_QI]
    ki = w[:, o + D_QI:o + D_QI + IDX_DIM]
    wi = w[:, o + D_QI + IDX_DIM:o + D_QI + IDX_DIM + N_IDX_HEADS]
    wb = w[:, o + D_QI + IDX_DIM + N_IDX_HEADS:]
    wn = jnp.concatenate([ka, za, ki, ki, wb], axis=1).astype(BF16)
    wi = jnp.pad(wi, ((0, 0), (0, WI_ROWS - N_IDX_HEADS)))
    wt = jnp.concatenate([qa, va, qi, wi], axis=1).T.astype(BF16)
    return wn, wt


def _bias_lookup(table, bucket):
    onehot = (bucket[..., None] == np.arange(N_BUCKETS)).astype(np.float32)
    return jnp.einsum('...n,nh->...h', onehot, table.astype(F32), precision=lax.Precision.HIGHEST)


def _dsa_bias_tiles(bias_a, tq, tk):
    k = np.arange(tk)[:, None]
    q = np.arange(tq)[None, :]
    far = bias_a[N_BUCKETS - 1].astype(F32)
    tiles = [_bias_lookup(bias_a, _rel_bucket(q - k + tk * (1 - v))) - far for v in range(1 + tq // tk)]
    return (jnp.transpose(jnp.stack(tiles), (0, 3, 1, 2)) * LOG2E).astype(BF16)


def _dil_bias_tiles(bias_b, window, dil):
    steps = window // dil
    a = np.arange(BLOCK)[:, None]
    c = np.arange(2 * BLOCK)[None, :]
    diff = a - c + BLOCK
    band = (diff >= 0) & (diff <= steps)
    bias = jnp.transpose(_bias_lookup(bias_b, _rel_bucket(diff * dil)), (2, 0, 1)) * LOG2E
    normal = jnp.where(band[None], bias, NEG_BIG)
    first = jnp.where((band & (c >= BLOCK))[None], bias, NEG_BIG)
    return jnp.stack([normal, first]).astype(BF16)


def kernel(x, norm_gain, w_in, w_out, rel_bias, q_norm_a, k_norm_a, q_norm_b, k_norm_b):
    b, s, d = x.shape
    n = b * s
    depth = norm_gain.shape[0]
    tq = min(DSA_TQ, s)
    tk = min(DSA_TK, s)
    tm_in = min(INPROJ_TM, s)
    tm_out = min(OUT_TM, s)
    bias_a = rel_bias[:, :N_HEADS_A]
    bias_b = rel_bias[:, N_HEADS_A:]
    eye_blocks = (jnp.arange(D_A)[:, None] // HEAD_DIM == jnp.arange(D_A)[None, :] // HEAD_DIM).astype(BF16)
    expand = (jnp.arange(LANES)[:, None] == jnp.arange(D_B)[None, :] // HEAD_DIM).astype(BF16)
    assert tk >= MAX_DISTANCE and tq % tk == 0
    dsa_bias = _dsa_bias_tiles(bias_a, tq, tk)
    dil_bias = [_dil_bias_tiles(bias_b, wnd, dil) for wnd, dil in DIL_PATTERNS]
    dils = tuple(dil for _, dil in DIL_PATTERNS)

    for layer in range(depth):
        gn = jnp.stack([jnp.tile(g[layer], N_HEADS_A) for g in (q_norm_a, k_norm_a, q_norm_b, k_norm_b)])
        wn, wt = _pack_w_in(w_in[layer])
        npat = len(dils)
        outs = _inproj(x, norm_gain[layer][None, :], wn, wt, eye_blocks, gn, gn[0][:, None], tm_in, tk, dils)
        qat, ka, vat, ga, qit, ki, wit, gb = outs[:8]
        qviews, kviews, vviews = (outs[8 + j * npat:8 + (j + 1) * npat] for j in range(3))
        a_g = _dsa(qat, ka, vat, qit, ki, wit, ga, dsa_bias, tq, tk)

        os_, lses = [], []
        for j, ((wnd, dil), bias_p) in enumerate(zip(DIL_PATTERNS, dil_bias)):
            assert wnd // dil == BLOCK and s % (dil * BLOCK) == 0 and tm_in % (dil * 2 * SUBLANES) == 0 and tm_out % (dil * SUBLANES) == 0
            n_sub = s // dil
            qb_blocks = min(DIL_BLOCKS_PER_STEP, n_sub // BLOCK)
            nres = max(1, min(dil, DIL_BLOCKS_PER_STEP // qb_blocks))
            o, lse = _dilated_pattern(qviews[j], kviews[j], vviews[j], bias_p, dil, qb_blocks, nres)
            os_.append(o.reshape(n // dil, dil * D_B))
            lses.append(lse.reshape(n // dil, dil * LANES))
        x = _out_proj(x.reshape(n, d), a_g.reshape(n, D_A), gb.reshape(n, D_B), w_out[layer].astype(BF16),
                      expand, os_, lses, dils, tm_out).reshape(b, s, d)
    return x
```
